```python
import jax, jax.numpy as jnp
from jax import lax
import numpy as np

D_MODEL = 1024
BATCH = 1
SEQ = 16384
DEPTH = 2

HEAD_DIM = 64
ROPE_DIM = HEAD_DIM // 4
ROPE_THETA = 500000.0
RMS_EPS = 1e-6
Q_BLOCK = 128
NEG_INF = -1e30
POS_BIG = 1e30

MOBA_HEADS = 6
MOBA_BLOCK = 256
MOBA_TOPK = 3

NSA_HEADS = 4
NSA_CMP_LEN = 32
NSA_CMP_STRIDE = 16
NSA_CMP_HIDDEN = 4 * HEAD_DIM
NSA_SLC_BLOCK = 64
NSA_SLC_TOPN = 16
NSA_WINDOW = 512

FOX_HEADS = 6
N_BRANCH = 3

MOBA_W = MOBA_HEADS * HEAD_DIM
NSA_W = NSA_HEADS * HEAD_DIM
FOX_W = FOX_HEADS * HEAD_DIM
IN_SPLITS = (MOBA_W,) * 4 + (NSA_W,) + (HEAD_DIM,) * 6 + (3 * NSA_HEADS, NSA_W) + (FOX_W,) * 3 + (FOX_HEADS, FOX_W, N_BRANCH * D_MODEL)
IN_COLS = sum(IN_SPLITS)
ATTN_SCALE = HEAD_DIM ** -0.5

kernel_name = "hybrid_moba_nsa_fox_gated"


def rms_norm(x, g):
    xf = x.astype(jnp.float32)
    y = xf * lax.rsqrt(jnp.mean(xf * xf, axis=-1, keepdims=True) + RMS_EPS)
    return (y * g.astype(jnp.float32)).astype(x.dtype)


def rope_tables(pos):
    inv = ROPE_THETA ** (-jnp.arange(0, ROPE_DIM, 2, dtype=jnp.float32) / ROPE_DIM)
    ang = pos.astype(jnp.float32)[:, None] * inv[None, :]
    return jnp.cos(ang), jnp.sin(ang)


def apply_partial_rope(x, cos, sin):
    half = ROPE_DIM // 2
    xr = x[..., :ROPE_DIM].astype(jnp.float32)
    x1, x2 = xr[..., :half], xr[..., half:]
    rot = jnp.concatenate([x1 * cos - x2 * sin, x2 * cos + x1 * sin], axis=-1)
    return jnp.concatenate([rot.astype(x.dtype), x[..., ROPE_DIM:]], axis=-1)


def masked_softmax(logits, mask):
    p = jax.nn.softmax(jnp.where(mask, logits.astype(jnp.float32), NEG_INF), axis=-1)
    return p * mask


def split_heads(t, n):
    B, S, _ = t.shape
    return t.reshape(B, S, n, HEAD_DIM).transpose(0, 2, 1, 3)


def merge_heads(t):
    B, H, S, dh = t.shape
    return t.transpose(0, 2, 1, 3).reshape(B, S, H * dh)


def sweep(fn, S):
    out = lax.map(fn, jnp.arange(S // Q_BLOCK))
    nq, B, H, Q, dh = out.shape
    return out.transpose(1, 2, 0, 3, 4).reshape(B, H, nq * Q, dh)


def moba_attention(q, k, v):
    B, H, S, dh = q.shape
    nb = -(-S // MOBA_BLOCK)
    pad = nb * MOBA_BLOCK - S
    kp = jnp.pad(k, ((0, 0), (0, 0), (0, pad), (0, 0)))
    vp = jnp.pad(v, ((0, 0), (0, 0), (0, pad), (0, 0)))
    kb = kp.reshape(B, H, nb, MOBA_BLOCK, dh)
    vb = vp.reshape(B, H, nb, MOBA_BLOCK, dh)
    kmean = jnp.mean(kb.astype(jnp.float32), axis=3).astype(k.dtype)
    n_sel = max(1, min(MOBA_TOPK, nb - 1))
    bi = jnp.arange(B)[:, None, None, None]
    hi = jnp.arange(H)[None, :, None, None]
    blk_ids = jnp.arange(nb)
    inner = jnp.arange(MOBA_BLOCK)

    def chunk(c):
        c0 = c * Q_BLOCK
        qc = lax.dynamic_slice_in_dim(q, c0, Q_BLOCK, axis=2)
        t = c0 + jnp.arange(Q_BLOCK)
        cur = t // MOBA_BLOCK
        gate = jnp.einsum('bhqd,bhnd->bhqn', qc, kmean).astype(jnp.float32)
        gate = jnp.where(blk_ids[None, :] < cur[:, None], gate, NEG_INF)
        _, idx = lax.top_k(gate, n_sel)
        sel_ok = jnp.arange(n_sel)[None, :] < cur[:, None]
        ks = kb[bi, hi, idx]
        vs = vb[bi, hi, idx]
        s_sel = jnp.einsum('bhqd,bhqnkd->bhqnk', qc, ks) * ATTN_SCALE
        m_sel = jnp.broadcast_to(sel_ok[:, :, None], (Q_BLOCK, n_sel, MOBA_BLOCK))
        own0 = (c0 // MOBA_BLOCK) * MOBA_BLOCK
        ko = lax.dynamic_slice_in_dim(kp, own0, MOBA_BLOCK, axis=2)
        vo = lax.dynamic_slice_in_dim(vp, own0, MOBA_BLOCK, axis=2)
        s_own = jnp.einsum('bhqd,bhkd->bhqk', qc, ko) * ATTN_SCALE
        m_own = (own0 + inner)[None, :] <= t[:, None]
        n_k = n_sel * MOBA_BLOCK
        logits = jnp.concatenate([s_sel.reshape(B, H, Q_BLOCK, n_k), s_own], axis=-1)
        mask = jnp.concatenate([m_sel.reshape(Q_BLOCK, n_k), m_own], axis=-1)
        p = masked_softmax(logits, mask).astype(v.dtype)
        p_sel = p[..., :n_k].reshape(B, H, Q_BLOCK, n_sel, MOBA_BLOCK)
        return (jnp.einsum('bhqnk,bhqnkd->bhqd', p_sel, vs)
                + jnp.einsum('bhqk,bhkd->bhqd', p[..., n_k:], vo))

    return sweep(chunk, S)


def nsa_compress(t, gidx, pe, w1, w2):
    B = t.shape[0]
    blocks = t[:, gidx] + pe
    flat = blocks.reshape(B, gidx.shape[0], NSA_CMP_LEN * HEAD_DIM)
    return jax.nn.silu(flat @ w1) @ w2


def nsa_attention(q, k_cmp, v_cmp, cmp_end, k_slc, v_slc, k_win, v_win, gates):
    B, H, S, dh = q.shape
    ns = S // NSA_SLC_BLOCK
    n_top = min(NSA_SLC_TOPN, ns)
    cmp_start = cmp_end - (NSA_CMP_LEN - 1)
    slc_start = jnp.arange(ns) * NSA_SLC_BLOCK
    overlap = ((cmp_start[:, None] <= slc_start[None, :] + NSA_SLC_BLOCK - 1)
               & (cmp_end[:, None] >= slc_start[None, :])).astype(jnp.float32)
    ksb = k_slc.reshape(B, ns, NSA_SLC_BLOCK, dh)
    vsb = v_slc.reshape(B, ns, NSA_SLC_BLOCK, dh)
    kwp = jnp.pad(k_win, ((0, 0), (NSA_WINDOW, 0), (0, 0)))
    vwp = jnp.pad(v_win, ((0, 0), (NSA_WINDOW, 0), (0, 0)))
    bi = jnp.arange(B)[:, None, None]
    blk = jnp.arange(ns)
    inner = jnp.arange(NSA_SLC_BLOCK)

    def chunk(c):
        c0 = c * Q_BLOCK
        qc = lax.dynamic_slice_in_dim(q, c0, Q_BLOCK, axis=2)
        t = c0 + jnp.arange(Q_BLOCK)
        s_c = jnp.einsum('bhqd,bnd->bhqn', qc, k_cmp) * ATTN_SCALE
        m_c = cmp_end[None, :] <= t[:, None]
        p_c = masked_softmax(s_c, m_c)
        o_c = jnp.einsum('bhqn,bnd->bhqd', p_c.astype(v_cmp.dtype), v_cmp)
        imp = jnp.einsum('bhqn,nm->bqm', p_c, overlap)
        cur = t // NSA_SLC_BLOCK
        score = jnp.where(blk[None, :] <= cur[:, None], imp, NEG_INF)
        forced = (blk[None, :] == 0) | (blk[None, :] == cur[:, None]) | (blk[None, :] == cur[:, None] - 1)
        score = jnp.where(forced, POS_BIG, score)
        _, idx = lax.top_k(score, n_top)
        ok = jnp.arange(n_top)[None, :] < (cur + 1)[:, None]
        ks = ksb[bi, idx]
        vs = vsb[bi, idx]
        s_s = jnp.einsum('bhqd,bqnkd->bhqnk', qc, ks) * ATTN_SCALE
        kpos = idx[..., None] * NSA_SLC_BLOCK + inner
        m_s = (kpos <= t[None, :, None, None]) & ok[None, :, :, None]
        n_k = n_top * NSA_SLC_BLOCK
        p_s = masked_softmax(s_s.reshape(B, H, Q_BLOCK, n_k), m_s.reshape(B, 1, Q_BLOCK, n_k))
        o_s = jnp.einsum('bhqnk,bqnkd->bhqd', p_s.reshape(B, H, Q_BLOCK, n_top, NSA_SLC_BLOCK).astype(v_slc.dtype), vs)
        kw = lax.dynamic_slice_in_dim(kwp, c0, NSA_WINDOW + Q_BLOCK, axis=1)
        vw = lax.dynamic_slice_in_dim(vwp, c0, NSA_WINDOW + Q_BLOCK, axis=1)
        wpos = c0 - NSA_WINDOW + jnp.arange(NSA_WINDOW + Q_BLOCK)
        diff = t[:, None] - wpos[None, :]
        m_w = (diff >= 0) & (diff < NSA_WINDOW) & (wpos[None, :] >= 0)
        s_w = jnp.einsum('bhqd,bkd->bhqk', qc, kw) * ATTN_SCALE
        o_w = jnp.einsum('bhqk,bkd->bhqd', masked_softmax(s_w, m_w).astype(v_win.dtype), vw)
        gc = lax.dynamic_slice_in_dim(gates, c0, Q_BLOCK, axis=2)
        return gc[..., 0:1] * o_c + gc[..., 1:2] * o_s + gc[..., 2:3] * o_w

    return sweep(chunk, S)


def fox_attention(q, k, v, log_f):
    B, H, S, dh = q.shape
    csum = jnp.cumsum(log_f, axis=-1)
    kpos = jnp.arange(S)

    def chunk(c):
        c0 = c * Q_BLOCK
        qc = lax.dynamic_slice_in_dim(q, c0, Q_BLOCK, axis=2)
        cq = lax.dynamic_slice_in_dim(csum, c0, Q_BLOCK, axis=2)
        t = c0 + jnp.arange(Q_BLOCK)
        s = (jnp.einsum('bhqd,bhkd->bhqk', qc, k).astype(jnp.float32) * ATTN_SCALE
             + cq[..., None] - csum[:, :, None, :])
        p = masked_softmax(s, kpos[None, :] <= t[:, None])
        return jnp.einsum('bhqk,bhkd->bhqd', p.astype(v.dtype), v)

    return sweep(chunk, S)


def hybrid_layer(x, norm_g, w_in, b_f, b_gate, moba_qk_g, nsa_q_g, nsa_k_g, fox_qk_g,
                 cmp_pe, cmp_w1, cmp_w2, w_up_moba, w_up_nsa, w_up_fox, w_out):
    B, S, _ = x.shape
    h = rms_norm(x, norm_g)
    proj = h @ w_in
    offsets = np.cumsum(IN_SPLITS)[:-1].tolist()
    (mq, mk, mv, mz, nq, kc, vc, ksl, vsl, kw, vw, ng, nz,
     fq, fk, fv, ff, fz, gl) = jnp.split(proj, offsets, axis=-1)
    pos = jnp.arange(S)
    cos, sin = rope_tables(pos)

    q = apply_partial_rope(rms_norm(split_heads(mq, MOBA_HEADS), moba_qk_g[0]), cos, sin)
    k = apply_partial_rope(rms_norm(split_heads(mk, MOBA_HEADS), moba_qk_g[1]), cos, sin)
    o_moba = merge_heads(moba_attention(q, k, split_heads(mv, MOBA_HEADS)))
    y_moba = (o_moba * jax.nn.silu(mz)) @ w_up_moba

    q = apply_partial_rope(rms_norm(split_heads(nq, NSA_HEADS), nsa_q_g), cos, sin)
    nc = (S - NSA_CMP_LEN) // NSA_CMP_STRIDE + 1
    starts = jnp.arange(nc) * NSA_CMP_STRIDE
    gidx = starts[:, None] + jnp.arange(NSA_CMP_LEN)[None, :]
    cmp_end = starts + NSA_CMP_LEN - 1
    k_cmp = nsa_compress(kc, gidx, cmp_pe[0], cmp_w1[0], cmp_w2[0])
    v_cmp = nsa_compress(vc, gidx, cmp_pe[1], cmp_w1[1], cmp_w2[1])
    cos_c, sin_c = rope_tables(cmp_end)
    k_cmp = apply_partial_rope(rms_norm(k_cmp, nsa_k_g[0]), cos_c, sin_c)
    ksl = apply_partial_rope(rms_norm(ksl, nsa_k_g[1]), cos, sin)
    kw = apply_partial_rope(rms_norm(kw, nsa_k_g[2]), cos, sin)
    gates = jax.nn.sigmoid(ng.reshape(B, S, NSA_HEADS, 3).transpose(0, 2, 1, 3))
    o_nsa = merge_heads(nsa_attention(q, k_cmp, v_cmp, cmp_end, ksl, vsl, kw, vw, gates))
    y_nsa = (o_nsa * jax.nn.silu(nz)) @ w_up_nsa

    q = rms_norm(split_heads(fq, FOX_HEADS), fox_qk_g[0])
    k = rms_norm(split_heads(fk, FOX_HEADS), fox_qk_g[1])
    log_f = jax.nn.log_sigmoid((ff + b_f).astype(jnp.float32)).transpose(0, 2, 1)
    o_fox = merge_heads(fox_attention(q, k, split_heads(fv, FOX_HEADS), log_f))
    y_fox = (o_fox * jax.nn.silu(fz)) @ w_up_fox

    g = jax.nn.sigmoid(gl + b_gate).reshape(B, S, N_BRANCH, D_MODEL)
    merged = g[:, :, 0] * y_moba + g[:, :, 1] * y_nsa + g[:, :, 2] * y_fox
    return x + merged @ w_out


def setup_inputs(seed: int = 0) -> dict:
    key = jax.random.key(seed)
    ks = jax.random.split(key, 16)
    f32 = jnp.float32

    def nrm(k, shape, scale):
        return jax.random.normal(k, shape, f32) * scale

    cmp_in = NSA_CMP_LEN * HEAD_DIM
    return {
        'x': nrm(ks[0], (BATCH, SEQ, D_MODEL), 1.0),
        'norm_g': 1.0 + nrm(ks[1], (DEPTH, D_MODEL), 0.02),
        'w_in': nrm(ks[2], (DEPTH, D_MODEL, IN_COLS), D_MODEL ** -0.5),
        'b_f': 3.0 + nrm(ks[3], (DEPTH, FOX_HEADS), 0.5),
        'b_gate': nrm(ks[4], (DEPTH, N_BRANCH * D_MODEL), 0.01),
        'moba_qk_g': 1.0 + nrm(ks[5], (DEPTH, 2, HEAD_DIM), 0.02),
        'nsa_q_g': 1.0 + nrm(ks[6], (DEPTH, HEAD_DIM), 0.02),
        'nsa_k_g': 1.0 + nrm(ks[7], (DEPTH, 3, HEAD_DIM), 0.02),
        'fox_qk_g': 1.0 + nrm(ks[8], (DEPTH, 2, HEAD_DIM), 0.02),
        'cmp_pe': nrm(ks[9], (DEPTH, 2, NSA_CMP_LEN, HEAD_DIM), 0.1),
        'cmp_w1': nrm(ks[10], (DEPTH, 2, cmp_in, NSA_CMP_HIDDEN), cmp_in ** -0.5),
        'cmp_w2': nrm(ks[11], (DEPTH, 2, NSA_CMP_HIDDEN, HEAD_DIM), NSA_CMP_HIDDEN ** -0.5),
        'w_up_moba': nrm(ks[12], (DEPTH, MOBA_W, D_MODEL), MOBA_W ** -0.5),
        'w_up_nsa': nrm(ks[13], (DEPTH, NSA_W, D_MODEL), NSA_W ** -0.5),
        'w_up_fox': nrm(ks[14], (DEPTH, FOX_W, D_MODEL), FOX_W ** -0.5),
        'w_out': nrm(ks[15], (DEPTH, D_MODEL, D_MODEL), D_MODEL ** -0.5),
    }


def reference(x, norm_g, w_in, b_f, b_gate, moba_qk_g, nsa_q_g, nsa_k_g, fox_qk_g,
              cmp_pe, cmp_w1, cmp_w2, w_up_moba, w_up_nsa, w_up_fox, w_out):
    for l in range(DEPTH):
        x = hybrid_layer(x, norm_g[l], w_in[l], b_f[l], b_gate[l], moba_qk_g[l], nsa_q_g[l],
                         nsa_k_g[l], fox_qk_g[l], cmp_pe[l], cmp_w1[l], cmp_w2[l],
                         w_up_moba[l], w_up_nsa[l], w_up_fox[l], w_out[l])
    return x
```

```python
import functools

import numpy as np
import jax
import jax.numpy as jnp
from jax import lax
from jax.experimental import pallas as pl
from jax.experimental.pallas import tpu as pltpu

D_MODEL = 1024
HEAD_DIM = 64
ROPE_DIM = HEAD_DIM // 4
ROPE_THETA = 500000.0
RMS_EPS = 1e-6
NEG_INF = -1e30
POS_BIG = 1e30
M_FLOOR = -1e20

MOBA_HEADS = 6
MOBA_BLOCK = 256
MOBA_TOPK = 3
NSA_HEADS = 4
NSA_CMP_LEN = 32
NSA_CMP_STRIDE = 16
NSA_CMP_HIDDEN = 4 * HEAD_DIM
NSA_SLC_BLOCK = 64
NSA_SLC_TOPN = 16
NSA_WINDOW = 512
FOX_HEADS = 6
N_BRANCH = 3
MOBA_W = MOBA_HEADS * HEAD_DIM
NSA_W = NSA_HEADS * HEAD_DIM
FOX_W = FOX_HEADS * HEAD_DIM
IN_SPLITS = (MOBA_W,) * 4 + (NSA_W,) + (HEAD_DIM,) * 6 + (3 * NSA_HEADS, NSA_W) + (FOX_W,) * 3 + (FOX_HEADS, FOX_W, N_BRANCH * D_MODEL)
ATTN_SCALE = HEAD_DIM ** -0.5

LANES = 128
AUG = 2 * HEAD_DIM
CODE_BLOCKS = AUG - HEAD_DIM
Q_BLOCK = 128
N_ZHEADS = MOBA_HEADS + NSA_HEADS + FOX_HEADS

SEG_A = 2 * MOBA_W + NSA_W + 2 * HEAD_DIM + 2 * FOX_W
SEG_B = 2 * MOBA_W + 4 * HEAD_DIM
SEG_C = MOBA_W + NSA_W + FOX_W
SEG_D = LANES
OFF_B = SEG_A
OFF_C = OFF_B + SEG_B
OFF_D = OFF_C + SEG_C
MAIN_COLS = OFF_D + SEG_D
FF_LANE = 3 * NSA_HEADS

VMEM_LIMIT = 56 * 1024 * 1024

f32 = jnp.float32
bf16 = jnp.bfloat16


def _cparams(sem):
    return pltpu.CompilerParams(dimension_semantics=sem, vmem_limit_bytes=VMEM_LIMIT)


def _iota2(shape, dim):
    return lax.broadcasted_iota(jnp.int32, shape, dim)


def _div(x, d):
    return jnp.right_shift(x, int(d).bit_length() - 1)


def _mod(x, d):
    return jnp.bitwise_and(x, d - 1)


def _place(n_in, n_out, in_off, out_off, width=HEAD_DIM, val=1.0):
    r = _iota2((n_in, n_out), 0) - in_off
    c = _iota2((n_in, n_out), 1) - out_off
    hit = (r == c) & (r >= 0) & (r < width)
    return jnp.where(hit, val, 0.0).astype(bf16)


def _split3(x):
    hi = x.astype(bf16)
    r = x - hi.astype(f32)
    mid = r.astype(bf16)
    lo = (r - mid.astype(f32)).astype(bf16)
    return hi, mid, lo


def _dot(a, b):
    return jnp.dot(a, b, preferred_element_type=f32)


def _dot_nt(a, b):
    return lax.dot_general(a, b, (((1,), (1,)), ((), ())), preferred_element_type=f32)


def _rms_matmul_kernel(x_ref, g_ref, w_ref, o_ref, h_sc):
    @pl.when(pl.program_id(1) == 0)
    def _():
        x = x_ref[...]
        ms = jnp.mean(x * x, axis=-1, keepdims=True)
        h_sc[...] = (x * lax.rsqrt(ms + RMS_EPS) * g_ref[...]).astype(bf16)

    o_ref[...] = _dot(h_sc[...], w_ref[...])


def rms_matmul(x, g, w, tm=512, tn=512):
    S, D = x.shape
    N = w.shape[1]
    return pl.pallas_call(
        _rms_matmul_kernel,
        grid=(S // tm, N // tn),
        in_specs=[pl.BlockSpec((tm, D), lambda i, j: (i, 0)),
                  pl.BlockSpec((1, D), lambda i, j: (0, 0)),
                  pl.BlockSpec((D, tn), lambda i, j: (0, j))],
        out_specs=pl.BlockSpec((tm, tn), lambda i, j: (i, j)),
        out_shape=jax.ShapeDtypeStruct((S, N), f32),
        scratch_shapes=[pltpu.VMEM((tm, D), bf16)],
        compiler_params=_cparams(("arbitrary", "arbitrary")),
        name="rms_matmul",
    )(x, g, w)


def _prep_kernel(p_ref, cos_ref, sin_ref, gain_ref, bf_ref,
                 mq_ref, mk_ref, nq_ref, ksl_ref, kw_ref, fq_ref, fk_ref,
                 mv_ref, fv_ref, vsl_ref, vw_ref, zs_ref, carry_sc, *, ts):
    i = pl.program_id(0)

    @pl.when(i == 0)
    def _():
        carry_sc[...] = jnp.zeros_like(carry_sc)

    lane = _iota2((ts, LANES), 1)
    pos = _iota2((ts, LANES), 0) + i * ts
    blockdiag = jnp.where(_div(_iota2((LANES, LANES), 0), HEAD_DIM) == _div(_iota2((LANES, LANES), 1), HEAD_DIM),
                          1.0, 0.0).astype(bf16)
    to64 = [_place(LANES, HEAD_DIM, 0, 0), _place(LANES, HEAD_DIM, HEAD_DIM, 0)]
    to128 = [_place(LANES, AUG, 0, 0), _place(LANES, AUG, HEAD_DIM, 0)]
    to128_scaled = [_place(LANES, AUG, 0, 0, val=ATTN_SCALE), _place(LANES, AUG, HEAD_DIM, 0, val=ATTN_SCALE)]
    first_half = _mod(lane, HEAD_DIM) < (ROPE_DIM // 2)
    cos = cos_ref[...]
    sin = sin_ref[...]

    def normed(c, rope):
        x = p_ref[:, c * LANES:(c + 1) * LANES]
        x2 = x * x
        hi = x2.astype(bf16)
        lo = (x2 - hi.astype(f32)).astype(bf16)
        ss = _dot(hi, blockdiag) + _dot(lo, blockdiag)
        y = x * lax.rsqrt(ss * (1.0 / HEAD_DIM) + RMS_EPS) * gain_ref[:, c * LANES:(c + 1) * LANES]
        if rope:
            up = pltpu.roll(y, LANES - ROPE_DIM // 2, 1)
            dn = pltpu.roll(y, ROPE_DIM // 2, 1)
            y = y * cos + jnp.where(first_half, up, dn) * sin
        return y.astype(bf16)

    d = p_ref[:, OFF_D:OFF_D + LANES] + bf_ref[...]
    logf = jnp.minimum(d, 0.0) - jnp.log(1.0 + jnp.exp(-jnp.abs(d)))
    tri = jnp.where(_iota2((ts, ts), 1) <= _iota2((ts, ts), 0), 1.0, 0.0).astype(bf16)
    lh, lm, ll = _split3(logf)
    c = carry_sc[0:1, :] + (_dot(tri, lh) + _dot(tri, lm) + _dot(tri, ll))
    carry_sc[...] = jnp.broadcast_to(c[ts - 1:ts, :], carry_sc.shape)
    ch, cm, cl = _split3(c)
    one_q = jnp.where((lane >= HEAD_DIM + 3) & (lane < HEAD_DIM + 6), 1.0, 0.0)
    one_k = jnp.where((lane >= HEAD_DIM) & (lane < HEAD_DIM + 3), 1.0, 0.0)

    def decay_cols(h, base):
        src = FF_LANE + h
        return (_dot(ch, _place(LANES, AUG, src, base, width=1))
                + _dot(cm, _place(LANES, AUG, src, base + 1, width=1))
                + _dot(cl, _place(LANES, AUG, src, base + 2, width=1)))

    moba_code = jnp.where((lane >= HEAD_DIM) & ((lane - HEAD_DIM) == _div(pos, MOBA_BLOCK)), 1.0, 0.0)
    for c_i in range(3):
        yq = normed(c_i, True)
        yk = normed(3 + c_i, True)
        for half in range(2):
            h = 2 * c_i + half
            mq_ref[h] = _dot(yq, to64[half]).astype(bf16)
            mk_ref[h] = (_dot(yk, to128[half]) + moba_code).astype(bf16)
    for c_i in range(2):
        y = normed(6 + c_i, True)
        for half in range(2):
            h = 2 * c_i + half
            yh = _dot(y, to64[half]).astype(bf16)
            for qb in range(ts // Q_BLOCK):
                nq_ref[(qb * NSA_HEADS + h) * Q_BLOCK:(qb * NSA_HEADS + h + 1) * Q_BLOCK, :] = (
                    yh[qb * Q_BLOCK:(qb + 1) * Q_BLOCK, :])
    y = normed(8, True)
    slc_code = jnp.where((lane >= HEAD_DIM) & ((lane - HEAD_DIM) == _mod(_div(pos, NSA_SLC_BLOCK), CODE_BLOCKS)), 1.0, 0.0)
    ksl_ref[...] = (_dot(y, to128[0]) + slc_code).astype(bf16)
    kw_ref[...] = _dot(y, to128[1]).astype(bf16)
    for c_i in range(3):
        yq = normed(9 + c_i, False)
        yk = normed(12 + c_i, False)
        for half in range(2):
            h = 2 * c_i + half
            fq_ref[h] = (_dot(yq, to128_scaled[half]) + decay_cols(h, HEAD_DIM) + one_q).astype(bf16)
            fk_ref[h] = (_dot(yk, to128[half]) - decay_cols(h, HEAD_DIM + 3) + one_k).astype(bf16)
    for c_i in range(3):
        xm = p_ref[:, OFF_B + c_i * LANES:OFF_B + (c_i + 1) * LANES].astype(bf16)
        xf = p_ref[:, OFF_B + MOBA_W + c_i * LANES:OFF_B + MOBA_W + (c_i + 1) * LANES].astype(bf16)
        for half in range(2):
            mv_ref[2 * c_i + half] = _dot(xm, to64[half]).astype(bf16)
            fv_ref[2 * c_i + half] = _dot(xf, to64[half]).astype(bf16)
    xs = p_ref[:, OFF_B + 2 * MOBA_W:OFF_B + 2 * MOBA_W + LANES].astype(bf16)
    vsl_ref[...] = _dot(xs, to64[0]).astype(bf16)
    vw_ref[...] = _dot(xs, to64[1]).astype(bf16)
    for c_i in range(SEG_C // LANES):
        z = p_ref[:, OFF_C + c_i * LANES:OFF_C + (c_i + 1) * LANES]
        zs = (z * (1.0 / (1.0 + jnp.exp(-z)))).astype(bf16)
        for half in range(2):
            zs_ref[2 * c_i + half] = _dot(zs, to64[half]).astype(bf16)


def prep(proj_main, cos_t, sin_t, gain_row, bf_row, ts=256):
    S = proj_main.shape[0]
    head64 = lambda n: jax.ShapeDtypeStruct((n, S, HEAD_DIM), bf16)
    head128 = lambda n: jax.ShapeDtypeStruct((n, S, AUG), bf16)
    spec_h = lambda n, w: pl.BlockSpec((n, ts, w), lambda i: (0, i, 0))
    spec_r = lambda w: pl.BlockSpec((ts, w), lambda i: (i, 0))
    out_shape = (head64(MOBA_HEADS), head128(MOBA_HEADS),
                 jax.ShapeDtypeStruct((NSA_HEADS * S, HEAD_DIM), bf16),
                 jax.ShapeDtypeStruct((S, AUG), bf16), jax.ShapeDtypeStruct((S, AUG), bf16),
                 head128(FOX_HEADS), head128(FOX_HEADS),
                 head64(MOBA_HEADS), head64(FOX_HEADS),
                 jax.ShapeDtypeStruct((S, HEAD_DIM), bf16), jax.ShapeDtypeStruct((S, HEAD_DIM), bf16),
                 head64(N_ZHEADS))
    out_specs = (spec_h(MOBA_HEADS, HEAD_DIM), spec_h(MOBA_HEADS, AUG),
                 pl.BlockSpec((NSA_HEADS * ts, HEAD_DIM), lambda i: (i, 0)),
                 spec_r(AUG), spec_r(AUG),
                 spec_h(FOX_HEADS, AUG), spec_h(FOX_HEADS, AUG),
                 spec_h(MOBA_HEADS, HEAD_DIM), spec_h(FOX_HEADS, HEAD_DIM),
                 spec_r(HEAD_DIM), spec_r(HEAD_DIM),
                 spec_h(N_ZHEADS, HEAD_DIM))
    return pl.pallas_call(
        functools.partial(_prep_kernel, ts=ts),
        grid=(S // ts,),
        in_specs=[pl.BlockSpec((ts, MAIN_COLS), lambda i: (i, 0)),
                  spec_r(LANES), spec_r(LANES),
                  pl.BlockSpec((1, SEG_A), lambda i: (0, 0)),
                  pl.BlockSpec((1, LANES), lambda i: (0, 0))],
        out_specs=out_specs,
        out_shape=out_shape,
        scratch_shapes=[pltpu.VMEM((8, LANES), f32)],
        compiler_params=_cparams(("arbitrary",)),
        name="prep",
    )(proj_main, cos_t, sin_t, gain_row, bf_row)


def _kmean_kernel(k_ref, o_ref, *, rows):
    n = rows // MOBA_BLOCK
    avg = jnp.where(_div(_iota2((n, rows), 1), MOBA_BLOCK) == _iota2((n, rows), 0),
                    1.0 / MOBA_BLOCK, 0.0).astype(bf16)
    o_ref[0] = _dot(avg, k_ref[0])[:, :HEAD_DIM]


def moba_kmean(mk_aug):
    H, S, _ = mk_aug.shape
    rows = 8 * MOBA_BLOCK
    return pl.pallas_call(
        functools.partial(_kmean_kernel, rows=rows),
        grid=(H, S // rows),
        in_specs=[pl.BlockSpec((1, rows, AUG), lambda h, i: (h, i, 0))],
        out_specs=pl.BlockSpec((1, 8, HEAD_DIM), lambda h, i: (h, i, 0)),
        out_shape=jax.ShapeDtypeStruct((H, CODE_BLOCKS, HEAD_DIM), f32),
        compiler_params=_cparams(("arbitrary", "arbitrary")),
        name="moba_kmean",
    )(mk_aug)


def _top_select(score, idx, n_pick, floor):
    big = jnp.int32(2 ** 30)
    sel = jnp.zeros(score.shape, f32)
    for _ in range(n_pick):
        m = jnp.max(score, axis=1, keepdims=True)
        first = jnp.min(jnp.where(score == m, idx, big), axis=1, keepdims=True)
        pick = (idx == first) & (m > floor)
        sel = jnp.where(pick, 1.0, sel)
        score = jnp.where(pick, -jnp.inf, score)
    return sel


def _moba_select_kernel(q_ref, km_ref, o_ref, *, tq):
    i = pl.program_id(1)
    q = q_ref[0]
    km = km_ref[0]
    km_hi = km.astype(bf16)
    km_lo = (km - km_hi.astype(f32)).astype(bf16)
    gate = _dot_nt(q, km_hi) + _dot_nt(q, km_lo)
    blk = _iota2((tq, CODE_BLOCKS), 1)
    cur = _div(_iota2((tq, CODE_BLOCKS), 0) + i * tq, MOBA_BLOCK)
    score = jnp.where(blk < cur, gate, NEG_INF)
    sel = _top_select(score, blk, MOBA_TOPK, NEG_INF)
    bias = jnp.where((sel > 0.0) | (blk == cur), 0.0, NEG_INF).astype(bf16)
    o_ref[0, 0] = (_dot(q, _place(HEAD_DIM, AUG, 0, 0, val=ATTN_SCALE))
                   + _dot(bias, _place(CODE_BLOCKS, AUG, 0, HEAD_DIM))).astype(bf16)


def moba_select(mq, kmean, tq=512):
    H, S, _ = mq.shape
    return pl.pallas_call(
        functools.partial(_moba_select_kernel, tq=tq),
        grid=(H, S // tq),
        in_specs=[pl.BlockSpec((1, tq, HEAD_DIM), lambda h, i: (h, i, 0)),
                  pl.BlockSpec((1, CODE_BLOCKS, HEAD_DIM), lambda h, i: (h, 0, 0))],
        out_specs=pl.BlockSpec((1, 1, tq, AUG), lambda h, i: (h, 0, i, 0)),
        out_shape=jax.ShapeDtypeStruct((H, 1, S, AUG), bf16),
        compiler_params=_cparams(("arbitrary", "arbitrary")),
        name="moba_select",
    )(mq, kmean)


def _cmp_kernel(tc_ref, pe_ref, w1_ref, w2_ref, gain_ref, cos_ref, sin_ref, kc_ref, vc_ref, *, ncp):
    half_w = NSA_CMP_STRIDE * HEAD_DIM

    def mlp(b):
        t = tc_ref[b]
        top = (t + pe_ref[b, 0:1, :]).astype(bf16)
        bot = (t + pe_ref[b, 1:2, :]).astype(bf16)
        a = _dot(top, w1_ref[b, 0:half_w, :])
        nxt = _dot(bot, w1_ref[b, half_w:2 * half_w, :])
        hid = a + pltpu.roll(nxt, ncp - 1, 0)
        act = hid * (1.0 / (1.0 + jnp.exp(-hid)))
        return _dot(act.astype(bf16), w2_ref[b])

    k = mlp(0)
    ms = jnp.sum(k * k, axis=-1, keepdims=True) * (1.0 / HEAD_DIM)
    y = k * lax.rsqrt(ms + RMS_EPS) * gain_ref[...]
    lane = _iota2((ncp, LANES), 1)
    up = pltpu.roll(y, LANES - ROPE_DIM // 2, 1)
    dn = pltpu.roll(y, ROPE_DIM // 2, 1)
    y = y * cos_ref[...] + jnp.where(lane < ROPE_DIM // 2, up, dn) * sin_ref[...]
    kc_ref[...] = y[:, :HEAD_DIM].astype(bf16)
    vc_ref[...] = mlp(1)[:, :HEAD_DIM].astype(bf16)


def nsa_compress(tc, pe, w1, w2, gain, cos_c, sin_c):
    ncp = tc.shape[1]
    return pl.pallas_call(
        functools.partial(_cmp_kernel, ncp=ncp),
        out_shape=(jax.ShapeDtypeStruct((ncp, HEAD_DIM), bf16), jax.ShapeDtypeStruct((ncp, HEAD_DIM), bf16)),
        compiler_params=pltpu.CompilerParams(vmem_limit_bytes=VMEM_LIMIT),
        name="nsa_compress",
    )(tc, pe, w1, w2, gain, cos_c, sin_c)


def _nsa_cmp_select_kernel(q_ref, kc_ref, vc_ref, oc_ref, qa_ref, *, ncp, nsp, n_win):
    i = pl.program_id(0)
    rows = NSA_HEADS * Q_BLOCK
    q = q_ref[...]
    s = _dot_nt(q, kc_ref[...]) * ATTN_SCALE
    qpos = i * Q_BLOCK + _mod(_iota2((rows, ncp), 0), Q_BLOCK)
    kend = _iota2((rows, ncp), 1) * NSA_CMP_STRIDE + (NSA_CMP_LEN - 1)
    ok = kend <= qpos
    sm = jnp.where(ok, s, NEG_INF)
    e = jnp.exp(sm - jnp.max(sm, axis=1, keepdims=True))
    p = jnp.where(ok, e * (1.0 / jnp.sum(e, axis=1, keepdims=True)), 0.0)
    oc_ref[...] = _dot(p.astype(bf16), vc_ref[...]).astype(bf16)
    psum = p[0:Q_BLOCK] + p[Q_BLOCK:2 * Q_BLOCK] + p[2 * Q_BLOCK:3 * Q_BLOCK] + p[3 * Q_BLOCK:4 * Q_BLOCK]
    c0 = _iota2((ncp, nsp), 0) * NSA_CMP_STRIDE
    b0 = _iota2((ncp, nsp), 1) * NSA_SLC_BLOCK
    overlap = jnp.where((c0 <= b0 + (NSA_SLC_BLOCK - 1)) & (c0 + (NSA_CMP_LEN - 1) >= b0), 1.0, 0.0).astype(bf16)
    ph, pm, plo = _split3(psum)
    imp = _dot(ph, overlap) + _dot(pm, overlap) + _dot(plo, overlap)
    blk = _iota2((Q_BLOCK, nsp), 1)
    cur = _div(_iota2((Q_BLOCK, nsp), 0) + i * Q_BLOCK, NSA_SLC_BLOCK)
    score = jnp.where(blk <= cur, imp, NEG_INF)
    forced = (blk == 0) | (blk == cur) | (blk == cur - 1)
    score = jnp.where(forced, POS_BIG, score)
    sel = _top_select(score, blk, NSA_SLC_TOPN, NEG_INF)
    bias = jnp.where(sel > 0.0, 0.0, NEG_INF).astype(bf16)
    qs = _dot(q, _place(HEAD_DIM, AUG, 0, 0, val=ATTN_SCALE))
    for w in range(n_win):
        bw = _dot(bias, _place(nsp, AUG, w * CODE_BLOCKS, HEAD_DIM, width=CODE_BLOCKS))
        for h in range(NSA_HEADS):
            qa_ref[0, w, h * Q_BLOCK:(h + 1) * Q_BLOCK, :] = (qs[h * Q_BLOCK:(h + 1) * Q_BLOCK, :] + bw).astype(bf16)


def nsa_cmp_select(nq, kc, vc, S):
    ncp = kc.shape[0]
    ns = S // NSA_SLC_BLOCK
    nsp = max(LANES, ns)
    n_win = max(1, ns // CODE_BLOCKS)
    rows = NSA_HEADS * Q_BLOCK
    return pl.pallas_call(
        functools.partial(_nsa_cmp_select_kernel, ncp=ncp, nsp=nsp, n_win=n_win),
        grid=(S // Q_BLOCK,),
        in_specs=[pl.BlockSpec((rows, HEAD_DIM), lambda i: (i, 0)),
                  pl.BlockSpec((ncp, HEAD_DIM), lambda i: (0, 0)),
                  pl.BlockSpec((ncp, HEAD_DIM), lambda i: (0, 0))],
        out_specs=(pl.BlockSpec((rows, HEAD_DIM), lambda i: (i, 0)),
                   pl.BlockSpec((1, n_win, rows, AUG), lambda i: (0, 0, i, 0))),
        out_shape=(jax.ShapeDtypeStruct((NSA_HEADS * S, HEAD_DIM), bf16),
                   jax.ShapeDtypeStruct((1, n_win, NSA_HEADS * S, AUG), bf16)),
        compiler_params=_cparams(("arbitrary",)),
        name="nsa_cmp_select",
    )(nq, kc, vc)


def _flash_kernel(qi_t, ki_t, win_t, first_t, last_t, q_ref, k_ref, v_ref, o_ref, m_sc, l_sc, acc_sc,
                  *, tq, tk, pos_block, window):
    p_id = pl.program_id(1)
    qi = qi_t[p_id]
    ki = ki_t[p_id]

    @pl.when(first_t[p_id] == 1)
    def _():
        m_sc[...] = jnp.full(m_sc.shape, M_FLOOR, f32)
        l_sc[...] = jnp.zeros_like(l_sc)
        acc_sc[...] = jnp.zeros_like(acc_sc)

    s = _dot_nt(q_ref[0, 0], k_ref[0])
    row = _iota2((tq, tk), 0)
    col = _iota2((tq, tk), 1)
    delta = (qi * pos_block - ki * tk) + _mod(row, pos_block) - col
    if window is None:
        ok = delta >= 0
    else:
        ok = delta.astype(jnp.uint32) < jnp.uint32(window)
    s = jnp.where(ok, s, NEG_INF)
    m_prev = m_sc[:, 0:1]
    m_new = jnp.maximum(m_prev, jnp.max(s, axis=1, keepdims=True))
    alpha = jnp.exp(m_prev - m_new)
    p = jnp.exp(s - m_new)
    l_new = alpha * l_sc[:, 0:1] + jnp.sum(p, axis=1, keepdims=True)
    acc_sc[...] = alpha * acc_sc[...] + _dot(p.astype(bf16), v_ref[0])
    m_sc[...] = jnp.broadcast_to(m_new, m_sc.shape)
    l_sc[...] = jnp.broadcast_to(l_new, l_sc.shape)

    @pl.when(last_t[p_id] == 1)
    def _():
        l = l_sc[:, 0:1]
        inv = jnp.where(l > 0.0, 1.0 / l, 0.0)
        o_ref[0] = (acc_sc[...] * inv).astype(o_ref.dtype)


def _pair_tables(n_q, k_range, tk, win_keys):
    qi, ki, wi, first, last = [], [], [], [], []
    for a in range(n_q):
        lo, hi = k_range(a)
        for b in range(lo, hi + 1):
            qi.append(a)
            ki.append(b)
            wi.append((b * tk) // win_keys)
            first.append(int(b == lo))
            last.append(int(b == hi))
    return tuple(jnp.asarray(np.asarray(t, np.int32)) for t in (qi, ki, wi, first, last))


def flash(q, k, v, tables, *, tq, tk, pos_block, window, name):
    G, _, Sq, _ = q.shape
    n_pairs = tables[0].shape[0]
    grid_spec = pltpu.PrefetchScalarGridSpec(
        num_scalar_prefetch=5,
        grid=(G, n_pairs),
        in_specs=[pl.BlockSpec((1, 1, tq, AUG), lambda g, p, qi, ki, wi, fi, la: (g, wi[p], qi[p], 0)),
                  pl.BlockSpec((1, tk, AUG), lambda g, p, qi, ki, wi, fi, la: (g, ki[p], 0)),
                  pl.BlockSpec((1, tk, HEAD_DIM), lambda g, p, qi, ki, wi, fi, la: (g, ki[p], 0))],
        out_specs=pl.BlockSpec((1, tq, HEAD_DIM), lambda g, p, qi, ki, wi, fi, la: (g, qi[p], 0)),
        scratch_shapes=[pltpu.VMEM((tq, LANES), f32), pltpu.VMEM((tq, LANES), f32), pltpu.VMEM((tq, HEAD_DIM), f32)],
    )
    return pl.pallas_call(
        functools.partial(_flash_kernel, tq=tq, tk=tk, pos_block=pos_block, window=window),
        grid_spec=grid_spec,
        out_shape=jax.ShapeDtypeStruct((G, Sq, HEAD_DIM), bf16),
        compiler_params=_cparams(("arbitrary", "arbitrary")),
        name=name,
    )(*tables, q, k, v)


def _out_kernel(x_ref, gl_ref, sd_ref, bg_ref, om_ref, of_ref, oc_ref, os_ref, ow_ref, zs_ref,
                wm_ref, wn_ref, wf_ref, wo_ref, o_ref, *, tm):
    def up(o_of_head, n_heads, z_off, w_ref):
        y = jnp.zeros((tm, D_MODEL), f32)
        for h in range(n_heads):
            g = (o_of_head(h) * zs_ref[z_off + h].astype(f32)).astype(bf16)
            y = y + _dot(g, w_ref[h])
        return y

    sg = 1.0 / (1.0 + jnp.exp(-sd_ref[...]))

    def nsa_head(h):
        rows = slice(h * Q_BLOCK, (h + 1) * Q_BLOCK)
        return (sg[:, 3 * h:3 * h + 1] * oc_ref[rows, :].astype(f32)
                + sg[:, 3 * h + 1:3 * h + 2] * os_ref[rows, :].astype(f32)
                + sg[:, 3 * h + 2:3 * h + 3] * ow_ref[rows, :].astype(f32))

    y_m = up(lambda h: om_ref[h].astype(f32), MOBA_HEADS, 0, wm_ref)
    y_n = up(nsa_head, NSA_HEADS, MOBA_HEADS, wn_ref)
    y_f = up(lambda h: of_ref[h].astype(f32), FOX_HEADS, MOBA_HEADS + NSA_HEADS, wf_ref)
    merged = jnp.zeros((tm, D_MODEL), f32)
    for b, y in enumerate((y_m, y_n, y_f)):
        cols = slice(b * D_MODEL, (b + 1) * D_MODEL)
        merged = merged + y * (1.0 / (1.0 + jnp.exp(-(gl_ref[:, cols] + bg_ref[:, cols]))))
    o_ref[...] = x_ref[...] + _dot(merged.astype(bf16), wo_ref[...])


def out_proj(x, proj_gl, proj_main, b_gate, o_m, o_f, o_c, o_s, o_w, zs, w_m, w_n, w_f, w_o):
    S = x.shape[0]
    tm = Q_BLOCK
    rows = NSA_HEADS * Q_BLOCK
    full = lambda a: pl.BlockSpec(a.shape, lambda i: (0,) * a.ndim)
    return pl.pallas_call(
        functools.partial(_out_kernel, tm=tm),
        grid=(S // tm,),
        in_specs=[pl.BlockSpec((tm, D_MODEL), lambda i: (i, 0)),
                  pl.BlockSpec((tm, N_BRANCH * D_MODEL), lambda i: (i, 0)),
                  pl.BlockSpec((tm, LANES), lambda i: (i, OFF_D // LANES)),
                  full(b_gate),
                  pl.BlockSpec((MOBA_HEADS, tm, HEAD_DIM), lambda i: (0, i, 0)),
                  pl.BlockSpec((FOX_HEADS, tm, HEAD_DIM), lambda i: (0, i, 0)),
                  pl.BlockSpec((rows, HEAD_DIM), lambda i: (i, 0)),
                  pl.BlockSpec((rows, HEAD_DIM), lambda i: (i, 0)),
                  pl.BlockSpec((rows, HEAD_DIM), lambda i: (i, 0)),
                  pl.BlockSpec((N_ZHEADS, tm, HEAD_DIM), lambda i: (0, i, 0)),
                  full(w_m), full(w_n), full(w_f), full(w_o)],
        out_specs=pl.BlockSpec((tm, D_MODEL), lambda i: (i, 0)),
        out_shape=jax.ShapeDtypeStruct((S, D_MODEL), f32),
        compiler_params=_cparams(("arbitrary",)),
        name="out_proj",
    )(x, proj_gl, proj_main, b_gate, o_m, o_f, o_c, o_s, o_w, zs, w_m, w_n, w_f, w_o)


def _rope_tables(pos, rows):
    inv = ROPE_THETA ** (-jnp.arange(0, ROPE_DIM, 2, dtype=f32) / ROPE_DIM)
    ang = pos.astype(f32)[:, None] * inv[None, :]
    cos, sin = jnp.cos(ang), jnp.sin(ang)
    n = pos.shape[0]
    rest = HEAD_DIM - ROPE_DIM
    cos_h = jnp.concatenate([cos, cos, jnp.ones((n, rest), f32)], axis=1)
    sin_h = jnp.concatenate([-sin, sin, jnp.zeros((n, rest), f32)], axis=1)
    cos_t = jnp.concatenate([cos_h, cos_h], axis=1)
    sin_t = jnp.concatenate([sin_h, sin_h], axis=1)
    pad = ((0, rows - n), (0, 0))
    return jnp.pad(cos_t, pad), jnp.pad(sin_t, pad)


def _split_w_in(w_in):
    offs = np.concatenate([[0], np.cumsum(IN_SPLITS)])
    names = ("mq", "mk", "mv", "mz", "nq", "kc", "vc", "ksl", "vsl", "kw", "vw", "ng", "nz", "fq", "fk", "fv", "ff", "fz", "gl")
    seg = {n: w_in[:, offs[j]:offs[j + 1]] for j, n in enumerate(names)}
    pad_d = jnp.zeros((D_MODEL, SEG_D - 3 * NSA_HEADS - FOX_HEADS), w_in.dtype)
    main = jnp.concatenate([seg[n] for n in ("mq", "mk", "nq", "ksl", "kw", "fq", "fk",
                                             "mv", "fv", "vsl", "vw", "kc", "vc",
                                             "mz", "nz", "fz", "ng", "ff")] + [pad_d], axis=1)
    return main.astype(bf16), seg["gl"].astype(bf16)


def _layer(x, norm_g, w_in, b_f, b_gate, moba_qk_g, nsa_q_g, nsa_k_g, fox_qk_g,
           cmp_pe, cmp_w1, cmp_w2, w_up_moba, w_up_nsa, w_up_fox, w_out, tables, rope):
    S = x.shape[0]
    w_main, w_gl = _split_w_in(w_in)
    g_row = norm_g.reshape(1, D_MODEL)
    proj_main = rms_matmul(x, g_row, w_main)
    proj_gl = rms_matmul(x, g_row, w_gl)

    gain_row = jnp.concatenate([jnp.tile(moba_qk_g[0], MOBA_HEADS), jnp.tile(moba_qk_g[1], MOBA_HEADS),
                                jnp.tile(nsa_q_g, NSA_HEADS), nsa_k_g[1], nsa_k_g[2],
                                jnp.tile(fox_qk_g[0], FOX_HEADS), jnp.tile(fox_qk_g[1], FOX_HEADS)]).reshape(1, SEG_A)
    bf_row = jnp.zeros((1, LANES), f32).at[0, FF_LANE:FF_LANE + FOX_HEADS].set(b_f)
    (mq, mk, nq, ksl, kw, fq, fk, mv, fv, vsl, vw, zs) = prep(proj_main, rope[0], rope[1], gain_row, bf_row)

    q_moba = moba_select(mq, moba_kmean(mk))
    o_m = flash(q_moba, mk, mv, tables["causal"], tq=512, tk=512, pos_block=512, window=None, name="flash_moba")

    ncp = S // NSA_CMP_STRIDE
    kv_off = OFF_B + 2 * MOBA_W + 2 * HEAD_DIM
    tc = jnp.stack([proj_main[:, kv_off:kv_off + HEAD_DIM].reshape(ncp, NSA_CMP_STRIDE * HEAD_DIM),
                    proj_main[:, kv_off + HEAD_DIM:kv_off + 2 * HEAD_DIM].reshape(ncp, NSA_CMP_STRIDE * HEAD_DIM)])
    pe = cmp_pe.reshape(2, 2, NSA_CMP_STRIDE * HEAD_DIM)
    w2p = jnp.pad(cmp_w2, ((0, 0), (0, 0), (0, LANES - HEAD_DIM))).astype(bf16)
    gain_c = jnp.pad(nsa_k_g[0], (0, LANES - HEAD_DIM)).reshape(1, LANES)
    kc, vc = nsa_compress(tc, pe, cmp_w1.astype(bf16), w2p, gain_c, rope[2], rope[3])
    o_c, q_slc = nsa_cmp_select(nq, kc, vc, S)
    o_s = flash(q_slc, ksl[None], vsl[None], tables["slc"], tq=NSA_HEADS * Q_BLOCK, tk=512,
                pos_block=Q_BLOCK, window=None, name="flash_nsa_slc")[0]
    o_w = flash(q_slc, kw[None], vw[None], tables["win"], tq=NSA_HEADS * Q_BLOCK, tk=512,
                pos_block=Q_BLOCK, window=NSA_WINDOW, name="flash_nsa_win")[0]

    o_f = flash(fq[:, None], fk, fv, tables["causal"], tq=512, tk=512, pos_block=512, window=None, name="flash_fox")

    return out_proj(x, proj_gl, proj_main, b_gate.reshape(1, N_BRANCH * D_MODEL), o_m, o_f, o_c, o_s, o_w, zs,
                    w_up_moba.astype(bf16).reshape(MOBA_HEADS, HEAD_DIM, D_MODEL),
                    w_up_nsa.astype(bf16).reshape(NSA_HEADS, HEAD_DIM, D_MODEL),
                    w_up_fox.astype(bf16).reshape(FOX_HEADS, HEAD_DIM, D_MODEL),
                    w_out.astype(bf16))


def kernel(x, norm_g, w_in, b_f, b_gate, moba_qk_g, nsa_q_g, nsa_k_g, fox_qk_g, cmp_pe, cmp_w1, cmp_w2,
           w_up_moba, w_up_nsa, w_up_fox, w_out):
    B, S, _ = x.shape
    assert B == 1 and S % 2048 == 0 and S // MOBA_BLOCK <= CODE_BLOCKS
    depth = norm_g.shape[0]
    tk = 512
    win_keys = CODE_BLOCKS * NSA_SLC_BLOCK
    tables = {
        "causal": _pair_tables(S // 512, lambda a: (0, a), tk, S),
        "slc": _pair_tables(S // Q_BLOCK, lambda a: (0, (a * Q_BLOCK + Q_BLOCK - 1) // tk), tk, win_keys),
        "win": _pair_tables(S // Q_BLOCK, lambda a: (max(0, a * Q_BLOCK - (NSA_WINDOW - 1)) // tk,
                                                     (a * Q_BLOCK + Q_BLOCK - 1) // tk), tk, S * 2),
    }
    ncp = S // NSA_CMP_STRIDE
    cos_t, sin_t = _rope_tables(jnp.arange(S), S)
    cmp_end = jnp.arange(ncp - 1) * NSA_CMP_STRIDE + (NSA_CMP_LEN - 1)
    cos_c, sin_c = _rope_tables(cmp_end, ncp)
    rope = (cos_t, sin_t, cos_c, sin_c)
    h = x[0]
    for l in range(depth):
        h = _layer(h, norm_g[l], w_in[l], b_f[l], b_gate[l], moba_qk_g[l], nsa_q_g[l], nsa_k_g[l], fox_qk_g[l],
                   cmp_pe[l], cmp_w1[l], cmp_w2[l], w_up_moba[l], w_up_nsa[l], w_up_fox[l], w_out[l], tables, rope)
    return h[None]
```

```python
import functools

import numpy as np
import jax
import jax.numpy as jnp
from jax import lax
from jax.experimental import pallas as pl
from jax.experimental.pallas import tpu as pltpu

D_MODEL = 1024
HEAD_DIM = 64
ROPE_DIM = HEAD_DIM // 4
ROPE_THETA = 500000.0
RMS_EPS = 1e-6
NEG_INF = -1e30
POS_BIG = 1e30
M_FLOOR = -1e20

MOBA_HEADS = 6
MOBA_BLOCK = 256
MOBA_TOPK = 3
NSA_HEADS = 4
NSA_CMP_LEN = 32
NSA_CMP_STRIDE = 16
NSA_CMP_HIDDEN = 4 * HEAD_DIM
NSA_SLC_BLOCK = 64
NSA_SLC_TOPN = 16
NSA_WINDOW = 512
FOX_HEADS = 6
N_BRANCH = 3
MOBA_W = MOBA_HEADS * HEAD_DIM
NSA_W = NSA_HEADS * HEAD_DIM
FOX_W = FOX_HEADS * HEAD_DIM
IN_SPLITS = (MOBA_W,) * 4 + (NSA_W,) + (HEAD_DIM,) * 6 + (3 * NSA_HEADS, NSA_W) + (FOX_W,) * 3 + (FOX_HEADS, FOX_W, N_BRANCH * D_MODEL)
ATTN_SCALE = HEAD_DIM ** -0.5

LANES = 128
AUG = 2 * HEAD_DIM
CODE_BLOCKS = AUG - HEAD_DIM
Q_BLOCK = 128
N_ZHEADS = MOBA_HEADS + NSA_HEADS + FOX_HEADS

SEG_A = 2 * MOBA_W + NSA_W + 2 * HEAD_DIM + 2 * FOX_W
SEG_B = 2 * MOBA_W + 4 * HEAD_DIM
SEG_C = MOBA_W + NSA_W + FOX_W
SEG_D = LANES
OFF_B = SEG_A
OFF_C = OFF_B + SEG_B
OFF_D = OFF_C + SEG_C
MAIN_COLS = OFF_D + SEG_D
FF_LANE = 3 * NSA_HEADS

VMEM_LIMIT = 56 * 1024 * 1024

f32 = jnp.float32
bf16 = jnp.bfloat16


def _cparams(sem):
    return pltpu.CompilerParams(dimension_semantics=sem, vmem_limit_bytes=VMEM_LIMIT)


def _iota2(shape, dim):
    return lax.broadcasted_iota(jnp.int32, shape, dim)


def _div(x, d):
    return jnp.right_shift(x, int(d).bit_length() - 1)


def _mod(x, d):
    return jnp.bitwise_and(x, d - 1)


def _place(n_in, n_out, in_off, out_off, width=HEAD_DIM, val=1.0):
    r = _iota2((n_in, n_out), 0) - in_off
    c = _iota2((n_in, n_out), 1) - out_off
    hit = (r == c) & (r >= 0) & (r < width)
    return jnp.where(hit, val, 0.0).astype(bf16)


def _place_t(n_out, n_in, in_off, out_off, width=HEAD_DIM, val=1.0):
    r = _iota2((n_out, n_in), 0) - out_off
    c = _iota2((n_out, n_in), 1) - in_off
    hit = (r == c) & (r >= 0) & (r < width)
    return jnp.where(hit, val, 0.0).astype(bf16)


def _split3(x):
    hi = x.astype(bf16)
    r = x - hi.astype(f32)
    mid = r.astype(bf16)
    lo = (r - mid.astype(f32)).astype(bf16)
    return hi, mid, lo


def _dot(a, b):
    return jnp.dot(a, b, preferred_element_type=f32)


def _dot_nt(a, b):
    return lax.dot_general(a, b, (((1,), (1,)), ((), ())), preferred_element_type=f32)


def _rms_matmul_kernel(x_ref, g_ref, w_ref, o_ref, h_sc):
    @pl.when(pl.program_id(1) == 0)
    def _():
        x = x_ref[...]
        ms = jnp.mean(x * x, axis=-1, keepdims=True)
        h_sc[...] = (x * lax.rsqrt(ms + RMS_EPS) * g_ref[...]).astype(bf16)

    o_ref[...] = _dot(h_sc[...], w_ref[...])


def rms_matmul(x, g, w, tm=512, tn=512):
    S, D = x.shape
    N = w.shape[1]
    return pl.pallas_call(
        _rms_matmul_kernel,
        grid=(S // tm, N // tn),
        in_specs=[pl.BlockSpec((tm, D), lambda i, j: (i, 0)),
                  pl.BlockSpec((1, D), lambda i, j: (0, 0)),
                  pl.BlockSpec((D, tn), lambda i, j: (0, j))],
        out_specs=pl.BlockSpec((tm, tn), lambda i, j: (i, j)),
        out_shape=jax.ShapeDtypeStruct((S, N), f32),
        scratch_shapes=[pltpu.VMEM((tm, D), bf16)],
        compiler_params=_cparams(("arbitrary", "arbitrary")),
        name="rms_matmul",
    )(x, g, w)


def _prep_kernel(p_ref, cos_ref, sin_ref, gain_ref, bf_ref,
                 mq_ref, mk_ref, nq_ref, ksl_ref, kw_ref, fq_ref, fk_ref,
                 mv_ref, fv_ref, vsl_ref, vw_ref, zs_ref, carry_sc, *, ts):
    i = pl.program_id(0)

    @pl.when(i == 0)
    def _():
        carry_sc[...] = jnp.zeros_like(carry_sc)

    lane = _iota2((ts, LANES), 1)
    pos = _iota2((ts, LANES), 0) + i * ts
    blockdiag = jnp.where(_div(_iota2((LANES, LANES), 0), HEAD_DIM) == _div(_iota2((LANES, LANES), 1), HEAD_DIM),
                          1.0, 0.0).astype(bf16)
    to64 = [_place(LANES, HEAD_DIM, 0, 0), _place(LANES, HEAD_DIM, HEAD_DIM, 0)]
    to128 = [_place(LANES, AUG, 0, 0), _place(LANES, AUG, HEAD_DIM, 0)]
    rows_scaled = [_place_t(AUG, LANES, 0, 0, val=ATTN_SCALE), _place_t(AUG, LANES, HEAD_DIM, 0, val=ATTN_SCALE)]
    rows = [_place_t(AUG, LANES, 0, 0), _place_t(AUG, LANES, HEAD_DIM, 0)]
    rowi = _iota2((AUG, ts), 0)
    ones_row = jnp.where(rowi == HEAD_DIM, 1.0, 0.0)
    first_half = _mod(lane, HEAD_DIM) < (ROPE_DIM // 2)
    cos = cos_ref[...]
    sin = sin_ref[...]

    def normed(c, rope):
        x = p_ref[:, c * LANES:(c + 1) * LANES]
        x2 = x * x
        hi = x2.astype(bf16)
        lo = (x2 - hi.astype(f32)).astype(bf16)
        ss = _dot(hi, blockdiag) + _dot(lo, blockdiag)
        y = x * lax.rsqrt(ss * (1.0 / HEAD_DIM) + RMS_EPS) * gain_ref[:, c * LANES:(c + 1) * LANES]
        if rope:
            up = pltpu.roll(y, LANES - ROPE_DIM // 2, 1)
            dn = pltpu.roll(y, ROPE_DIM // 2, 1)
            y = y * cos + jnp.where(first_half, up, dn) * sin
        return y.astype(bf16)

    d = p_ref[:, OFF_D:OFF_D + LANES] + bf_ref[...]
    logf = jnp.minimum(d, 0.0) - jnp.log(1.0 + jnp.exp(-jnp.abs(d)))
    tri = jnp.where(_iota2((ts, ts), 1) <= _iota2((ts, ts), 0), 1.0, 0.0).astype(bf16)
    lh, lm, ll = _split3(logf)
    c = carry_sc[0:1, :] + (_dot(tri, lh) + _dot(tri, lm) + _dot(tri, ll))
    carry_sc[...] = jnp.broadcast_to(c[ts - 1:ts, :], carry_sc.shape)
    ch, cm, cl = _split3(c)
    one_q = jnp.where((rowi >= HEAD_DIM + 3) & (rowi < HEAD_DIM + 6), 1.0, 0.0)
    one_k = jnp.where((lane >= HEAD_DIM) & (lane < HEAD_DIM + 3), 1.0, 0.0)

    def decay_cols(h, base):
        src = FF_LANE + h
        return (_dot(ch, _place(LANES, AUG, src, base, width=1))
                + _dot(cm, _place(LANES, AUG, src, base + 1, width=1))
                + _dot(cl, _place(LANES, AUG, src, base + 2, width=1)))

    def decay_rows(h, base):
        src = FF_LANE + h
        return (_dot_nt(_place_t(AUG, LANES, src, base, width=1), ch)
                + _dot_nt(_place_t(AUG, LANES, src, base + 1, width=1), cm)
                + _dot_nt(_place_t(AUG, LANES, src, base + 2, width=1), cl))

    moba_code = jnp.where((lane >= HEAD_DIM) & ((lane - HEAD_DIM) == _div(pos, MOBA_BLOCK)), 1.0, 0.0)
    for c_i in range(3):
        yq = normed(c_i, True)
        yk = normed(3 + c_i, True)
        for half in range(2):
            h = 2 * c_i + half
            mq_ref[h] = _dot(yq, to64[half]).astype(bf16)
            mk_ref[h] = (_dot(yk, to128[half]) + moba_code).astype(bf16)
    for c_i in range(2):
        y = normed(6 + c_i, True)
        for half in range(2):
            h = 2 * c_i + half
            yh = _dot(y, to64[half]).astype(bf16)
            for qb in range(ts // Q_BLOCK):
                nq_ref[(qb * NSA_HEADS + h) * Q_BLOCK:(qb * NSA_HEADS + h + 1) * Q_BLOCK, :] = (
                    yh[qb * Q_BLOCK:(qb + 1) * Q_BLOCK, :])
    y = normed(8, True)
    slc_code = jnp.where((lane >= HEAD_DIM) & ((lane - HEAD_DIM) == _mod(_div(pos, NSA_SLC_BLOCK), CODE_BLOCKS)), 1.0, 0.0)
    ksl_ref[...] = (_dot(y, to128[0]) + slc_code).astype(bf16)
    kw_ref[...] = _dot(y, to128[1]).astype(bf16)
    for c_i in range(3):
        yq = normed(9 + c_i, False)
        yk = normed(12 + c_i, False)
        for half in range(2):
            h = 2 * c_i + half
            fq_ref[h] = (_dot_nt(rows_scaled[half], yq) + decay_rows(h, HEAD_DIM) + one_q).astype(bf16)
            fk_ref[h] = (_dot(yk, to128[half]) - decay_cols(h, HEAD_DIM + 3) + one_k).astype(bf16)
    for c_i in range(3):
        xm = p_ref[:, OFF_B + c_i * LANES:OFF_B + (c_i + 1) * LANES].astype(bf16)
        xf = p_ref[:, OFF_B + MOBA_W + c_i * LANES:OFF_B + MOBA_W + (c_i + 1) * LANES].astype(bf16)
        for half in range(2):
            mv_ref[2 * c_i + half] = (_dot_nt(rows[half], xm) + ones_row).astype(bf16)
            fv_ref[2 * c_i + half] = (_dot_nt(rows[half], xf) + ones_row).astype(bf16)
    xs = p_ref[:, OFF_B + 2 * MOBA_W:OFF_B + 2 * MOBA_W + LANES].astype(bf16)
    vsl_ref[...] = (_dot_nt(rows[0], xs) + ones_row).astype(bf16)
    vw_ref[...] = (_dot_nt(rows[1], xs) + ones_row).astype(bf16)
    for c_i in range(SEG_C // LANES):
        z = p_ref[:, OFF_C + c_i * LANES:OFF_C + (c_i + 1) * LANES]
        zs = (z * (1.0 / (1.0 + jnp.exp(-z)))).astype(bf16)
        for half in range(2):
            zs_ref[2 * c_i + half] = _dot(zs, to64[half]).astype(bf16)


def prep(proj_main, cos_t, sin_t, gain_row, bf_row, ts=256):
    S = proj_main.shape[0]
    head64 = lambda n: jax.ShapeDtypeStruct((n, S, HEAD_DIM), bf16)
    head128 = lambda n: jax.ShapeDtypeStruct((n, S, AUG), bf16)
    spec_h = lambda n, w: pl.BlockSpec((n, ts, w), lambda i: (0, i, 0))
    spec_r = lambda w: pl.BlockSpec((ts, w), lambda i: (i, 0))
    head_t = lambda n: jax.ShapeDtypeStruct((n, AUG, S), bf16)
    spec_ht = lambda n: pl.BlockSpec((n, AUG, ts), lambda i: (0, 0, i))
    spec_t = pl.BlockSpec((AUG, ts), lambda i: (0, i))
    out_shape = (head64(MOBA_HEADS), head128(MOBA_HEADS),
                 jax.ShapeDtypeStruct((NSA_HEADS * S, HEAD_DIM), bf16),
                 jax.ShapeDtypeStruct((S, AUG), bf16), jax.ShapeDtypeStruct((S, AUG), bf16),
                 head_t(FOX_HEADS), head128(FOX_HEADS),
                 head_t(MOBA_HEADS), head_t(FOX_HEADS),
                 jax.ShapeDtypeStruct((AUG, S), bf16), jax.ShapeDtypeStruct((AUG, S), bf16),
                 head64(N_ZHEADS))
    out_specs = (spec_h(MOBA_HEADS, HEAD_DIM), spec_h(MOBA_HEADS, AUG),
                 pl.BlockSpec((NSA_HEADS * ts, HEAD_DIM), lambda i: (i, 0)),
                 spec_r(AUG), spec_r(AUG),
                 spec_ht(FOX_HEADS), spec_h(FOX_HEADS, AUG),
                 spec_ht(MOBA_HEADS), spec_ht(FOX_HEADS),
                 spec_t, spec_t,
                 spec_h(N_ZHEADS, HEAD_DIM))
    return pl.pallas_call(
        functools.partial(_prep_kernel, ts=ts),
        grid=(S // ts,),
        in_specs=[pl.BlockSpec((ts, MAIN_COLS), lambda i: (i, 0)),
                  spec_r(LANES), spec_r(LANES),
                  pl.BlockSpec((1, SEG_A), lambda i: (0, 0)),
                  pl.BlockSpec((1, LANES), lambda i: (0, 0))],
        out_specs=out_specs,
        out_shape=out_shape,
        scratch_shapes=[pltpu.VMEM((8, LANES), f32)],
        compiler_params=_cparams(("arbitrary",)),
        name="prep",
    )(proj_main, cos_t, sin_t, gain_row, bf_row)


def _kmean_kernel(k_ref, o_ref, *, rows):
    n = rows // MOBA_BLOCK
    avg = jnp.where(_div(_iota2((n, rows), 1), MOBA_BLOCK) == _iota2((n, rows), 0),
                    1.0 / MOBA_BLOCK, 0.0).astype(bf16)
    o_ref[0] = _dot(avg, k_ref[0])[:, :HEAD_DIM]


def moba_kmean(mk_aug):
    H, S, _ = mk_aug.shape
    rows = 8 * MOBA_BLOCK
    return pl.pallas_call(
        functools.partial(_kmean_kernel, rows=rows),
        grid=(H, S // rows),
        in_specs=[pl.BlockSpec((1, rows, AUG), lambda h, i: (h, i, 0))],
        out_specs=pl.BlockSpec((1, 8, HEAD_DIM), lambda h, i: (h, i, 0)),
        out_shape=jax.ShapeDtypeStruct((H, CODE_BLOCKS, HEAD_DIM), f32),
        compiler_params=_cparams(("arbitrary", "arbitrary")),
        name="moba_kmean",
    )(mk_aug)


def _top_select(score, idx, n_pick, floor):
    big = jnp.int32(2 ** 30)
    sel = jnp.zeros(score.shape, f32)
    for _ in range(n_pick):
        m = jnp.max(score, axis=1, keepdims=True)
        first = jnp.min(jnp.where(score == m, idx, big), axis=1, keepdims=True)
        pick = (idx == first) & (m > floor)
        sel = jnp.where(pick, 1.0, sel)
        score = jnp.where(pick, -jnp.inf, score)
    return sel


def _moba_select_kernel(q_ref, km_ref, o_ref, *, tq):
    i = pl.program_id(1)
    q = q_ref[0]
    km = km_ref[0]
    km_hi = km.astype(bf16)
    km_lo = (km - km_hi.astype(f32)).astype(bf16)
    gate = _dot_nt(q, km_hi) + _dot_nt(q, km_lo)
    blk = _iota2((tq, CODE_BLOCKS), 1)
    cur = _div(_iota2((tq, CODE_BLOCKS), 0) + i * tq, MOBA_BLOCK)
    score = jnp.where(blk < cur, gate, NEG_INF)
    sel = _top_select(score, blk, MOBA_TOPK, NEG_INF)
    bias = jnp.where((sel > 0.0) | (blk == cur), 0.0, NEG_INF).astype(bf16)
    o_ref[0, 0] = (_dot_nt(_place_t(AUG, HEAD_DIM, 0, 0, val=ATTN_SCALE), q)
                   + _dot_nt(_place_t(AUG, CODE_BLOCKS, 0, HEAD_DIM), bias)).astype(bf16)


def moba_select(mq, kmean, tq=512):
    H, S, _ = mq.shape
    return pl.pallas_call(
        functools.partial(_moba_select_kernel, tq=tq),
        grid=(H, S // tq),
        in_specs=[pl.BlockSpec((1, tq, HEAD_DIM), lambda h, i: (h, i, 0)),
                  pl.BlockSpec((1, CODE_BLOCKS, HEAD_DIM), lambda h, i: (h, 0, 0))],
        out_specs=pl.BlockSpec((1, 1, AUG, tq), lambda h, i: (h, 0, 0, i)),
        out_shape=jax.ShapeDtypeStruct((H, 1, AUG, S), bf16),
        compiler_params=_cparams(("arbitrary", "arbitrary")),
        name="moba_select",
    )(mq, kmean)


def _cmp_kernel(tc_ref, pe_ref, w1_ref, w2_ref, gain_ref, cos_ref, sin_ref, kc_ref, vc_ref, *, ncp):
    half_w = NSA_CMP_STRIDE * HEAD_DIM

    def mlp(b):
        t = tc_ref[b]
        top = (t + pe_ref[b, 0:1, :]).astype(bf16)
        bot = (t + pe_ref[b, 1:2, :]).astype(bf16)
        a = _dot(top, w1_ref[b, 0:half_w, :])
        nxt = _dot(bot, w1_ref[b, half_w:2 * half_w, :])
        hid = a + pltpu.roll(nxt, ncp - 1, 0)
        act = hid * (1.0 / (1.0 + jnp.exp(-hid)))
        return _dot(act.astype(bf16), w2_ref[b])

    k = mlp(0)
    ms = jnp.sum(k * k, axis=-1, keepdims=True) * (1.0 / HEAD_DIM)
    y = k * lax.rsqrt(ms + RMS_EPS) * gain_ref[...]
    lane = _iota2((ncp, LANES), 1)
    up = pltpu.roll(y, LANES - ROPE_DIM // 2, 1)
    dn = pltpu.roll(y, ROPE_DIM // 2, 1)
    y = y * cos_ref[...] + jnp.where(lane < ROPE_DIM // 2, up, dn) * sin_ref[...]
    kc_ref[...] = y[:, :HEAD_DIM].astype(bf16)
    vc_ref[...] = mlp(1)[:, :HEAD_DIM].astype(bf16)


def nsa_compress(tc, pe, w1, w2, gain, cos_c, sin_c):
    ncp = tc.shape[1]
    return pl.pallas_call(
        functools.partial(_cmp_kernel, ncp=ncp),
        out_shape=(jax.ShapeDtypeStruct((ncp, HEAD_DIM), bf16), jax.ShapeDtypeStruct((ncp, HEAD_DIM), bf16)),
        compiler_params=pltpu.CompilerParams(vmem_limit_bytes=VMEM_LIMIT),
        name="nsa_compress",
    )(tc, pe, w1, w2, gain, cos_c, sin_c)


def _nsa_cmp_select_kernel(q_ref, kc_ref, vc_ref, oc_ref, qa_ref, *, ncp, nsp, n_win):
    i = pl.program_id(0)
    rows = NSA_HEADS * Q_BLOCK
    q = q_ref[...]
    s = _dot_nt(q, kc_ref[...]) * ATTN_SCALE
    qpos = i * Q_BLOCK + _mod(_iota2((rows, ncp), 0), Q_BLOCK)
    kend = _iota2((rows, ncp), 1) * NSA_CMP_STRIDE + (NSA_CMP_LEN - 1)
    ok = kend <= qpos
    sm = jnp.where(ok, s, NEG_INF)
    e = jnp.exp(sm - jnp.max(sm, axis=1, keepdims=True))
    p = jnp.where(ok, e * (1.0 / jnp.sum(e, axis=1, keepdims=True)), 0.0)
    oc_ref[...] = _dot(p.astype(bf16), vc_ref[...]).astype(bf16)
    psum = p[0:Q_BLOCK] + p[Q_BLOCK:2 * Q_BLOCK] + p[2 * Q_BLOCK:3 * Q_BLOCK] + p[3 * Q_BLOCK:4 * Q_BLOCK]
    c0 = _iota2((ncp, nsp), 0) * NSA_CMP_STRIDE
    b0 = _iota2((ncp, nsp), 1) * NSA_SLC_BLOCK
    overlap = jnp.where((c0 <= b0 + (NSA_SLC_BLOCK - 1)) & (c0 + (NSA_CMP_LEN - 1) >= b0), 1.0, 0.0).astype(bf16)
    ph, pm, plo = _split3(psum)
    imp = _dot(ph, overlap) + _dot(pm, overlap) + _dot(plo, overlap)
    blk = _iota2((Q_BLOCK, nsp), 1)
    cur = _div(_iota2((Q_BLOCK, nsp), 0) + i * Q_BLOCK, NSA_SLC_BLOCK)
    score = jnp.where(blk <= cur, imp, NEG_INF)
    forced = (blk == 0) | (blk == cur) | (blk == cur - 1)
    score = jnp.where(forced, POS_BIG, score)
    sel = _top_select(score, blk, NSA_SLC_TOPN, NEG_INF)
    bias = jnp.where(sel > 0.0, 0.0, NEG_INF).astype(bf16)
    qs = _dot_nt(_place_t(AUG, HEAD_DIM, 0, 0, val=ATTN_SCALE), q)
    for w in range(n_win):
        bw = _dot_nt(_place_t(AUG, nsp, w * CODE_BLOCKS, HEAD_DIM, width=CODE_BLOCKS), bias)
        qa_ref[0, w] = (qs + jnp.concatenate([bw] * NSA_HEADS, axis=1)).astype(bf16)


def nsa_cmp_select(nq, kc, vc, S):
    ncp = kc.shape[0]
    ns = S // NSA_SLC_BLOCK
    nsp = max(LANES, ns)
    n_win = max(1, ns // CODE_BLOCKS)
    rows = NSA_HEADS * Q_BLOCK
    return pl.pallas_call(
        functools.partial(_nsa_cmp_select_kernel, ncp=ncp, nsp=nsp, n_win=n_win),
        grid=(S // Q_BLOCK,),
        in_specs=[pl.BlockSpec((rows, HEAD_DIM), lambda i: (i, 0)),
                  pl.BlockSpec((ncp, HEAD_DIM), lambda i: (0, 0)),
                  pl.BlockSpec((ncp, HEAD_DIM), lambda i: (0, 0))],
        out_specs=(pl.BlockSpec((rows, HEAD_DIM), lambda i: (i, 0)),
                   pl.BlockSpec((1, n_win, AUG, rows), lambda i: (0, 0, 0, i))),
        out_shape=(jax.ShapeDtypeStruct((NSA_HEADS * S, HEAD_DIM), bf16),
                   jax.ShapeDtypeStruct((1, n_win, AUG, NSA_HEADS * S), bf16)),
        compiler_params=_cparams(("arbitrary",)),
        name="nsa_cmp_select",
    )(nq, kc, vc)


def _flash_kernel(qi_t, ki_t, win_t, first_t, last_t, mask_t, q_ref, k_ref, v_ref, o_ref, m_sc, acc_sc,
                  *, n_heads, tq, tk, pos_block, window):
    p_id = pl.program_id(0)
    qi = qi_t[p_id]
    ki = ki_t[p_id]

    @pl.when(first_t[p_id] == 1)
    def _():
        m_sc[...] = jnp.full(m_sc.shape, M_FLOOR, f32)
        acc_sc[...] = jnp.zeros_like(acc_sc)

    def tile(masked):
        if masked:
            krow = _iota2((tk, tq), 0)
            qcol = _iota2((tk, tq), 1)
            delta = (qi * pos_block - ki * tk) + _mod(qcol, pos_block) - krow
            ok = (delta >= 0) if window is None else (delta.astype(jnp.uint32) < jnp.uint32(window))
        for g in range(n_heads):
            s = _dot(k_ref[g], q_ref[g, 0])
            if masked:
                s = jnp.where(ok, s, NEG_INF)
            m_prev = m_sc[g, 0:1, :]
            m_new = jnp.maximum(m_prev, jnp.max(s, axis=0, keepdims=True))
            alpha = jnp.exp(m_prev - m_new)
            p = jnp.exp(s - m_new).astype(bf16)
            acc_sc[g] = alpha * acc_sc[g] + _dot(v_ref[g], p)
            m_sc[g] = jnp.broadcast_to(m_new, m_sc.shape[1:])

    @pl.when(mask_t[p_id] == 1)
    def _():
        tile(True)

    @pl.when(mask_t[p_id] == 0)
    def _():
        tile(False)

    @pl.when(last_t[p_id] == 1)
    def _():
        for g in range(n_heads):
            acc = acc_sc[g]
            l = acc[HEAD_DIM:HEAD_DIM + 1, :]
            inv = jnp.where(l > 0.0, 1.0 / l, 0.0)
            o_ref[g] = (acc * inv).T[:, :HEAD_DIM].astype(o_ref.dtype)


def _pair_tables(n_q, k_range, tk, win_keys, needs_mask):
    qi, ki, wi, first, last, mask = [], [], [], [], [], []
    for a in range(n_q):
        lo, hi = k_range(a)
        for b in range(lo, hi + 1):
            qi.append(a)
            ki.append(b)
            wi.append((b * tk) // win_keys)
            first.append(int(b == lo))
            last.append(int(b == hi))
            mask.append(int(needs_mask(a, b)))
    return tuple(jnp.asarray(np.asarray(t, np.int32)) for t in (qi, ki, wi, first, last, mask))


def flash(q, k, v, tables, *, tq, tk, pos_block, window, name):
    G, _, _, Sq = q.shape
    n_pairs = tables[0].shape[0]
    grid_spec = pltpu.PrefetchScalarGridSpec(
        num_scalar_prefetch=6,
        grid=(n_pairs,),
        in_specs=[pl.BlockSpec((G, 1, AUG, tq), lambda p, qi, ki, wi, fi, la, ma: (0, wi[p], 0, qi[p])),
                  pl.BlockSpec((G, tk, AUG), lambda p, qi, ki, wi, fi, la, ma: (0, ki[p], 0)),
                  pl.BlockSpec((G, AUG, tk), lambda p, qi, ki, wi, fi, la, ma: (0, 0, ki[p]))],
        out_specs=pl.BlockSpec((G, tq, HEAD_DIM), lambda p, qi, ki, wi, fi, la, ma: (0, qi[p], 0)),
        scratch_shapes=[pltpu.VMEM((G, 8, tq), f32), pltpu.VMEM((G, AUG, tq), f32)],
    )
    return pl.pallas_call(
        functools.partial(_flash_kernel, n_heads=G, tq=tq, tk=tk, pos_block=pos_block, window=window),
        grid_spec=grid_spec,
        out_shape=jax.ShapeDtypeStruct((G, Sq, HEAD_DIM), bf16),
        compiler_params=_cparams(("arbitrary",)),
        name=name,
    )(*tables, q, k, v)


def _out_kernel(x_ref, gl_ref, sd_ref, bg_ref, om_ref, of_ref, oc_ref, os_ref, ow_ref, zs_ref,
                wm_ref, wn_ref, wf_ref, wo_ref, o_ref, *, tm):
    def up(o_of_head, n_heads, z_off, w_ref):
        y = jnp.zeros((tm, D_MODEL), f32)
        for h in range(n_heads):
            g = (o_of_head(h) * zs_ref[z_off + h].astype(f32)).astype(bf16)
            y = y + _dot(g, w_ref[h])
        return y

    sg = 1.0 / (1.0 + jnp.exp(-sd_ref[...]))

    def nsa_head(h):
        rows = slice(h * Q_BLOCK, (h + 1) * Q_BLOCK)
        return (sg[:, 3 * h:3 * h + 1] * oc_ref[rows, :].astype(f32)
                + sg[:, 3 * h + 1:3 * h + 2] * os_ref[rows, :].astype(f32)
                + sg[:, 3 * h + 2:3 * h + 3] * ow_ref[rows, :].astype(f32))

    y_m = up(lambda h: om_ref[h].astype(f32), MOBA_HEADS, 0, wm_ref)
    y_n = up(nsa_head, NSA_HEADS, MOBA_HEADS, wn_ref)
    y_f = up(lambda h: of_ref[h].astype(f32), FOX_HEADS, MOBA_HEADS + NSA_HEADS, wf_ref)
    merged = jnp.zeros((tm, D_MODEL), f32)
    for b, y in enumerate((y_m, y_n, y_f)):
        cols = slice(b * D_MODEL, (b + 1) * D_MODEL)
        merged = merged + y * (1.0 / (1.0 + jnp.exp(-(gl_ref[:, cols] + bg_ref[:, cols]))))
    o_ref[...] = x_ref[...] + _dot(merged.astype(bf16), wo_ref[...])


def out_proj(x, proj_gl, proj_main, b_gate, o_m, o_f, o_c, o_s, o_w, zs, w_m, w_n, w_f, w_o):
    S = x.shape[0]
    tm = Q_BLOCK
    rows = NSA_HEADS * Q_BLOCK
    full = lambda a: pl.BlockSpec(a.shape, lambda i: (0,) * a.ndim)
    return pl.pallas_call(
        functools.partial(_out_kernel, tm=tm),
        grid=(S // tm,),
        in_specs=[pl.BlockSpec((tm, D_MODEL), lambda i: (i, 0)),
                  pl.BlockSpec((tm, N_BRANCH * D_MODEL), lambda i: (i, 0)),
                  pl.BlockSpec((tm, LANES), lambda i: (i, OFF_D // LANES)),
                  full(b_gate),
                  pl.BlockSpec((MOBA_HEADS, tm, HEAD_DIM), lambda i: (0, i, 0)),
                  pl.BlockSpec((FOX_HEADS, tm, HEAD_DIM), lambda i: (0, i, 0)),
                  pl.BlockSpec((rows, HEAD_DIM), lambda i: (i, 0)),
                  pl.BlockSpec((rows, HEAD_DIM), lambda i: (i, 0)),
                  pl.BlockSpec((rows, HEAD_DIM), lambda i: (i, 0)),
                  pl.BlockSpec((N_ZHEADS, tm, HEAD_DIM), lambda i: (0, i, 0)),
                  full(w_m), full(w_n), full(w_f), full(w_o)],
        out_specs=pl.BlockSpec((tm, D_MODEL), lambda i: (i, 0)),
        out_shape=jax.ShapeDtypeStruct((S, D_MODEL), f32),
        compiler_params=_cparams(("arbitrary",)),
        name="out_proj",
    )(x, proj_gl, proj_main, b_gate, o_m, o_f, o_c, o_s, o_w, zs, w_m, w_n, w_f, w_o)


def _rope_tables(pos, rows):
    inv = ROPE_THETA ** (-jnp.arange(0, ROPE_DIM, 2, dtype=f32) / ROPE_DIM)
    ang = pos.astype(f32)[:, None] * inv[None, :]
    cos, sin = jnp.cos(ang), jnp.sin(ang)
    n = pos.shape[0]
    rest = HEAD_DIM - ROPE_DIM
    cos_h = jnp.concatenate([cos, cos, jnp.ones((n, rest), f32)], axis=1)
    sin_h = jnp.concatenate([-sin, sin, jnp.zeros((n, rest), f32)], axis=1)
    cos_t = jnp.concatenate([cos_h, cos_h], axis=1)
    sin_t = jnp.concatenate([sin_h, sin_h], axis=1)
    pad = ((0, rows - n), (0, 0))
    return jnp.pad(cos_t, pad), jnp.pad(sin_t, pad)


def _split_w_in(w_in):
    offs = np.concatenate([[0], np.cumsum(IN_SPLITS)])
    names = ("mq", "mk", "mv", "mz", "nq", "kc", "vc", "ksl", "vsl", "kw", "vw", "ng", "nz", "fq", "fk", "fv", "ff", "fz", "gl")
    seg = {n: w_in[:, offs[j]:offs[j + 1]] for j, n in enumerate(names)}
    pad_d = jnp.zeros((D_MODEL, SEG_D - 3 * NSA_HEADS - FOX_HEADS), w_in.dtype)
    main = jnp.concatenate([seg[n] for n in ("mq", "mk", "nq", "ksl", "kw", "fq", "fk",
                                             "mv", "fv", "vsl", "vw", "kc", "vc",
                                             "mz", "nz", "fz", "ng", "ff")] + [pad_d], axis=1)
    return main.astype(bf16), seg["gl"].astype(bf16)


def _layer(x, norm_g, w_in, b_f, b_gate, moba_qk_g, nsa_q_g, nsa_k_g, fox_qk_g,
           cmp_pe, cmp_w1, cmp_w2, w_up_moba, w_up_nsa, w_up_fox, w_out, tables, rope):
    S = x.shape[0]
    w_main, w_gl = _split_w_in(w_in)
    g_row = norm_g.reshape(1, D_MODEL)
    proj_main = rms_matmul(x, g_row, w_main)
    proj_gl = rms_matmul(x, g_row, w_gl)

    gain_row = jnp.concatenate([jnp.tile(moba_qk_g[0], MOBA_HEADS), jnp.tile(moba_qk_g[1], MOBA_HEADS),
                                jnp.tile(nsa_q_g, NSA_HEADS), nsa_k_g[1], nsa_k_g[2],
                                jnp.tile(fox_qk_g[0], FOX_HEADS), jnp.tile(fox_qk_g[1], FOX_HEADS)]).reshape(1, SEG_A)
    bf_row = jnp.zeros((1, LANES), f32).at[0, FF_LANE:FF_LANE + FOX_HEADS].set(b_f)
    (mq, mk, nq, ksl, kw, fq, fk, mv, fv, vsl, vw, zs) = prep(proj_main, rope[0], rope[1], gain_row, bf_row)

    q_moba = moba_select(mq, moba_kmean(mk))
    o_m = flash(q_moba, mk, mv, tables["causal"], tq=512, tk=512, pos_block=512, window=None, name="flash_moba")

    ncp = S // NSA_CMP_STRIDE
    kv_off = OFF_B + 2 * MOBA_W + 2 * HEAD_DIM
    tc = jnp.stack([proj_main[:, kv_off:kv_off + HEAD_DIM].reshape(ncp, NSA_CMP_STRIDE * HEAD_DIM),
                    proj_main[:, kv_off + HEAD_DIM:kv_off + 2 * HEAD_DIM].reshape(ncp, NSA_CMP_STRIDE * HEAD_DIM)])
    pe = cmp_pe.reshape(2, 2, NSA_CMP_STRIDE * HEAD_DIM)
    w2p = jnp.pad(cmp_w2, ((0, 0), (0, 0), (0, LANES - HEAD_DIM))).astype(bf16)
    gain_c = jnp.pad(nsa_k_g[0], (0, LANES - HEAD_DIM)).reshape(1, LANES)
    kc, vc = nsa_compress(tc, pe, cmp_w1.astype(bf16), w2p, gain_c, rope[2], rope[3])
    o_c, q_slc = nsa_cmp_select(nq, kc, vc, S)
    o_s = flash(q_slc, ksl[None], vsl[None], tables["slc"], tq=NSA_HEADS * Q_BLOCK, tk=512,
                pos_block=Q_BLOCK, window=None, name="flash_nsa_slc")[0]
    o_w = flash(q_slc, kw[None], vw[None], tables["win"], tq=NSA_HEADS * Q_BLOCK, tk=512,
                pos_block=Q_BLOCK, window=NSA_WINDOW, name="flash_nsa_win")[0]

    o_f = flash(fq[:, None], fk, fv, tables["causal"], tq=512, tk=512, pos_block=512, window=None, name="flash_fox")
    return out_proj(x, proj_gl, proj_main, b_gate.reshape(1, N_BRANCH * D_MODEL), o_m, o_f, o_c, o_s, o_w, zs,
                    w_up_moba.astype(bf16).reshape(MOBA_HEADS, HEAD_DIM, D_MODEL),
                    w_up_nsa.astype(bf16).reshape(NSA_HEADS, HEAD_DIM, D_MODEL),
                    w_up_fox.astype(bf16).reshape(FOX_HEADS, HEAD_DIM, D_MODEL),
                    w_out.astype(bf16))


def kernel(x, norm_g, w_in, b_f, b_gate, moba_qk_g, nsa_q_g, nsa_k_g, fox_qk_g, cmp_pe, cmp_w1, cmp_w2,
           w_up_moba, w_up_nsa, w_up_fox, w_out):
    B, S, _ = x.shape
    assert B == 1 and S % 2048 == 0 and S // MOBA_BLOCK <= CODE_BLOCKS
    depth = norm_g.shape[0]
    tk = 512
    win_keys = CODE_BLOCKS * NSA_SLC_BLOCK
    tables = {
        "causal": _pair_tables(S // 512, lambda a: (0, a), tk, S, lambda a, b: a == b),
        "slc": _pair_tables(S // Q_BLOCK, lambda a: (0, (a * Q_BLOCK + Q_BLOCK - 1) // tk), tk, win_keys,
                            lambda a, b: b * tk + tk - 1 > a * Q_BLOCK),
        "win": _pair_tables(S // Q_BLOCK, lambda a: (max(0, a * Q_BLOCK - (NSA_WINDOW - 1)) // tk,
                                                     (a * Q_BLOCK + Q_BLOCK - 1) // tk), tk, S * 2,
                            lambda a, b: True),
    }
    ncp = S // NSA_CMP_STRIDE
    cos_t, sin_t = _rope_tables(jnp.arange(S), S)
    cmp_end = jnp.arange(ncp - 1) * NSA_CMP_STRIDE + (NSA_CMP_LEN - 1)
    cos_c, sin_c = _rope_tables(cmp_end, ncp)
    rope = (cos_t, sin_t, cos_c, sin_c)
    h = x[0]
    for l in range(depth):
        h = _layer(h, norm_g[l], w_in[l], b_f[l], b_gate[l], moba_qk_g[l], nsa_q_g[l], nsa_k_g[l], fox_qk_g[l],
                   cmp_pe[l], cmp_w1[l], cmp_w2[l], w_up_moba[l], w_up_nsa[l], w_up_fox[l], w_out[l], tables, rope)
    return h[None]
```

```python
import functools

import numpy as np
import jax
import jax.numpy as jnp
from jax import lax
from jax.experimental import pallas as pl
from jax.experimental.pallas import tpu as pltpu

D_MODEL = 1024
HEAD_DIM = 64
ROPE_DIM = HEAD_DIM // 4
ROPE_THETA = 500000.0
RMS_EPS = 1e-6
NEG_INF = -1e30
POS_BIG = 1e30
M_FLOOR = -1e20

MOBA_HEADS = 6
MOBA_BLOCK = 256
MOBA_TOPK = 3
NSA_HEADS = 4
NSA_CMP_LEN = 32
NSA_CMP_STRIDE = 16
NSA_CMP_HIDDEN = 4 * HEAD_DIM
NSA_SLC_BLOCK = 64
NSA_SLC_TOPN = 16
NSA_WINDOW = 512
FOX_HEADS = 6
N_BRANCH = 3
MOBA_W = MOBA_HEADS * HEAD_DIM
NSA_W = NSA_HEADS * HEAD_DIM
FOX_W = FOX_HEADS * HEAD_DIM
IN_SPLITS = (MOBA_W,) * 4 + (NSA_W,) + (HEAD_DIM,) * 6 + (3 * NSA_HEADS, NSA_W) + (FOX_W,) * 3 + (FOX_HEADS, FOX_W, N_BRANCH * D_MODEL)
ATTN_SCALE = HEAD_DIM ** -0.5

LANES = 128
AUG = 2 * HEAD_DIM
CODE_BLOCKS = AUG - HEAD_DIM
Q_BLOCK = 128
N_ZHEADS = MOBA_HEADS + NSA_HEADS + FOX_HEADS

SEG_A = 2 * MOBA_W + NSA_W + 2 * HEAD_DIM + 2 * FOX_W
SEG_B = 2 * MOBA_W + 4 * HEAD_DIM
SEG_C = MOBA_W + NSA_W + FOX_W
SEG_D = LANES
OFF_B = SEG_A
OFF_C = OFF_B + SEG_B
OFF_D = OFF_C + SEG_C
MAIN_COLS = OFF_D + SEG_D
FF_LANE = 3 * NSA_HEADS
KV_CMP_OFF = OFF_B + 2 * MOBA_W + 2 * HEAD_DIM
ALL_COLS = MAIN_COLS + N_BRANCH * D_MODEL

VMEM_LIMIT = 56 * 1024 * 1024
FLASH_QW = 512

f32 = jnp.float32
bf16 = jnp.bfloat16


def _cparams(sem):
    return pltpu.CompilerParams(dimension_semantics=sem, vmem_limit_bytes=VMEM_LIMIT)


def _iota2(shape, dim):
    return lax.broadcasted_iota(jnp.int32, shape, dim)


def _div(x, d):
    return jnp.right_shift(x, int(d).bit_length() - 1)


def _mod(x, d):
    return jnp.bitwise_and(x, d - 1)


def _place(n_in, n_out, in_off, out_off, width=HEAD_DIM, val=1.0):
    r = _iota2((n_in, n_out), 0) - in_off
    c = _iota2((n_in, n_out), 1) - out_off
    hit = (r == c) & (r >= 0) & (r < width)
    return jnp.where(hit, val, 0.0).astype(bf16)


def _place_t(n_out, n_in, in_off, out_off, width=HEAD_DIM, val=1.0):
    r = _iota2((n_out, n_in), 0) - out_off
    c = _iota2((n_out, n_in), 1) - in_off
    hit = (r == c) & (r >= 0) & (r < width)
    return jnp.where(hit, val, 0.0).astype(bf16)


def _split3(x):
    hi = x.astype(bf16)
    r = x - hi.astype(f32)
    mid = r.astype(bf16)
    lo = (r - mid.astype(f32)).astype(bf16)
    return hi, mid, lo


def _dot(a, b):
    return jnp.dot(a, b, preferred_element_type=f32)


def _dot_nt(a, b):
    return lax.dot_general(a, b, (((1,), (1,)), ((), ())), preferred_element_type=f32)


def _w_in_plan():
    offs = np.concatenate([[0], np.cumsum(IN_SPLITS)])
    names = ("mq", "mk", "mv", "mz", "nq", "kc", "vc", "ksl", "vsl", "kw", "vw", "ng", "nz", "fq", "fk", "fv", "ff", "fz", "gl")
    start = {n: int(offs[j]) for j, n in enumerate(names)}
    width = {n: int(IN_SPLITS[j]) for j, n in enumerate(names)}
    order = ("mq", "mk", "nq", "ksl", "kw", "fq", "fk", "mv", "fv", "vsl", "vw", "kc", "vc", "mz", "nz", "fz", "ng", "ff")
    plan = [[] for _ in range(ALL_COLS // LANES)]
    new = 0
    for n in order + ("pad", "gl"):
        if n == "pad":
            new = MAIN_COLS
            continue
        src, left = start[n], width[n]
        while left > 0:
            w = min(left, LANES - new % LANES, LANES - src % LANES)
            plan[new // LANES].append((src // LANES, src % LANES, new % LANES, w))
            src, new, left = src + w, new + w, left - w
    assert new == ALL_COLS
    return plan


def _repack_kernel(w_ref, o_ref, *, plan, tr, n_cols):
    lane = _iota2((tr, LANES), 1)
    loaded = {}

    def source(a):
        if a not in loaded:
            x = w_ref[0, :, a * LANES:(a + 1) * LANES]
            if (a + 1) * LANES > n_cols:
                x = jnp.where(lane < n_cols - a * LANES, x, 0.0)
            loaded[a] = x.astype(bf16)
        return loaded[a]

    for b, pieces in enumerate(plan):
        acc = jnp.zeros((tr, LANES), f32)
        for a, lane_in, lane_out, w in pieces:
            acc = acc + _dot(source(a), _place(LANES, LANES, lane_in, lane_out, width=w))
        o_ref[:, b * LANES:(b + 1) * LANES] = acc.astype(bf16)


def repack_w_in(w_in, layer, tr=128):
    _, D, n_cols = w_in.shape
    padded = pl.cdiv(n_cols, LANES) * LANES
    return pl.pallas_call(
        functools.partial(_repack_kernel, plan=_w_in_plan(), tr=tr, n_cols=n_cols),
        grid=(D // tr,),
        in_specs=[pl.BlockSpec((1, tr, padded), lambda i: (layer, i, 0))],
        out_specs=pl.BlockSpec((tr, ALL_COLS), lambda i: (i, 0)),
        out_shape=jax.ShapeDtypeStruct((D, ALL_COLS), bf16),
        compiler_params=_cparams(("arbitrary",)),
        name="repack_w_in",
    )(w_in)


def _rms_matmul_kernel(x_ref, g_ref, w_ref, o_ref):
    x = x_ref[...]
    ms = jnp.mean(x * x, axis=-1, keepdims=True)
    h = (x * lax.rsqrt(ms + RMS_EPS) * g_ref[...]).astype(bf16)
    o_ref[...] = _dot(h, w_ref[...])


def rms_matmul(x, g, w, tm=1024, tn=1024):
    S, D = x.shape
    N = w.shape[1]
    return pl.pallas_call(
        _rms_matmul_kernel,
        grid=(N // tn, S // tm),
        in_specs=[pl.BlockSpec((tm, D), lambda j, i: (i, 0)),
                  pl.BlockSpec((1, D), lambda j, i: (0, 0)),
                  pl.BlockSpec((D, tn), lambda j, i: (0, j))],
        out_specs=pl.BlockSpec((tm, tn), lambda j, i: (i, j)),
        out_shape=jax.ShapeDtypeStruct((S, N), f32),
        compiler_params=_cparams(("arbitrary", "arbitrary")),
        name="rms_matmul",
    )(x, g, w)


def _prep_kernel(p_ref, cos_ref, sin_ref, gain_ref, bf_ref,
                 mq_ref, mk_ref, nq_ref, ksl_ref, kw_ref, fq_ref, fk_ref,
                 mv_ref, fv_ref, vsl_ref, vw_ref, zs_ref, carry_sc, *, ts):
    i = pl.program_id(0)

    @pl.when(i == 0)
    def _():
        carry_sc[...] = jnp.zeros_like(carry_sc)

    lane = _iota2((ts, LANES), 1)
    pos = _iota2((ts, LANES), 0) + i * ts
    blockdiag = jnp.where(_div(_iota2((LANES, LANES), 0), HEAD_DIM) == _div(_iota2((LANES, LANES), 1), HEAD_DIM),
                          1.0, 0.0).astype(bf16)
    to64 = [_place(LANES, HEAD_DIM, 0, 0), _place(LANES, HEAD_DIM, HEAD_DIM, 0)]
    to128 = [_place(LANES, AUG, 0, 0), _place(LANES, AUG, HEAD_DIM, 0)]
    rows_scaled = [_place_t(AUG, LANES, 0, 0, val=ATTN_SCALE), _place_t(AUG, LANES, HEAD_DIM, 0, val=ATTN_SCALE)]
    rows = [_place_t(AUG, LANES, 0, 0), _place_t(AUG, LANES, HEAD_DIM, 0)]
    rowi = _iota2((AUG, ts), 0)
    ones_row = jnp.where(rowi == HEAD_DIM, 1.0, 0.0)
    first_half = _mod(lane, HEAD_DIM) < (ROPE_DIM // 2)
    cos = cos_ref[...]
    sin = sin_ref[...]

    def normed(c, rope):
        x = p_ref[:, c * LANES:(c + 1) * LANES]
        x2 = x * x
        hi = x2.astype(bf16)
        lo = (x2 - hi.astype(f32)).astype(bf16)
        ss = _dot(hi, blockdiag) + _dot(lo, blockdiag)
        y = x * lax.rsqrt(ss * (1.0 / HEAD_DIM) + RMS_EPS) * gain_ref[:, c * LANES:(c + 1) * LANES]
        if rope:
            up = pltpu.roll(y, LANES - ROPE_DIM // 2, 1)
            dn = pltpu.roll(y, ROPE_DIM // 2, 1)
            y = y * cos + jnp.where(first_half, up, dn) * sin
        return y.astype(bf16)

    d = p_ref[:, OFF_D:OFF_D + LANES] + bf_ref[...]
    logf = jnp.minimum(d, 0.0) - jnp.log(1.0 + jnp.exp(-jnp.abs(d)))
    tri = jnp.where(_iota2((ts, ts), 1) <= _iota2((ts, ts), 0), 1.0, 0.0).astype(bf16)
    lh, lm, ll = _split3(logf)
    c = carry_sc[0:1, :] + (_dot(tri, lh) + _dot(tri, lm) + _dot(tri, ll))
    carry_sc[...] = jnp.broadcast_to(c[ts - 1:ts, :], carry_sc.shape)
    ch, cm, cl = _split3(c)
    one_q = jnp.where((rowi >= HEAD_DIM + 3) & (rowi < HEAD_DIM + 6), 1.0, 0.0)
    one_k = jnp.where((lane >= HEAD_DIM) & (lane < HEAD_DIM + 3), 1.0, 0.0)

    def decay_cols(h, base):
        src = FF_LANE + h
        return (_dot(ch, _place(LANES, AUG, src, base, width=1))
                + _dot(cm, _place(LANES, AUG, src, base + 1, width=1))
                + _dot(cl, _place(LANES, AUG, src, base + 2, width=1)))

    def decay_rows(h, base):
        src = FF_LANE + h
        return (_dot_nt(_place_t(AUG, LANES, src, base, width=1), ch)
                + _dot_nt(_place_t(AUG, LANES, src, base + 1, width=1), cm)
                + _dot_nt(_place_t(AUG, LANES, src, base + 2, width=1), cl))

    moba_code = jnp.where((lane >= HEAD_DIM) & ((lane - HEAD_DIM) == _div(pos, MOBA_BLOCK)), 1.0, 0.0)
    for c_i in range(3):
        yq = normed(c_i, True)
        yk = normed(3 + c_i, True)
        for half in range(2):
            h = 2 * c_i + half
            mq_ref[h] = _dot(yq, to64[half]).astype(bf16)
            mk_ref[h] = (_dot(yk, to128[half]) + moba_code).astype(bf16)
    for c_i in range(2):
        y = normed(6 + c_i, True)
        for half in range(2):
            h = 2 * c_i + half
            yh = _dot(y, to64[half]).astype(bf16)
            for qb in range(ts // Q_BLOCK):
                nq_ref[(qb * NSA_HEADS + h) * Q_BLOCK:(qb * NSA_HEADS + h + 1) * Q_BLOCK, :] = (
                    yh[qb * Q_BLOCK:(qb + 1) * Q_BLOCK, :])
    y = normed(8, True)
    slc_code = jnp.where((lane >= HEAD_DIM) & ((lane - HEAD_DIM) == _mod(_div(pos, NSA_SLC_BLOCK), CODE_BLOCKS)), 1.0, 0.0)
    ksl_ref[...] = (_dot(y, to128[0]) + slc_code).astype(bf16)
    kw_ref[...] = _dot(y, to128[1]).astype(bf16)
    for c_i in range(3):
        yq = normed(9 + c_i, False)
        yk = normed(12 + c_i, False)
        for half in range(2):
            h = 2 * c_i + half
            fq_ref[h] = (_dot_nt(rows_scaled[half], yq) + decay_rows(h, HEAD_DIM) + one_q).astype(bf16)
            fk_ref[h] = (_dot(yk, to128[half]) - decay_cols(h, HEAD_DIM + 3) + one_k).astype(bf16)
    for c_i in range(3):
        xm = p_ref[:, OFF_B + c_i * LANES:OFF_B + (c_i + 1) * LANES].astype(bf16)
        xf = p_ref[:, OFF_B + MOBA_W + c_i * LANES:OFF_B + MOBA_W + (c_i + 1) * LANES].astype(bf16)
        for half in range(2):
            mv_ref[2 * c_i + half] = (_dot_nt(rows[half], xm) + ones_row).astype(bf16)
            fv_ref[2 * c_i + half] = (_dot_nt(rows[half], xf) + ones_row).astype(bf16)
    xs = p_ref[:, OFF_B + 2 * MOBA_W:OFF_B + 2 * MOBA_W + LANES].astype(bf16)
    vsl_ref[...] = (_dot_nt(rows[0], xs) + ones_row).astype(bf16)
    vw_ref[...] = (_dot_nt(rows[1], xs) + ones_row).astype(bf16)
    for c_i in range(SEG_C // LANES):
        z = p_ref[:, OFF_C + c_i * LANES:OFF_C + (c_i + 1) * LANES]
        zs = (z * (1.0 / (1.0 + jnp.exp(-z)))).astype(bf16)
        for half in range(2):
            zs_ref[2 * c_i + half] = _dot(zs, to64[half]).astype(bf16)


def prep(proj, cos_t, sin_t, gain_row, bf_row, ts=256):
    S = proj.shape[0]
    head64 = lambda n: jax.ShapeDtypeStruct((n, S, HEAD_DIM), bf16)
    head128 = lambda n: jax.ShapeDtypeStruct((n, S, AUG), bf16)
    spec_h = lambda n, w: pl.BlockSpec((n, ts, w), lambda i: (0, i, 0))
    spec_r = lambda w: pl.BlockSpec((ts, w), lambda i: (i, 0))
    head_t = lambda n: jax.ShapeDtypeStruct((n, AUG, S), bf16)
    spec_ht = lambda n: pl.BlockSpec((n, AUG, ts), lambda i: (0, 0, i))
    spec_t = pl.BlockSpec((AUG, ts), lambda i: (0, i))
    out_shape = (head64(MOBA_HEADS), head128(MOBA_HEADS),
                 jax.ShapeDtypeStruct((NSA_HEADS * S, HEAD_DIM), bf16),
                 jax.ShapeDtypeStruct((S, AUG), bf16), jax.ShapeDtypeStruct((S, AUG), bf16),
                 head_t(FOX_HEADS), head128(FOX_HEADS),
                 head_t(MOBA_HEADS), head_t(FOX_HEADS),
                 jax.ShapeDtypeStruct((AUG, S), bf16), jax.ShapeDtypeStruct((AUG, S), bf16),
                 head64(N_ZHEADS))
    out_specs = (spec_h(MOBA_HEADS, HEAD_DIM), spec_h(MOBA_HEADS, AUG),
                 pl.BlockSpec((NSA_HEADS * ts, HEAD_DIM), lambda i: (i, 0)),
                 spec_r(AUG), spec_r(AUG),
                 spec_ht(FOX_HEADS), spec_h(FOX_HEADS, AUG),
                 spec_ht(MOBA_HEADS), spec_ht(FOX_HEADS),
                 spec_t, spec_t,
                 spec_h(N_ZHEADS, HEAD_DIM))
    return pl.pallas_call(
        functools.partial(_prep_kernel, ts=ts),
        grid=(S // ts,),
        in_specs=[pl.BlockSpec((ts, MAIN_COLS), lambda i: (i, 0)),
                  spec_r(LANES), spec_r(LANES),
                  pl.BlockSpec((1, SEG_A), lambda i: (0, 0)),
                  pl.BlockSpec((1, LANES), lambda i: (0, 0))],
        out_specs=out_specs,
        out_shape=out_shape,
        scratch_shapes=[pltpu.VMEM((8, LANES), f32)],
        compiler_params=_cparams(("arbitrary",)),
        name="prep",
    )(proj, cos_t, sin_t, gain_row, bf_row)


def _kmean_kernel(k_ref, o_ref, *, rows):
    n = rows // MOBA_BLOCK
    avg = jnp.where(_div(_iota2((n, rows), 1), MOBA_BLOCK) == _iota2((n, rows), 0),
                    1.0 / MOBA_BLOCK, 0.0).astype(bf16)
    o_ref[0] = _dot(avg, k_ref[0])[:, :HEAD_DIM]


def moba_kmean(mk_aug):
    H, S, _ = mk_aug.shape
    rows = 8 * MOBA_BLOCK
    return pl.pallas_call(
        functools.partial(_kmean_kernel, rows=rows),
        grid=(H, S // rows),
        in_specs=[pl.BlockSpec((1, rows, AUG), lambda h, i: (h, i, 0))],
        out_specs=pl.BlockSpec((1, 8, HEAD_DIM), lambda h, i: (h, i, 0)),
        out_shape=jax.ShapeDtypeStruct((H, CODE_BLOCKS, HEAD_DIM), f32),
        compiler_params=_cparams(("arbitrary", "arbitrary")),
        name="moba_kmean",
    )(mk_aug)


def _top_select(score, idx, n_pick, floor):
    big = jnp.int32(2 ** 30)
    sel = jnp.zeros(score.shape, f32)
    for _ in range(n_pick):
        m = jnp.max(score, axis=1, keepdims=True)
        first = jnp.min(jnp.where(score == m, idx, big), axis=1, keepdims=True)
        pick = (idx == first) & (m > floor)
        sel = jnp.where(pick, 1.0, sel)
        score = jnp.where(pick, -jnp.inf, score)
    return sel


def _moba_select_kernel(q_ref, km_ref, o_ref, *, tq):
    i = pl.program_id(1)
    q = q_ref[0]
    km = km_ref[0]
    km_hi = km.astype(bf16)
    km_lo = (km - km_hi.astype(f32)).astype(bf16)
    gate = _dot_nt(q, km_hi) + _dot_nt(q, km_lo)
    blk = _iota2((tq, CODE_BLOCKS), 1)
    cur = _div(_iota2((tq, CODE_BLOCKS), 0) + i * tq, MOBA_BLOCK)
    score = jnp.where(blk < cur, gate, NEG_INF)
    sel = _top_select(score, blk, MOBA_TOPK, NEG_INF)
    bias = jnp.where((sel > 0.0) | (blk == cur), 0.0, NEG_INF).astype(bf16)
    o_ref[0, 0] = (_dot_nt(_place_t(AUG, HEAD_DIM, 0, 0, val=ATTN_SCALE), q)
                   + _dot_nt(_place_t(AUG, CODE_BLOCKS, 0, HEAD_DIM), bias)).astype(bf16)


def moba_select(mq, kmean, tq=512):
    H, S, _ = mq.shape
    return pl.pallas_call(
        functools.partial(_moba_select_kernel, tq=tq),
        grid=(H, S // tq),
        in_specs=[pl.BlockSpec((1, tq, HEAD_DIM), lambda h, i: (h, i, 0)),
                  pl.BlockSpec((1, CODE_BLOCKS, HEAD_DIM), lambda h, i: (h, 0, 0))],
        out_specs=pl.BlockSpec((1, 1, AUG, tq), lambda h, i: (h, 0, 0, i)),
        out_shape=jax.ShapeDtypeStruct((H, 1, AUG, S), bf16),
        compiler_params=_cparams(("arbitrary", "arbitrary")),
        name="moba_select",
    )(mq, kmean)


def _cmp_kernel(x_ref, pe_ref, w1_ref, w2_ref, gain_ref, cos_ref, sin_ref, kc_ref, vc_ref, *, ncp):
    top = jnp.zeros((ncp, 2 * NSA_CMP_HIDDEN), f32)
    nxt = jnp.zeros((ncp, 2 * NSA_CMP_HIDDEN), f32)
    for j in range(NSA_CMP_STRIDE):
        xj = x_ref[pl.ds(j, ncp, stride=NSA_CMP_STRIDE), :]
        top = top + _dot((xj + pe_ref[j:j + 1, :]).astype(bf16), w1_ref[j])
        nxt = nxt + _dot((xj + pe_ref[NSA_CMP_STRIDE + j:NSA_CMP_STRIDE + j + 1, :]).astype(bf16),
                         w1_ref[NSA_CMP_STRIDE + j])
    hid = top + pltpu.roll(nxt, ncp - 1, 0)
    act = (hid * (1.0 / (1.0 + jnp.exp(-hid)))).astype(bf16)
    k = _dot(act[:, :NSA_CMP_HIDDEN], w2_ref[0])
    v = _dot(act[:, NSA_CMP_HIDDEN:], w2_ref[1])
    ms = jnp.sum(k * k, axis=-1, keepdims=True) * (1.0 / HEAD_DIM)
    y = k * lax.rsqrt(ms + RMS_EPS) * gain_ref[...]
    lane = _iota2((ncp, LANES), 1)
    up = pltpu.roll(y, LANES - ROPE_DIM // 2, 1)
    dn = pltpu.roll(y, ROPE_DIM // 2, 1)
    y = y * cos_ref[...] + jnp.where(lane < ROPE_DIM // 2, up, dn) * sin_ref[...]
    kc_ref[...] = y[:, :HEAD_DIM].astype(bf16)
    vc_ref[...] = v[:, :HEAD_DIM].astype(bf16)


def nsa_compress(proj, pe, w1, w2, gain, cos_c, sin_c):
    S = proj.shape[0]
    ncp = S // NSA_CMP_STRIDE
    full = lambda a: pl.BlockSpec(a.shape, lambda i: (0,) * a.ndim)
    return pl.pallas_call(
        functools.partial(_cmp_kernel, ncp=ncp),
        grid=(1,),
        in_specs=[pl.BlockSpec((S, LANES), lambda i: (0, KV_CMP_OFF // LANES)),
                  full(pe), full(w1), full(w2), full(gain), full(cos_c), full(sin_c)],
        out_specs=(pl.BlockSpec((ncp, HEAD_DIM), lambda i: (0, 0)), pl.BlockSpec((ncp, HEAD_DIM), lambda i: (0, 0))),
        out_shape=(jax.ShapeDtypeStruct((ncp, HEAD_DIM), bf16), jax.ShapeDtypeStruct((ncp, HEAD_DIM), bf16)),
        compiler_params=_cparams(("arbitrary",)),
        name="nsa_compress",
    )(proj, pe, w1, w2, gain, cos_c, sin_c)


def _nsa_cmp_select_kernel(q_ref, kc_ref, vc_ref, oc_ref, qa_ref, *, ncp, nsp, n_win):
    i = pl.program_id(0)
    rows = NSA_HEADS * Q_BLOCK
    q = q_ref[...]
    s = _dot_nt(q, kc_ref[...]) * ATTN_SCALE
    qpos = i * Q_BLOCK + _mod(_iota2((rows, ncp), 0), Q_BLOCK)
    kend = _iota2((rows, ncp), 1) * NSA_CMP_STRIDE + (NSA_CMP_LEN - 1)
    ok = kend <= qpos
    sm = jnp.where(ok, s, NEG_INF)
    e = jnp.exp(sm - jnp.max(sm, axis=1, keepdims=True))
    p = jnp.where(ok, e * (1.0 / jnp.sum(e, axis=1, keepdims=True)), 0.0)
    oc_ref[...] = _dot(p.astype(bf16), vc_ref[...]).astype(bf16)
    psum = p[0:Q_BLOCK] + p[Q_BLOCK:2 * Q_BLOCK] + p[2 * Q_BLOCK:3 * Q_BLOCK] + p[3 * Q_BLOCK:4 * Q_BLOCK]
    c0 = _iota2((ncp, nsp), 0) * NSA_CMP_STRIDE
    b0 = _iota2((ncp, nsp), 1) * NSA_SLC_BLOCK
    overlap = jnp.where((c0 <= b0 + (NSA_SLC_BLOCK - 1)) & (c0 + (NSA_CMP_LEN - 1) >= b0), 1.0, 0.0).astype(bf16)
    ph, pm, plo = _split3(psum)
    imp = _dot(ph, overlap) + _dot(pm, overlap) + _dot(plo, overlap)
    blk = _iota2((Q_BLOCK, nsp), 1)
    cur = _div(_iota2((Q_BLOCK, nsp), 0) + i * Q_BLOCK, NSA_SLC_BLOCK)
    score = jnp.where(blk <= cur, imp, NEG_INF)
    forced = (blk == 0) | (blk == cur) | (blk == cur - 1)
    score = jnp.where(forced, POS_BIG, score)
    sel = _top_select(score, blk, NSA_SLC_TOPN, NEG_INF)
    bias = jnp.where(sel > 0.0, 0.0, NEG_INF).astype(bf16)
    qs = _dot_nt(_place_t(AUG, HEAD_DIM, 0, 0, val=ATTN_SCALE), q)
    for w in range(n_win):
        bw = _dot_nt(_place_t(AUG, nsp, w * CODE_BLOCKS, HEAD_DIM, width=CODE_BLOCKS), bias)
        qa_ref[0, w] = (qs + jnp.concatenate([bw] * NSA_HEADS, axis=1)).astype(bf16)


def nsa_cmp_select(nq, kc, vc, S):
    ncp = kc.shape[0]
    ns = S // NSA_SLC_BLOCK
    nsp = max(LANES, ns)
    n_win = max(1, ns // CODE_BLOCKS)
    rows = NSA_HEADS * Q_BLOCK
    return pl.pallas_call(
        functools.partial(_nsa_cmp_select_kernel, ncp=ncp, nsp=nsp, n_win=n_win),
        grid=(S // Q_BLOCK,),
        in_specs=[pl.BlockSpec((rows, HEAD_DIM), lambda i: (i, 0)),
                  pl.BlockSpec((ncp, HEAD_DIM), lambda i: (0, 0)),
                  pl.BlockSpec((ncp, HEAD_DIM), lambda i: (0, 0))],
        out_specs=(pl.BlockSpec((rows, HEAD_DIM), lambda i: (i, 0)),
                   pl.BlockSpec((1, n_win, AUG, rows), lambda i: (0, 0, 0, i))),
        out_shape=(jax.ShapeDtypeStruct((NSA_HEADS * S, HEAD_DIM), bf16),
                   jax.ShapeDtypeStruct((1, n_win, AUG, NSA_HEADS * S), bf16)),
        compiler_params=_cparams(("arbitrary",)),
        name="nsa_cmp_select",
    )(nq, kc, vc)


def _flash_kernel(qi_t, ki_t, win_t, first_t, last_t, mask_t, q_ref, k_ref, v_ref, o_ref, m_sc, acc_sc,
                  *, n_heads, tq, tk, qw, tq_pos, rep_cols, pos_block, window):
    p_id = pl.program_id(0)
    qi = qi_t[p_id]
    ki = ki_t[p_id]

    @pl.when(first_t[p_id] == 1)
    def _():
        m_sc[...] = jnp.full(m_sc.shape, M_FLOOR, f32)
        acc_sc[...] = jnp.zeros_like(acc_sc)

    def tile(masked):
        chains = [(g, slice(c * qw, (c + 1) * qw)) for g in range(n_heads) for c in range(tq // qw)]
        if masked:
            rel = _mod(_iota2((tk, qw), 1), pos_block) - _iota2((tk, qw), 0)

        def scores(t):
            g, cols = chains[t]
            s = _dot(k_ref[g], q_ref[g, 0, :, cols])
            if not masked:
                return s
            base = qi * tq_pos + (cols.start // rep_cols) * pos_block + cols.start % pos_block - ki * tk
            delta = rel + base
            ok = (delta >= 0) if window is None else (delta.astype(jnp.uint32) < jnp.uint32(window))
            return jnp.where(ok, s, NEG_INF)

        def softmax(t, s):
            g, cols = chains[t]
            m_prev = m_sc[g, 0:1, cols]
            m_new = jnp.maximum(m_prev, jnp.max(s, axis=0, keepdims=True))
            m_sc[g, :, cols] = jnp.broadcast_to(m_new, (m_sc.shape[1], qw))
            return jnp.exp(s - m_new).astype(bf16), jnp.exp(m_prev - m_new)

        def accumulate(t, p, alpha):
            g, cols = chains[t]
            acc_sc[g, :, cols] = alpha * acc_sc[g, :, cols] + _dot(v_ref[g], p)

        n = len(chains)
        s_next, staged = scores(0), None
        for t in range(n + 1):
            s = s_next
            if t + 1 < n:
                s_next = scores(t + 1)
            if staged is not None:
                accumulate(t - 1, *staged)
            staged = softmax(t, s) if t < n else None

    @pl.when(mask_t[p_id] == 1)
    def _():
        tile(True)

    @pl.when(mask_t[p_id] == 0)
    def _():
        tile(False)

    @pl.when(last_t[p_id] == 1)
    def _():
        for g in range(n_heads):
            acc = acc_sc[g]
            l = acc[HEAD_DIM:HEAD_DIM + 1, :]
            inv = jnp.where(l > 0.0, 1.0 / l, 0.0)
            o_ref[g] = (acc * inv).T[:, :HEAD_DIM].astype(o_ref.dtype)


def _pair_tables(n_q, k_range, tk, win_keys, needs_mask):
    qi, ki, wi, first, last, mask = [], [], [], [], [], []
    for a in range(n_q):
        lo, hi = k_range(a)
        for b in range(lo, hi + 1):
            qi.append(a)
            ki.append(b)
            wi.append((b * tk) // win_keys)
            first.append(int(b == lo))
            last.append(int(b == hi))
            mask.append(int(needs_mask(a, b)))
    return tuple(jnp.asarray(np.asarray(t, np.int32)) for t in (qi, ki, wi, first, last, mask))


def flash(q, k, v, tables, *, tq, tk, pos_block, window, name):
    G, _, _, Sq = q.shape
    tq_pos = 512
    rep_cols = (tq // tq_pos) * pos_block
    assert rep_cols % FLASH_QW == 0 and (pos_block % FLASH_QW == 0 or FLASH_QW % pos_block == 0)
    n_pairs = tables[0].shape[0]
    grid_spec = pltpu.PrefetchScalarGridSpec(
        num_scalar_prefetch=6,
        grid=(n_pairs,),
        in_specs=[pl.BlockSpec((G, 1, AUG, tq), lambda p, qi, ki, wi, fi, la, ma: (0, wi[p], 0, qi[p])),
                  pl.BlockSpec((G, tk, AUG), lambda p, qi, ki, wi, fi, la, ma: (0, ki[p], 0)),
                  pl.BlockSpec((G, AUG, tk), lambda p, qi, ki, wi, fi, la, ma: (0, 0, ki[p]))],
        out_specs=pl.BlockSpec((G, tq, HEAD_DIM), lambda p, qi, ki, wi, fi, la, ma: (0, qi[p], 0)),
        scratch_shapes=[pltpu.VMEM((G, 8, tq), f32), pltpu.VMEM((G, AUG, tq), f32)],
    )
    return pl.pallas_call(
        functools.partial(_flash_kernel, n_heads=G, tq=tq, tk=tk, qw=FLASH_QW, tq_pos=tq_pos, rep_cols=rep_cols,
                          pos_block=pos_block, window=window),
        grid_spec=grid_spec,
        out_shape=jax.ShapeDtypeStruct((G, Sq, HEAD_DIM), bf16),
        compiler_params=_cparams(("arbitrary",)),
        name=name,
    )(*tables, q, k, v)


def _out_kernel(x_ref, gl0_ref, gl1_ref, gl2_ref, sd_ref, bg_ref, om_ref, of_ref, oc_ref, os_ref, ow_ref, zs_ref,
                wm_ref, wn_ref, wf_ref, wo_ref, o_ref, *, tm):
    def up(o_of_head, n_heads, z_off, w_ref):
        y = jnp.zeros((tm, D_MODEL), f32)
        for h in range(n_heads):
            g = (o_of_head(h) * zs_ref[z_off + h].astype(f32)).astype(bf16)
            y = y + _dot(g, w_ref[h])
        return y

    sg = 1.0 / (1.0 + jnp.exp(-sd_ref[...]))

    def nsa_head(h):
        rows = slice(h * Q_BLOCK, (h + 1) * Q_BLOCK)
        return (sg[:, 3 * h:3 * h + 1] * oc_ref[rows, :].astype(f32)
                + sg[:, 3 * h + 1:3 * h + 2] * os_ref[rows, :].astype(f32)
                + sg[:, 3 * h + 2:3 * h + 3] * ow_ref[rows, :].astype(f32))

    y_m = up(lambda h: om_ref[h].astype(f32), MOBA_HEADS, 0, wm_ref)
    y_n = up(nsa_head, NSA_HEADS, MOBA_HEADS, wn_ref)
    y_f = up(lambda h: of_ref[h].astype(f32), FOX_HEADS, MOBA_HEADS + NSA_HEADS, wf_ref)
    merged = jnp.zeros((tm, D_MODEL), f32)
    for b, (y, gl_ref) in enumerate(((y_m, gl0_ref), (y_n, gl1_ref), (y_f, gl2_ref))):
        cols = slice(b * D_MODEL, (b + 1) * D_MODEL)
        merged = merged + y * (1.0 / (1.0 + jnp.exp(-(gl_ref[...] + bg_ref[:, cols]))))
    o_ref[...] = x_ref[...] + _dot(merged.astype(bf16), wo_ref[...])


def out_proj(x, proj, b_gate, o_m, o_f, o_c, o_s, o_w, zs, w_m, w_n, w_f, w_o):
    S = x.shape[0]
    tm = Q_BLOCK
    rows = NSA_HEADS * Q_BLOCK
    full = lambda a: pl.BlockSpec(a.shape, lambda i: (0,) * a.ndim)
    gl_spec = lambda b: pl.BlockSpec((tm, D_MODEL), lambda i: (i, MAIN_COLS // D_MODEL + b))
    return pl.pallas_call(
        functools.partial(_out_kernel, tm=tm),
        grid=(S // tm,),
        in_specs=[pl.BlockSpec((tm, D_MODEL), lambda i: (i, 0)),
                  gl_spec(0), gl_spec(1), gl_spec(2),
                  pl.BlockSpec((tm, LANES), lambda i: (i, OFF_D // LANES)),
                  full(b_gate),
                  pl.BlockSpec((MOBA_HEADS, tm, HEAD_DIM), lambda i: (0, i, 0)),
                  pl.BlockSpec((FOX_HEADS, tm, HEAD_DIM), lambda i: (0, i, 0)),
                  pl.BlockSpec((rows, HEAD_DIM), lambda i: (i, 0)),
                  pl.BlockSpec((rows, HEAD_DIM), lambda i: (i, 0)),
                  pl.BlockSpec((rows, HEAD_DIM), lambda i: (i, 0)),
                  pl.BlockSpec((N_ZHEADS, tm, HEAD_DIM), lambda i: (0, i, 0)),
                  full(w_m), full(w_n), full(w_f), full(w_o)],
        out_specs=pl.BlockSpec((tm, D_MODEL), lambda i: (i, 0)),
        out_shape=jax.ShapeDtypeStruct((S, D_MODEL), f32),
        compiler_params=_cparams(("arbitrary",)),
        name="out_proj",
    )(x, proj, proj, proj, proj, b_gate, o_m, o_f, o_c, o_s, o_w, zs, w_m, w_n, w_f, w_o)


def _rope_tables(pos, rows):
    inv = ROPE_THETA ** (-jnp.arange(0, ROPE_DIM, 2, dtype=f32) / ROPE_DIM)
    ang = pos.astype(f32)[:, None] * inv[None, :]
    cos, sin = jnp.cos(ang), jnp.sin(ang)
    n = pos.shape[0]
    rest = HEAD_DIM - ROPE_DIM
    cos_h = jnp.concatenate([cos, cos, jnp.ones((n, rest), f32)], axis=1)
    sin_h = jnp.concatenate([-sin, sin, jnp.zeros((n, rest), f32)], axis=1)
    cos_t = jnp.concatenate([cos_h, cos_h], axis=1)
    sin_t = jnp.concatenate([sin_h, sin_h], axis=1)
    pad = ((0, rows - n), (0, 0))
    return jnp.pad(cos_t, pad), jnp.pad(sin_t, pad)


def _layer(x, norm_g, w_main, b_f, b_gate, moba_qk_g, nsa_q_g, nsa_k_g, fox_qk_g,
           cmp_pe, cmp_w1, cmp_w2, w_up_moba, w_up_nsa, w_up_fox, w_out, tables, rope):
    S = x.shape[0]
    proj = rms_matmul(x, norm_g.reshape(1, D_MODEL), w_main)

    gain_row = jnp.concatenate([jnp.tile(moba_qk_g[0], MOBA_HEADS), jnp.tile(moba_qk_g[1], MOBA_HEADS),
                                jnp.tile(nsa_q_g, NSA_HEADS), nsa_k_g[1], nsa_k_g[2],
                                jnp.tile(fox_qk_g[0], FOX_HEADS), jnp.tile(fox_qk_g[1], FOX_HEADS)]).reshape(1, SEG_A)
    bf_row = jnp.zeros((1, LANES), f32).at[0, FF_LANE:FF_LANE + FOX_HEADS].set(b_f)
    (mq, mk, nq, ksl, kw, fq, fk, mv, fv, vsl, vw, zs) = prep(proj, rope[0], rope[1], gain_row, bf_row)

    q_moba = moba_select(mq, moba_kmean(mk))
    o_m = flash(q_moba, mk, mv, tables["causal"], tq=512, tk=512, pos_block=512, window=None, name="flash_moba")

    pe = jnp.concatenate([cmp_pe[0], cmp_pe[1]], axis=1)
    w1 = cmp_w1.astype(bf16).reshape(2, NSA_CMP_LEN, HEAD_DIM, NSA_CMP_HIDDEN)
    zero = jnp.zeros_like(w1[0])
    w1 = jnp.concatenate([jnp.concatenate([w1[0], zero], axis=2),
                          jnp.concatenate([zero, w1[1]], axis=2)], axis=1)
    w2p = jnp.pad(cmp_w2, ((0, 0), (0, 0), (0, LANES - HEAD_DIM))).astype(bf16)
    gain_c = jnp.pad(nsa_k_g[0], (0, LANES - HEAD_DIM)).reshape(1, LANES)
    kc, vc = nsa_compress(proj, pe, w1, w2p, gain_c, rope[2], rope[3])
    o_c, q_slc = nsa_cmp_select(nq, kc, vc, S)
    nsa_cols = NSA_HEADS * 512
    o_s = flash(q_slc, ksl[None], vsl[None], tables["slc"], tq=nsa_cols, tk=512,
                pos_block=Q_BLOCK, window=None, name="flash_nsa_slc")[0]
    o_w = flash(q_slc, kw[None], vw[None], tables["win"], tq=nsa_cols, tk=512,
                pos_block=Q_BLOCK, window=NSA_WINDOW, name="flash_nsa_win")[0]

    o_f = flash(fq[:, None], fk, fv, tables["causal"], tq=512, tk=512, pos_block=512, window=None, name="flash_fox")
    return out_proj(x, proj, b_gate.reshape(1, N_BRANCH * D_MODEL), o_m, o_f, o_c, o_s, o_w, zs,
                    w_up_moba.astype(bf16).reshape(MOBA_HEADS, HEAD_DIM, D_MODEL),
                    w_up_nsa.astype(bf16).reshape(NSA_HEADS, HEAD_DIM, D_MODEL),
                    w_up_fox.astype(bf16).reshape(FOX_HEADS, HEAD_DIM, D_MODEL),
                    w_out.astype(bf16))


def kernel(x, norm_g, w_in, b_f, b_gate, moba_qk_g, nsa_q_g, nsa_k_g, fox_qk_g, cmp_pe, cmp_w1, cmp_w2,
           w_up_moba, w_up_nsa, w_up_fox, w_out):
    B, S, _ = x.shape
    assert B == 1 and S % 2048 == 0 and S // MOBA_BLOCK <= CODE_BLOCKS
    depth = norm_g.shape[0]
    tk = 512
    win_keys = CODE_BLOCKS * NSA_SLC_BLOCK
    tables = {
        "causal": _pair_tables(S // 512, lambda a: (0, a), tk, S, lambda a, b: a == b),
        "slc": _pair_tables(S // 512, lambda a: (0, a), tk, win_keys, lambda a, b: a == b),
        "win": _pair_tables(S // 512, lambda a: (max(0, a - 1), a), tk, S, lambda a, b: True),
    }
    ncp = S // NSA_CMP_STRIDE
    cos_t, sin_t = _rope_tables(jnp.arange(S), S)
    cmp_end = jnp.arange(ncp - 1) * NSA_CMP_STRIDE + (NSA_CMP_LEN - 1)
    cos_c, sin_c = _rope_tables(cmp_end, ncp)
    rope = (cos_t, sin_t, cos_c, sin_c)
    h = x[0]
    for l in range(depth):
        h = _layer(h, norm_g[l], repack_w_in(w_in, l), b_f[l], b_gate[l], moba_qk_g[l], nsa_q_g[l], nsa_k_g[l], fox_qk_g[l],
                   cmp_pe[l], cmp_w1[l], cmp_w2[l], w_up_moba[l], w_up_nsa[l], w_up_fox[l], w_out[l], tables, rope)
    return h[None]
```

```python
import functools

import numpy as np
import jax
import jax.numpy as jnp
from jax import lax
from jax.experimental import pallas as pl
from jax.experimental.pallas import tpu as pltpu

D_MODEL = 1024
HEAD_DIM = 64
ROPE_DIM = HEAD_DIM // 4
ROPE_THETA = 500000.0
RMS_EPS = 1e-6
NEG_INF = -1e30
POS_BIG = 1e30
M_FLOOR = -1e20

MOBA_HEADS = 6
MOBA_BLOCK = 256
MOBA_TOPK = 3
NSA_HEADS = 4
NSA_CMP_LEN = 32
NSA_CMP_STRIDE = 16
NSA_CMP_HIDDEN = 4 * HEAD_DIM
NSA_SLC_BLOCK = 64
NSA_SLC_TOPN = 16
NSA_WINDOW = 512
FOX_HEADS = 6
N_BRANCH = 3
MOBA_W = MOBA_HEADS * HEAD_DIM
NSA_W = NSA_HEADS * HEAD_DIM
FOX_W = FOX_HEADS * HEAD_DIM
IN_SPLITS = (MOBA_W,) * 4 + (NSA_W,) + (HEAD_DIM,) * 6 + (3 * NSA_HEADS, NSA_W) + (FOX_W,) * 3 + (FOX_HEADS, FOX_W, N_BRANCH * D_MODEL)
ATTN_SCALE = HEAD_DIM ** -0.5

LANES = 128
AUG = 2 * HEAD_DIM
CODE_BLOCKS = AUG - HEAD_DIM
Q_BLOCK = 128
N_ZHEADS = MOBA_HEADS + NSA_HEADS + FOX_HEADS

SEG_A = 2 * MOBA_W + NSA_W + 2 * HEAD_DIM + 2 * FOX_W
SEG_B = 2 * MOBA_W + 4 * HEAD_DIM
SEG_C = MOBA_W + NSA_W + FOX_W
SEG_D = LANES
OFF_B = SEG_A
OFF_C = OFF_B + SEG_B
OFF_D = OFF_C + SEG_C
MAIN_COLS = OFF_D + SEG_D
FF_LANE = 3 * NSA_HEADS
KV_CMP_OFF = OFF_B + 2 * MOBA_W + 2 * HEAD_DIM
ALL_COLS = MAIN_COLS + N_BRANCH * D_MODEL

VMEM_LIMIT = 56 * 1024 * 1024
FLASH_QW = 512
LOG2E = 1.4426950408889634
Q_SCALE = ATTN_SCALE * LOG2E
V_ROWS = 80

f32 = jnp.float32
bf16 = jnp.bfloat16


def _cparams(sem):
    return pltpu.CompilerParams(dimension_semantics=sem, vmem_limit_bytes=VMEM_LIMIT)


def _iota2(shape, dim):
    return lax.broadcasted_iota(jnp.int32, shape, dim)


def _div(x, d):
    return jnp.right_shift(x, int(d).bit_length() - 1)


def _mod(x, d):
    return jnp.bitwise_and(x, d - 1)


def _place(n_in, n_out, in_off, out_off, width=HEAD_DIM, val=1.0):
    r = _iota2((n_in, n_out), 0) - in_off
    c = _iota2((n_in, n_out), 1) - out_off
    hit = (r == c) & (r >= 0) & (r < width)
    return jnp.where(hit, val, 0.0).astype(bf16)


def _place_t(n_out, n_in, in_off, out_off, width=HEAD_DIM, val=1.0):
    r = _iota2((n_out, n_in), 0) - out_off
    c = _iota2((n_out, n_in), 1) - in_off
    hit = (r == c) & (r >= 0) & (r < width)
    return jnp.where(hit, val, 0.0).astype(bf16)


def _split3(x):
    hi = x.astype(bf16)
    r = x - hi.astype(f32)
    mid = r.astype(bf16)
    lo = (r - mid.astype(f32)).astype(bf16)
    return hi, mid, lo


def _dot(a, b):
    return jnp.dot(a, b, preferred_element_type=f32)


def _dot_nt(a, b):
    return lax.dot_general(a, b, (((1,), (1,)), ((), ())), preferred_element_type=f32)


def _w_in_plan():
    offs = np.concatenate([[0], np.cumsum(IN_SPLITS)])
    names = ("mq", "mk", "mv", "mz", "nq", "kc", "vc", "ksl", "vsl", "kw", "vw", "ng", "nz", "fq", "fk", "fv", "ff", "fz", "gl")
    start = {n: int(offs[j]) for j, n in enumerate(names)}
    width = {n: int(IN_SPLITS[j]) for j, n in enumerate(names)}
    order = ("mq", "mk", "nq", "ksl", "kw", "fq", "fk", "mv", "fv", "vsl", "vw", "kc", "vc", "mz", "nz", "fz", "ng", "ff")
    plan = [[] for _ in range(ALL_COLS // LANES)]
    new = 0
    for n in order + ("pad", "gl"):
        if n == "pad":
            new = MAIN_COLS
            continue
        src, left = start[n], width[n]
        while left > 0:
            w = min(left, LANES - new % LANES, LANES - src % LANES)
            plan[new // LANES].append((src // LANES, src % LANES, new % LANES, w))
            src, new, left = src + w, new + w, left - w
    assert new == ALL_COLS
    return plan


def _repack_kernel(w_ref, o_ref, *, plan, tr, n_cols):
    lane = _iota2((tr, LANES), 1)
    loaded = {}

    def source(a):
        if a not in loaded:
            x = w_ref[0, :, a * LANES:(a + 1) * LANES]
            if (a + 1) * LANES > n_cols:
                x = jnp.where(lane < n_cols - a * LANES, x, 0.0)
            loaded[a] = x.astype(bf16)
        return loaded[a]

    for b, pieces in enumerate(plan):
        acc = jnp.zeros((tr, LANES), f32)
        for a, lane_in, lane_out, w in pieces:
            acc = acc + _dot(source(a), _place(LANES, LANES, lane_in, lane_out, width=w))
        o_ref[:, b * LANES:(b + 1) * LANES] = acc.astype(bf16)


def repack_w_in(w_in, layer, tr=128):
    _, D, n_cols = w_in.shape
    padded = pl.cdiv(n_cols, LANES) * LANES
    return pl.pallas_call(
        functools.partial(_repack_kernel, plan=_w_in_plan(), tr=tr, n_cols=n_cols),
        grid=(D // tr,),
        in_specs=[pl.BlockSpec((1, tr, padded), lambda i: (layer, i, 0))],
        out_specs=pl.BlockSpec((tr, ALL_COLS), lambda i: (i, 0)),
        out_shape=jax.ShapeDtypeStruct((D, ALL_COLS), bf16),
        compiler_params=_cparams(("arbitrary",)),
        name="repack_w_in",
    )(w_in)


def _rms_matmul_kernel(x_ref, g_ref, w_ref, o_ref):
    x = x_ref[...]
    ms = jnp.mean(x * x, axis=-1, keepdims=True)
    h = (x * lax.rsqrt(ms + RMS_EPS) * g_ref[...]).astype(bf16)
    o_ref[...] = _dot(h, w_ref[...])


def rms_matmul(x, g, w, tm=1024, tn=1024):
    S, D = x.shape
    N = w.shape[1]
    return pl.pallas_call(
        _rms_matmul_kernel,
        grid=(N // tn, S // tm),
        in_specs=[pl.BlockSpec((tm, D), lambda j, i: (i, 0)),
                  pl.BlockSpec((1, D), lambda j, i: (0, 0)),
                  pl.BlockSpec((D, tn), lambda j, i: (0, j))],
        out_specs=pl.BlockSpec((tm, tn), lambda j, i: (i, j)),
        out_shape=jax.ShapeDtypeStruct((S, N), f32),
        compiler_params=_cparams(("arbitrary", "arbitrary")),
        name="rms_matmul",
    )(x, g, w)


def _prep_kernel(p_ref, cos_ref, sin_ref, gain_ref, bf_ref,
                 mq_ref, mk_ref, nq_ref, ksl_ref, kw_ref, fq_ref, fk_ref,
                 mv_ref, fv_ref, vsl_ref, vw_ref, zs_ref, carry_sc, *, ts):
    i = pl.program_id(0)

    @pl.when(i == 0)
    def _():
        carry_sc[...] = jnp.zeros_like(carry_sc)

    lane = _iota2((ts, LANES), 1)
    pos = _iota2((ts, LANES), 0) + i * ts
    blockdiag = jnp.where(_div(_iota2((LANES, LANES), 0), HEAD_DIM) == _div(_iota2((LANES, LANES), 1), HEAD_DIM),
                          1.0, 0.0).astype(bf16)
    to64 = [_place(LANES, HEAD_DIM, 0, 0), _place(LANES, HEAD_DIM, HEAD_DIM, 0)]
    to128 = [_place(LANES, AUG, 0, 0), _place(LANES, AUG, HEAD_DIM, 0)]
    rows_q = [_place_t(HEAD_DIM, LANES, 0, 0), _place_t(HEAD_DIM, LANES, HEAD_DIM, 0)]
    rows_aug = [_place_t(AUG, LANES, 0, 0), _place_t(AUG, LANES, HEAD_DIM, 0)]
    rows_v = [_place_t(V_ROWS, LANES, 0, 0), _place_t(V_ROWS, LANES, HEAD_DIM, 0)]
    rowi = _iota2((AUG, ts), 0)
    ones_row = jnp.where(_iota2((V_ROWS, ts), 0) == HEAD_DIM, 1.0, 0.0)
    first_half = _mod(lane, HEAD_DIM) < (ROPE_DIM // 2)
    cos = cos_ref[...]
    sin = sin_ref[...]

    def normed(c, rope, scale=None):
        x = p_ref[:, c * LANES:(c + 1) * LANES]
        x2 = x * x
        hi = x2.astype(bf16)
        lo = (x2 - hi.astype(f32)).astype(bf16)
        ss = _dot(hi, blockdiag) + _dot(lo, blockdiag)
        y = x * lax.rsqrt(ss * (1.0 / HEAD_DIM) + RMS_EPS) * gain_ref[:, c * LANES:(c + 1) * LANES]
        if rope:
            up = pltpu.roll(y, LANES - ROPE_DIM // 2, 1)
            dn = pltpu.roll(y, ROPE_DIM // 2, 1)
            y = y * cos + jnp.where(first_half, up, dn) * sin
        if scale is not None:
            y = y * scale
        return y.astype(bf16)

    d = p_ref[:, OFF_D:OFF_D + LANES] + bf_ref[...]
    logf = jnp.minimum(d, 0.0) - jnp.log(1.0 + jnp.exp(-jnp.abs(d)))
    tri = jnp.where(_iota2((ts, ts), 1) <= _iota2((ts, ts), 0), 1.0, 0.0).astype(bf16)
    lh, lm, ll = _split3(logf)
    c = carry_sc[0:1, :] + (_dot(tri, lh) + _dot(tri, lm) + _dot(tri, ll))
    carry_sc[...] = jnp.broadcast_to(c[ts - 1:ts, :], carry_sc.shape)
    ch, cm, cl = _split3(c * LOG2E)
    one_q = jnp.where((rowi >= HEAD_DIM + 3) & (rowi < HEAD_DIM + 6), 1.0, 0.0)
    one_k = jnp.where((lane >= HEAD_DIM) & (lane < HEAD_DIM + 3), 1.0, 0.0)

    def decay_cols(h, base):
        src = FF_LANE + h
        return (_dot(ch, _place(LANES, AUG, src, base, width=1))
                + _dot(cm, _place(LANES, AUG, src, base + 1, width=1))
                + _dot(cl, _place(LANES, AUG, src, base + 2, width=1)))

    def decay_rows(h, base):
        src = FF_LANE + h
        return (_dot_nt(_place_t(AUG, LANES, src, base, width=1), ch)
                + _dot_nt(_place_t(AUG, LANES, src, base + 1, width=1), cm)
                + _dot_nt(_place_t(AUG, LANES, src, base + 2, width=1), cl))

    moba_code = jnp.where((lane >= HEAD_DIM) & ((lane - HEAD_DIM) == _div(pos, MOBA_BLOCK)), 1.0, 0.0)
    for c_i in range(3):
        yq = normed(c_i, True, Q_SCALE)
        yk = normed(3 + c_i, True)
        for half in range(2):
            h = 2 * c_i + half
            mq_ref[h] = _dot_nt(rows_q[half], yq).astype(bf16)
            mk_ref[h] = (_dot(yk, to128[half]) + moba_code).astype(bf16)
    for c_i in range(2):
        y = normed(6 + c_i, True, Q_SCALE)
        for half in range(2):
            h = 2 * c_i + half
            yh = _dot_nt(rows_q[half], y).astype(bf16)
            for qb in range(ts // Q_BLOCK):
                nq_ref[:, (qb * NSA_HEADS + h) * Q_BLOCK:(qb * NSA_HEADS + h + 1) * Q_BLOCK] = (
                    yh[:, qb * Q_BLOCK:(qb + 1) * Q_BLOCK])
    y = normed(8, True)
    slc_code = jnp.where((lane >= HEAD_DIM) & ((lane - HEAD_DIM) == _mod(_div(pos, NSA_SLC_BLOCK), CODE_BLOCKS)), 1.0, 0.0)
    ksl_ref[...] = (_dot(y, to128[0]) + slc_code).astype(bf16)
    kw_ref[...] = _dot(y, to128[1]).astype(bf16)
    for c_i in range(3):
        yq = normed(9 + c_i, False, Q_SCALE)
        yk = normed(12 + c_i, False)
        for half in range(2):
            h = 2 * c_i + half
            fq_ref[h] = (_dot_nt(rows_aug[half], yq) + decay_rows(h, HEAD_DIM) + one_q).astype(bf16)
            fk_ref[h] = (_dot(yk, to128[half]) - decay_cols(h, HEAD_DIM + 3) + one_k).astype(bf16)
    for c_i in range(3):
        xm = p_ref[:, OFF_B + c_i * LANES:OFF_B + (c_i + 1) * LANES].astype(bf16)
        xf = p_ref[:, OFF_B + MOBA_W + c_i * LANES:OFF_B + MOBA_W + (c_i + 1) * LANES].astype(bf16)
        for half in range(2):
            mv_ref[2 * c_i + half] = (_dot_nt(rows_v[half], xm) + ones_row).astype(bf16)
            fv_ref[2 * c_i + half] = (_dot_nt(rows_v[half], xf) + ones_row).astype(bf16)
    xs = p_ref[:, OFF_B + 2 * MOBA_W:OFF_B + 2 * MOBA_W + LANES].astype(bf16)
    vsl_ref[...] = (_dot_nt(rows_v[0], xs) + ones_row).astype(bf16)
    vw_ref[...] = (_dot_nt(rows_v[1], xs) + ones_row).astype(bf16)
    for c_i in range(SEG_C // LANES):
        z = p_ref[:, OFF_C + c_i * LANES:OFF_C + (c_i + 1) * LANES]
        zs = (z * (1.0 / (1.0 + jnp.exp(-z)))).astype(bf16)
        for half in range(2):
            zs_ref[2 * c_i + half] = _dot(zs, to64[half]).astype(bf16)


def prep(proj, cos_t, sin_t, gain_row, bf_row, ts=256):
    S = proj.shape[0]
    head64 = lambda n: jax.ShapeDtypeStruct((n, S, HEAD_DIM), bf16)
    head128 = lambda n: jax.ShapeDtypeStruct((n, S, AUG), bf16)
    spec_h = lambda n, w: pl.BlockSpec((n, ts, w), lambda i: (0, i, 0))
    spec_r = lambda w: pl.BlockSpec((ts, w), lambda i: (i, 0))
    head_t = lambda n, r: jax.ShapeDtypeStruct((n, r, S), bf16)
    spec_ht = lambda n, r: pl.BlockSpec((n, r, ts), lambda i: (0, 0, i))
    spec_t = pl.BlockSpec((V_ROWS, ts), lambda i: (0, i))
    out_shape = (head_t(MOBA_HEADS, HEAD_DIM), head128(MOBA_HEADS),
                 jax.ShapeDtypeStruct((HEAD_DIM, NSA_HEADS * S), bf16),
                 jax.ShapeDtypeStruct((S, AUG), bf16), jax.ShapeDtypeStruct((S, AUG), bf16),
                 head_t(FOX_HEADS, AUG), head128(FOX_HEADS),
                 head_t(MOBA_HEADS, V_ROWS), head_t(FOX_HEADS, V_ROWS),
                 jax.ShapeDtypeStruct((V_ROWS, S), bf16), jax.ShapeDtypeStruct((V_ROWS, S), bf16),
                 head64(N_ZHEADS))
    out_specs = (spec_ht(MOBA_HEADS, HEAD_DIM), spec_h(MOBA_HEADS, AUG),
                 pl.BlockSpec((HEAD_DIM, NSA_HEADS * ts), lambda i: (0, i)),
                 spec_r(AUG), spec_r(AUG),
                 spec_ht(FOX_HEADS, AUG), spec_h(FOX_HEADS, AUG),
                 spec_ht(MOBA_HEADS, V_ROWS), spec_ht(FOX_HEADS, V_ROWS),
                 spec_t, spec_t,
                 spec_h(N_ZHEADS, HEAD_DIM))
    return pl.pallas_call(
        functools.partial(_prep_kernel, ts=ts),
        grid=(S // ts,),
        in_specs=[pl.BlockSpec((ts, MAIN_COLS), lambda i: (i, 0)),
                  spec_r(LANES), spec_r(LANES),
                  pl.BlockSpec((1, SEG_A), lambda i: (0, 0)),
                  pl.BlockSpec((1, LANES), lambda i: (0, 0))],
        out_specs=out_specs,
        out_shape=out_shape,
        scratch_shapes=[pltpu.VMEM((8, LANES), f32)],
        compiler_params=_cparams(("arbitrary",)),
        name="prep",
    )(proj, cos_t, sin_t, gain_row, bf_row)


def _kmean_kernel(k_ref, o_ref, *, rows):
    n = rows // MOBA_BLOCK
    avg = jnp.where(_div(_iota2((n, rows), 1), MOBA_BLOCK) == _iota2((n, rows), 0),
                    1.0 / MOBA_BLOCK, 0.0).astype(bf16)
    o_ref[0] = _dot(avg, k_ref[0])[:, :HEAD_DIM]


def moba_kmean(mk_aug):
    H, S, _ = mk_aug.shape
    rows = 8 * MOBA_BLOCK
    return pl.pallas_call(
        functools.partial(_kmean_kernel, rows=rows),
        grid=(H, S // rows),
        in_specs=[pl.BlockSpec((1, rows, AUG), lambda h, i: (h, i, 0))],
        out_specs=pl.BlockSpec((1, 8, HEAD_DIM), lambda h, i: (h, i, 0)),
        out_shape=jax.ShapeDtypeStruct((H, CODE_BLOCKS, HEAD_DIM), f32),
        compiler_params=_cparams(("arbitrary", "arbitrary")),
        name="moba_kmean",
    )(mk_aug)


def _top_select(score, idx, n_pick, floor):
    big = jnp.int32(2 ** 30)
    sel = jnp.zeros(score.shape, f32)
    for _ in range(n_pick):
        m = jnp.max(score, axis=0, keepdims=True)
        first = jnp.min(jnp.where(score == m, idx, big), axis=0, keepdims=True)
        pick = (idx == first) & (m > floor)
        sel = jnp.where(pick, 1.0, sel)
        score = jnp.where(pick, -jnp.inf, score)
    return sel


def _moba_select_kernel(q_ref, km_ref, o_ref, *, tq):
    i = pl.program_id(1)
    q = q_ref[0]
    km = km_ref[0]
    km_hi = km.astype(bf16)
    km_lo = (km - km_hi.astype(f32)).astype(bf16)
    gate = _dot(km_hi, q) + _dot(km_lo, q)
    blk = _iota2((CODE_BLOCKS, tq), 0)
    cur = _div(_iota2((CODE_BLOCKS, tq), 1) + i * tq, MOBA_BLOCK)
    score = jnp.where(blk < cur, gate, NEG_INF)
    sel = _top_select(score, blk, MOBA_TOPK, NEG_INF)
    bias = jnp.where((sel > 0.0) | (blk == cur), 0.0, NEG_INF).astype(bf16)
    o_ref[0, 0, 0:HEAD_DIM, :] = q
    o_ref[0, 0, HEAD_DIM:AUG, :] = bias


def moba_select(mq, kmean, tq=512):
    H, _, S = mq.shape
    return pl.pallas_call(
        functools.partial(_moba_select_kernel, tq=tq),
        grid=(H, S // tq),
        in_specs=[pl.BlockSpec((1, HEAD_DIM, tq), lambda h, i: (h, 0, i)),
                  pl.BlockSpec((1, CODE_BLOCKS, HEAD_DIM), lambda h, i: (h, 0, 0))],
        out_specs=pl.BlockSpec((1, 1, AUG, tq), lambda h, i: (h, 0, 0, i)),
        out_shape=jax.ShapeDtypeStruct((H, 1, AUG, S), bf16),
        compiler_params=_cparams(("arbitrary", "arbitrary")),
        name="moba_select",
    )(mq, kmean)


def _cmp_kernel(x_ref, pe_ref, w1_ref, w2_ref, gain_ref, cos_ref, sin_ref, kc_ref, vc_ref, *, ncp):
    top = jnp.zeros((ncp, 2 * NSA_CMP_HIDDEN), f32)
    nxt = jnp.zeros((ncp, 2 * NSA_CMP_HIDDEN), f32)
    for j in range(NSA_CMP_STRIDE):
        xj = x_ref[pl.ds(j, ncp, stride=NSA_CMP_STRIDE), :]
        top = top + _dot((xj + pe_ref[j:j + 1, :]).astype(bf16), w1_ref[j])
        nxt = nxt + _dot((xj + pe_ref[NSA_CMP_STRIDE + j:NSA_CMP_STRIDE + j + 1, :]).astype(bf16),
                         w1_ref[NSA_CMP_STRIDE + j])
    hid = top + pltpu.roll(nxt, ncp - 1, 0)
    act = (hid * (1.0 / (1.0 + jnp.exp(-hid)))).astype(bf16)
    k = _dot(act[:, :NSA_CMP_HIDDEN], w2_ref[0])
    v = _dot(act[:, NSA_CMP_HIDDEN:], w2_ref[1])
    ms = jnp.sum(k * k, axis=-1, keepdims=True) * (1.0 / HEAD_DIM)
    y = k * lax.rsqrt(ms + RMS_EPS) * gain_ref[...]
    lane = _iota2((ncp, LANES), 1)
    up = pltpu.roll(y, LANES - ROPE_DIM // 2, 1)
    dn = pltpu.roll(y, ROPE_DIM // 2, 1)
    y = y * cos_ref[...] + jnp.where(lane < ROPE_DIM // 2, up, dn) * sin_ref[...]
    kc_ref[...] = y[:, :HEAD_DIM].astype(bf16)
    vc_ref[...] = _dot_nt(_place_t(V_ROWS, LANES, 0, 0), v.astype(bf16)).astype(bf16)


def nsa_compress(proj, pe, w1, w2, gain, cos_c, sin_c):
    S = proj.shape[0]
    ncp = S // NSA_CMP_STRIDE
    full = lambda a: pl.BlockSpec(a.shape, lambda i: (0,) * a.ndim)
    return pl.pallas_call(
        functools.partial(_cmp_kernel, ncp=ncp),
        grid=(1,),
        in_specs=[pl.BlockSpec((S, LANES), lambda i: (0, KV_CMP_OFF // LANES)),
                  full(pe), full(w1), full(w2), full(gain), full(cos_c), full(sin_c)],
        out_specs=(pl.BlockSpec((ncp, HEAD_DIM), lambda i: (0, 0)), pl.BlockSpec((V_ROWS, ncp), lambda i: (0, 0))),
        out_shape=(jax.ShapeDtypeStruct((ncp, HEAD_DIM), bf16), jax.ShapeDtypeStruct((V_ROWS, ncp), bf16)),
        compiler_params=_cparams(("arbitrary",)),
        name="nsa_compress",
    )(proj, pe, w1, w2, gain, cos_c, sin_c)


def _nsa_cmp_select_kernel(q_ref, kc_ref, vc_ref, oc_ref, qa_ref, *, ncp, nsp, n_win, n_qb):
    i = pl.program_id(0)
    cols = NSA_HEADS * Q_BLOCK
    kc = kc_ref[...]
    vc = vc_ref[...]
    c0 = _iota2((nsp, ncp), 1) * NSA_CMP_STRIDE
    b0 = _iota2((nsp, ncp), 0) * NSA_SLC_BLOCK
    overlap = jnp.where((c0 <= b0 + (NSA_SLC_BLOCK - 1)) & (c0 + (NSA_CMP_LEN - 1) >= b0), 1.0, 0.0).astype(bf16)
    kend = _iota2((ncp, cols), 0) * NSA_CMP_STRIDE + (NSA_CMP_LEN - 1)
    col_pos = _mod(_iota2((ncp, cols), 1), Q_BLOCK)
    blk = _iota2((nsp, Q_BLOCK), 0)
    pad_rows = jnp.zeros((LANES - V_ROWS, cols), f32)

    def scores(b):
        return _dot(kc, q_ref[:, b * cols:(b + 1) * cols])

    def probs(b, s):
        ok = kend <= (i * n_qb + b) * Q_BLOCK + col_pos
        sm = jnp.where(ok, s, NEG_INF)
        e = jnp.exp2(sm - jnp.max(sm, axis=0, keepdims=True))
        return jnp.where(ok, e * (1.0 / jnp.sum(e, axis=0, keepdims=True)), 0.0)

    def select(b, p):
        o = _dot(vc, p.astype(bf16))
        oc_ref[b * cols:(b + 1) * cols, :] = jnp.concatenate([o, pad_rows], axis=0).T[:, :HEAD_DIM].astype(bf16)
        psum = (p[:, 0:Q_BLOCK] + p[:, Q_BLOCK:2 * Q_BLOCK]
                + p[:, 2 * Q_BLOCK:3 * Q_BLOCK] + p[:, 3 * Q_BLOCK:4 * Q_BLOCK])
        ph, pm, plo = _split3(psum)
        imp = _dot(overlap, ph) + _dot(overlap, pm) + _dot(overlap, plo)
        cur = _div(_iota2((nsp, Q_BLOCK), 1) + (i * n_qb + b) * Q_BLOCK, NSA_SLC_BLOCK)
        score = jnp.where(blk <= cur, imp, NEG_INF)
        forced = (blk == 0) | (blk == cur) | (blk == cur - 1)
        score = jnp.where(forced, POS_BIG, score)
        sel = _top_select(score, blk, NSA_SLC_TOPN, NEG_INF)
        bias = jnp.where(sel > 0.0, 0.0, NEG_INF).astype(bf16)
        for w in range(n_win):
            bw = bias[w * CODE_BLOCKS:(w + 1) * CODE_BLOCKS, :]
            qa_ref[0, w, 0:HEAD_DIM, b * cols:(b + 1) * cols] = q_ref[:, b * cols:(b + 1) * cols]
            qa_ref[0, w, HEAD_DIM:AUG, b * cols:(b + 1) * cols] = jnp.concatenate([bw] * NSA_HEADS, axis=1)

    s_next = scores(0)
    for b in range(n_qb):
        s = s_next
        if b + 1 < n_qb:
            s_next = scores(b + 1)
        select(b, probs(b, s))


def nsa_cmp_select(nq, kc, vc, S, n_qb=2):
    ncp = kc.shape[0]
    ns = S // NSA_SLC_BLOCK
    nsp = max(LANES, ns)
    n_win = max(1, ns // CODE_BLOCKS)
    cols = n_qb * NSA_HEADS * Q_BLOCK
    return pl.pallas_call(
        functools.partial(_nsa_cmp_select_kernel, ncp=ncp, nsp=nsp, n_win=n_win, n_qb=n_qb),
        grid=(S // (n_qb * Q_BLOCK),),
        in_specs=[pl.BlockSpec((HEAD_DIM, cols), lambda i: (0, i)),
                  pl.BlockSpec((ncp, HEAD_DIM), lambda i: (0, 0)),
                  pl.BlockSpec((V_ROWS, ncp), lambda i: (0, 0))],
        out_specs=(pl.BlockSpec((cols, HEAD_DIM), lambda i: (i, 0)),
                   pl.BlockSpec((1, n_win, AUG, cols), lambda i: (0, 0, 0, i))),
        out_shape=(jax.ShapeDtypeStruct((NSA_HEADS * S, HEAD_DIM), bf16),
                   jax.ShapeDtypeStruct((1, n_win, AUG, NSA_HEADS * S), bf16)),
        compiler_params=_cparams(("arbitrary",)),
        name="nsa_cmp_select",
    )(nq, kc, vc)


def _flash_kernel(qi_t, ki_t, win_t, first_t, last_t, mask_t, q_ref, k_ref, v_ref, o_ref, m_sc, acc_sc,
                  *, n_heads, tq, tk, qw, tq_pos, rep_cols, pos_block, window):
    p_id = pl.program_id(0)
    qi = qi_t[p_id]
    ki = ki_t[p_id]

    @pl.when(first_t[p_id] == 1)
    def _():
        m_sc[...] = jnp.full(m_sc.shape, M_FLOOR, f32)
        acc_sc[...] = jnp.zeros_like(acc_sc)

    def tile(masked):
        chains = [(g, slice(c * qw, (c + 1) * qw)) for g in range(n_heads) for c in range(tq // qw)]
        if masked:
            rel = _mod(_iota2((tk, qw), 1), pos_block) - _iota2((tk, qw), 0)

        def scores(t):
            g, cols = chains[t]
            s = _dot(k_ref[g], q_ref[g, 0, :, cols])
            if not masked:
                return s
            base = qi * tq_pos + (cols.start // rep_cols) * pos_block + cols.start % pos_block - ki * tk
            delta = rel + base
            ok = (delta >= 0) if window is None else (delta.astype(jnp.uint32) < jnp.uint32(window))
            return jnp.where(ok, s, NEG_INF)

        def softmax(t, s):
            g, cols = chains[t]
            m_prev = m_sc[g, 0:1, cols]
            m_new = jnp.maximum(m_prev, jnp.max(s, axis=0, keepdims=True))
            m_sc[g, :, cols] = jnp.broadcast_to(m_new, (m_sc.shape[1], qw))
            return jnp.exp2(s - m_new).astype(bf16), jnp.exp2(m_prev - m_new)

        def accumulate(t, p, alpha):
            g, cols = chains[t]
            acc_sc[g, :, cols] = alpha * acc_sc[g, :, cols] + _dot(v_ref[g], p)

        n = len(chains)
        s_next, staged = scores(0), None
        for t in range(n + 1):
            s = s_next
            if t + 1 < n:
                s_next = scores(t + 1)
            if staged is not None:
                accumulate(t - 1, *staged)
            staged = softmax(t, s) if t < n else None

    @pl.when(mask_t[p_id] == 1)
    def _():
        tile(True)

    @pl.when(mask_t[p_id] == 0)
    def _():
        tile(False)

    @pl.when(last_t[p_id] == 1)
    def _():
        for g in range(n_heads):
            acc = acc_sc[g]
            l = acc[HEAD_DIM:HEAD_DIM + 1, :]
            inv = jnp.where(l > 0.0, 1.0 / l, 0.0)
            o = jnp.concatenate([acc * inv, jnp.zeros((LANES - V_ROWS, tq), f32)], axis=0)
            o_ref[g] = o.T[:, :HEAD_DIM].astype(o_ref.dtype)


def _pair_tables(n_q, k_range, tk, win_keys, needs_mask):
    qi, ki, wi, first, last, mask = [], [], [], [], [], []
    for a in range(n_q):
        lo, hi = k_range(a)
        for b in range(lo, hi + 1):
            qi.append(a)
            ki.append(b)
            wi.append((b * tk) // win_keys)
            first.append(int(b == lo))
            last.append(int(b == hi))
            mask.append(int(needs_mask(a, b)))
    return tuple(jnp.asarray(np.asarray(t, np.int32)) for t in (qi, ki, wi, first, last, mask))


def flash(q, k, v, tables, *, tq, tk, pos_block, window, name):
    G, _, _, Sq = q.shape
    tq_pos = 512
    rep_cols = (tq // tq_pos) * pos_block
    assert rep_cols % FLASH_QW == 0 and (pos_block % FLASH_QW == 0 or FLASH_QW % pos_block == 0)
    n_pairs = tables[0].shape[0]
    grid_spec = pltpu.PrefetchScalarGridSpec(
        num_scalar_prefetch=6,
        grid=(n_pairs,),
        in_specs=[pl.BlockSpec((G, 1, AUG, tq), lambda p, qi, ki, wi, fi, la, ma: (0, wi[p], 0, qi[p])),
                  pl.BlockSpec((G, tk, AUG), lambda p, qi, ki, wi, fi, la, ma: (0, ki[p], 0)),
                  pl.BlockSpec((G, V_ROWS, tk), lambda p, qi, ki, wi, fi, la, ma: (0, 0, ki[p]))],
        out_specs=pl.BlockSpec((G, tq, HEAD_DIM), lambda p, qi, ki, wi, fi, la, ma: (0, qi[p], 0)),
        scratch_shapes=[pltpu.VMEM((G, 8, tq), f32), pltpu.VMEM((G, V_ROWS, tq), f32)],
    )
    return pl.pallas_call(
        functools.partial(_flash_kernel, n_heads=G, tq=tq, tk=tk, qw=FLASH_QW, tq_pos=tq_pos, rep_cols=rep_cols,
                          pos_block=pos_block, window=window),
        grid_spec=grid_spec,
        out_shape=jax.ShapeDtypeStruct((G, Sq, HEAD_DIM), bf16),
        compiler_params=_cparams(("arbitrary",)),
        name=name,
    )(*tables, q, k, v)


def _out_kernel(x_ref, gl0_ref, gl1_ref, gl2_ref, sd_ref, bg_ref, om_ref, of_ref, oc_ref, os_ref, ow_ref, zs_ref,
                wm_ref, wn_ref, wf_ref, wo_ref, o_ref, *, tm):
    def up(o_of_head, n_heads, z_off, w_ref):
        y = jnp.zeros((tm, D_MODEL), f32)
        for h in range(n_heads):
            g = (o_of_head(h) * zs_ref[z_off + h].astype(f32)).astype(bf16)
            y = y + _dot(g, w_ref[h])
        return y

    sg = 1.0 / (1.0 + jnp.exp(-sd_ref[...]))

    def nsa_head(h):
        def rows(ref):
            starts = [(qb * NSA_HEADS + h) * Q_BLOCK for qb in range(tm // Q_BLOCK)]
            return jnp.concatenate([ref[r0:r0 + Q_BLOCK, :] for r0 in starts], axis=0).astype(f32)

        return (sg[:, 3 * h:3 * h + 1] * rows(oc_ref) + sg[:, 3 * h + 1:3 * h + 2] * rows(os_ref)
                + sg[:, 3 * h + 2:3 * h + 3] * rows(ow_ref))

    y_m = up(lambda h: om_ref[h].astype(f32), MOBA_HEADS, 0, wm_ref)
    y_n = up(nsa_head, NSA_HEADS, MOBA_HEADS, wn_ref)
    y_f = up(lambda h: of_ref[h].astype(f32), FOX_HEADS, MOBA_HEADS + NSA_HEADS, wf_ref)
    merged = jnp.zeros((tm, D_MODEL), f32)
    for b, (y, gl_ref) in enumerate(((y_m, gl0_ref), (y_n, gl1_ref), (y_f, gl2_ref))):
        cols = slice(b * D_MODEL, (b + 1) * D_MODEL)
        merged = merged + y * (1.0 / (1.0 + jnp.exp(-(gl_ref[...] + bg_ref[:, cols]))))
    o_ref[...] = x_ref[...] + _dot(merged.astype(bf16), wo_ref[...])


def out_proj(x, proj, b_gate, o_m, o_f, o_c, o_s, o_w, zs, w_m, w_n, w_f, w_o):
    S = x.shape[0]
    tm = 2 * Q_BLOCK
    rows = NSA_HEADS * tm
    full = lambda a: pl.BlockSpec(a.shape, lambda i: (0,) * a.ndim)
    gl_spec = lambda b: pl.BlockSpec((tm, D_MODEL), lambda i: (i, MAIN_COLS // D_MODEL + b))
    return pl.pallas_call(
        functools.partial(_out_kernel, tm=tm),
        grid=(S // tm,),
        in_specs=[pl.BlockSpec((tm, D_MODEL), lambda i: (i, 0)),
                  gl_spec(0), gl_spec(1), gl_spec(2),
                  pl.BlockSpec((tm, LANES), lambda i: (i, OFF_D // LANES)),
                  full(b_gate),
                  pl.BlockSpec((MOBA_HEADS, tm, HEAD_DIM), lambda i: (0, i, 0)),
                  pl.BlockSpec((FOX_HEADS, tm, HEAD_DIM), lambda i: (0, i, 0)),
                  pl.BlockSpec((rows, HEAD_DIM), lambda i: (i, 0)),
                  pl.BlockSpec((rows, HEAD_DIM), lambda i: (i, 0)),
                  pl.BlockSpec((rows, HEAD_DIM), lambda i: (i, 0)),
                  pl.BlockSpec((N_ZHEADS, tm, HEAD_DIM), lambda i: (0, i, 0)),
                  full(w_m), full(w_n), full(w_f), full(w_o)],
        out_specs=pl.BlockSpec((tm, D_MODEL), lambda i: (i, 0)),
        out_shape=jax.ShapeDtypeStruct((S, D_MODEL), f32),
        compiler_params=_cparams(("arbitrary",)),
        name="out_proj",
    )(x, proj, proj, proj, proj, b_gate, o_m, o_f, o_c, o_s, o_w, zs, w_m, w_n, w_f, w_o)


def _rope_tables(pos, rows):
    inv = ROPE_THETA ** (-jnp.arange(0, ROPE_DIM, 2, dtype=f32) / ROPE_DIM)
    ang = pos.astype(f32)[:, None] * inv[None, :]
    cos, sin = jnp.cos(ang), jnp.sin(ang)
    n = pos.shape[0]
    rest = HEAD_DIM - ROPE_DIM
    cos_h = jnp.concatenate([cos, cos, jnp.ones((n, rest), f32)], axis=1)
    sin_h = jnp.concatenate([-sin, sin, jnp.zeros((n, rest), f32)], axis=1)
    cos_t = jnp.concatenate([cos_h, cos_h], axis=1)
    sin_t = jnp.concatenate([sin_h, sin_h], axis=1)
    pad = ((0, rows - n), (0, 0))
    return jnp.pad(cos_t, pad), jnp.pad(sin_t, pad)


def _layer(x, norm_g, w_main, b_f, b_gate, moba_qk_g, nsa_q_g, nsa_k_g, fox_qk_g,
           cmp_pe, cmp_w1, cmp_w2, w_up_moba, w_up_nsa, w_up_fox, w_out, tables, rope):
    S = x.shape[0]
    proj = rms_matmul(x, norm_g.reshape(1, D_MODEL), w_main)

    gain_row = jnp.concatenate([jnp.tile(moba_qk_g[0], MOBA_HEADS), jnp.tile(moba_qk_g[1], MOBA_HEADS),
                                jnp.tile(nsa_q_g, NSA_HEADS), nsa_k_g[1], nsa_k_g[2],
                                jnp.tile(fox_qk_g[0], FOX_HEADS), jnp.tile(fox_qk_g[1], FOX_HEADS)]).reshape(1, SEG_A)
    bf_row = jnp.zeros((1, LANES), f32).at[0, FF_LANE:FF_LANE + FOX_HEADS].set(b_f)
    (mq, mk, nq, ksl, kw, fq, fk, mv, fv, vsl, vw, zs) = prep(proj, rope[0], rope[1], gain_row, bf_row)

    q_moba = moba_select(mq, moba_kmean(mk))
    o_m = flash(q_moba, mk, mv, tables["causal"], tq=512, tk=512, pos_block=512, window=None, name="flash_moba")

    pe = jnp.concatenate([cmp_pe[0], cmp_pe[1]], axis=1)
    w1 = cmp_w1.astype(bf16).reshape(2, NSA_CMP_LEN, HEAD_DIM, NSA_CMP_HIDDEN)
    zero = jnp.zeros_like(w1[0])
    w1 = jnp.concatenate([jnp.concatenate([w1[0], zero], axis=2),
                          jnp.concatenate([zero, w1[1]], axis=2)], axis=1)
    w2p = jnp.pad(cmp_w2, ((0, 0), (0, 0), (0, LANES - HEAD_DIM))).astype(bf16)
    gain_c = jnp.pad(nsa_k_g[0], (0, LANES - HEAD_DIM)).reshape(1, LANES)
    kc, vc = nsa_compress(proj, pe, w1, w2p, gain_c, rope[2], rope[3])
    o_c, q_slc = nsa_cmp_select(nq, kc, vc, S)
    nsa_cols = NSA_HEADS * 512
    o_s = flash(q_slc, ksl[None], vsl[None], tables["slc"], tq=nsa_cols, tk=512,
                pos_block=Q_BLOCK, window=None, name="flash_nsa_slc")[0]
    o_w = flash(q_slc, kw[None], vw[None], tables["win"], tq=nsa_cols, tk=512,
                pos_block=Q_BLOCK, window=NSA_WINDOW, name="flash_nsa_win")[0]

    o_f = flash(fq[:, None], fk, fv, tables["causal"], tq=512, tk=512, pos_block=512, window=None, name="flash_fox")
    return out_proj(x, proj, b_gate.reshape(1, N_BRANCH * D_MODEL), o_m, o_f, o_c, o_s, o_w, zs,
                    w_up_moba.astype(bf16).reshape(MOBA_HEADS, HEAD_DIM, D_MODEL),
                    w_up_nsa.astype(bf16).reshape(NSA_HEADS, HEAD_DIM, D_MODEL),
                    w_up_fox.astype(bf16).reshape(FOX_HEADS, HEAD_DIM, D_MODEL),
                    w_out.astype(bf16))


def kernel(x, norm_g, w_in, b_f, b_gate, moba_qk_g, nsa_q_g, nsa_k_g, fox_qk_g, cmp_pe, cmp_w1, cmp_w2,
           w_up_moba, w_up_nsa, w_up_fox, w_out):
    B, S, _ = x.shape
    assert B == 1 and S % 2048 == 0 and S // MOBA_BLOCK <= CODE_BLOCKS
    depth = norm_g.shape[0]
    tk = 512
    win_keys = CODE_BLOCKS * NSA_SLC_BLOCK
    tables = {
        "causal": _pair_tables(S // 512, lambda a: (0, a), tk, S, lambda a, b: a == b),
        "slc": _pair_tables(S // 512, lambda a: (0, a), tk, win_keys, lambda a, b: a == b),
        "win": _pair_tables(S // 512, lambda a: (max(0, a - 1), a), tk, S, lambda a, b: True),
    }
    ncp = S // NSA_CMP_STRIDE
    cos_t, sin_t = _rope_tables(jnp.arange(S), S)
    cmp_end = jnp.arange(ncp - 1) * NSA_CMP_STRIDE + (NSA_CMP_LEN - 1)
    cos_c, sin_c = _rope_tables(cmp_end, ncp)
    rope = (cos_t, sin_t, cos_c, sin_c)
    h = x[0]
    for l in range(depth):
        h = _layer(h, norm_g[l], repack_w_in(w_in, l), b_f[l], b_gate[l], moba_qk_g[l], nsa_q_g[l], nsa_k_g[l], fox_qk_g[l],
                   cmp_pe[l], cmp_w1[l], cmp_w2[l], w_up_moba[l], w_up_nsa[l], w_up_fox[l], w_out[l], tables, rope)
    return h[None]
```

```python
import functools

import numpy as np
import jax
import jax.numpy as jnp
from jax import lax
from jax.experimental import pallas as pl
from jax.experimental.pallas import tpu as pltpu

D_MODEL = 1024
HEAD_DIM = 64
ROPE_DIM = HEAD_DIM // 4
ROPE_THETA = 500000.0
RMS_EPS = 1e-6
NEG_INF = -1e30
POS_BIG = 1e30
M_FLOOR = -1e20

MOBA_HEADS = 6
MOBA_BLOCK = 256
MOBA_TOPK = 3
NSA_HEADS = 4
NSA_CMP_LEN = 32
NSA_CMP_STRIDE = 16
NSA_CMP_HIDDEN = 4 * HEAD_DIM
NSA_SLC_BLOCK = 64
NSA_SLC_TOPN = 16
NSA_WINDOW = 512
FOX_HEADS = 6
N_BRANCH = 3
MOBA_W = MOBA_HEADS * HEAD_DIM
NSA_W = NSA_HEADS * HEAD_DIM
FOX_W = FOX_HEADS * HEAD_DIM
IN_SPLITS = (MOBA_W,) * 4 + (NSA_W,) + (HEAD_DIM,) * 6 + (3 * NSA_HEADS, NSA_W) + (FOX_W,) * 3 + (FOX_HEADS, FOX_W, N_BRANCH * D_MODEL)
ATTN_SCALE = HEAD_DIM ** -0.5

LANES = 128
AUG = 2 * HEAD_DIM
CODE_BLOCKS = AUG - HEAD_DIM
Q_BLOCK = 128
N_ZHEADS = MOBA_HEADS + NSA_HEADS + FOX_HEADS

SEG_A = 2 * MOBA_W + NSA_W + 2 * HEAD_DIM + 2 * FOX_W
SEG_B = 2 * MOBA_W + 4 * HEAD_DIM
SEG_C = MOBA_W + NSA_W + FOX_W
SEG_D = LANES
OFF_B = SEG_A
OFF_C = OFF_B + SEG_B
OFF_D = OFF_C + SEG_C
MAIN_COLS = OFF_D + SEG_D
FF_LANE = 3 * NSA_HEADS
KV_CMP_OFF = OFF_B + 2 * MOBA_W + 2 * HEAD_DIM
ALL_COLS = MAIN_COLS + N_BRANCH * D_MODEL

VMEM_LIMIT = 56 * 1024 * 1024
FLASH_QW = 512
FLASH_TK = 512
FLASH_SUB = 2
LOG2E = 1.4426950408889634
Q_SCALE = ATTN_SCALE * LOG2E
V_ROWS = 80

f32 = jnp.float32
bf16 = jnp.bfloat16


def _cparams(sem):
    return pltpu.CompilerParams(dimension_semantics=sem, vmem_limit_bytes=VMEM_LIMIT)


def _iota2(shape, dim):
    return lax.broadcasted_iota(jnp.int32, shape, dim)


def _div(x, d):
    return jnp.right_shift(x, int(d).bit_length() - 1)


def _mod(x, d):
    return jnp.bitwise_and(x, d - 1)


def _place(n_in, n_out, in_off, out_off, width=HEAD_DIM, val=1.0):
    r = _iota2((n_in, n_out), 0) - in_off
    c = _iota2((n_in, n_out), 1) - out_off
    hit = (r == c) & (r >= 0) & (r < width)
    return jnp.where(hit, val, 0.0).astype(bf16)


def _place_t(n_out, n_in, in_off, out_off, width=HEAD_DIM, val=1.0):
    r = _iota2((n_out, n_in), 0) - out_off
    c = _iota2((n_out, n_in), 1) - in_off
    hit = (r == c) & (r >= 0) & (r < width)
    return jnp.where(hit, val, 0.0).astype(bf16)


def _split3(x):
    hi = x.astype(bf16)
    r = x - hi.astype(f32)
    mid = r.astype(bf16)
    lo = (r - mid.astype(f32)).astype(bf16)
    return hi, mid, lo


def _dot(a, b):
    return jnp.dot(a, b, preferred_element_type=f32)


def _dot_nt(a, b):
    return lax.dot_general(a, b, (((1,), (1,)), ((), ())), preferred_element_type=f32)


def _w_in_plan():
    offs = np.concatenate([[0], np.cumsum(IN_SPLITS)])
    names = ("mq", "mk", "mv", "mz", "nq", "kc", "vc", "ksl", "vsl", "kw", "vw", "ng", "nz", "fq", "fk", "fv", "ff", "fz", "gl")
    start = {n: int(offs[j]) for j, n in enumerate(names)}
    width = {n: int(IN_SPLITS[j]) for j, n in enumerate(names)}
    order = ("mq", "mk", "nq", "ksl", "kw", "fq", "fk", "mv", "fv", "vsl", "vw", "kc", "vc", "mz", "nz", "fz", "ng", "ff")
    plan = [[] for _ in range(ALL_COLS // LANES)]
    new = 0
    for n in order + ("pad", "gl"):
        if n == "pad":
            new = MAIN_COLS
            continue
        src, left = start[n], width[n]
        while left > 0:
            w = min(left, LANES - new % LANES, LANES - src % LANES)
            plan[new // LANES].append((src // LANES, src % LANES, new % LANES, w))
            src, new, left = src + w, new + w, left - w
    assert new == ALL_COLS
    return plan


def _repack_kernel(w_ref, o_ref, *, plan, tr, n_cols):
    lane = _iota2((tr, LANES), 1)
    loaded = {}

    def source(a):
        if a not in loaded:
            x = w_ref[0, :, a * LANES:(a + 1) * LANES]
            if (a + 1) * LANES > n_cols:
                x = jnp.where(lane < n_cols - a * LANES, x, 0.0)
            loaded[a] = x.astype(bf16)
        return loaded[a]

    for b, pieces in enumerate(plan):
        acc = jnp.zeros((tr, LANES), f32)
        for a, lane_in, lane_out, w in pieces:
            acc = acc + _dot(source(a), _place(LANES, LANES, lane_in, lane_out, width=w))
        o_ref[:, b * LANES:(b + 1) * LANES] = acc.astype(bf16)


def repack_w_in(w_in, layer, tr=128):
    _, D, n_cols = w_in.shape
    padded = pl.cdiv(n_cols, LANES) * LANES
    return pl.pallas_call(
        functools.partial(_repack_kernel, plan=_w_in_plan(), tr=tr, n_cols=n_cols),
        grid=(D // tr,),
        in_specs=[pl.BlockSpec((1, tr, padded), lambda i: (layer, i, 0))],
        out_specs=pl.BlockSpec((tr, ALL_COLS), lambda i: (i, 0)),
        out_shape=jax.ShapeDtypeStruct((D, ALL_COLS), bf16),
        compiler_params=_cparams(("arbitrary",)),
        name="repack_w_in",
    )(w_in)


def _rms_matmul_kernel(x_ref, g_ref, w_ref, o_ref):
    x = x_ref[...]
    ms = jnp.mean(x * x, axis=-1, keepdims=True)
    h = (x * lax.rsqrt(ms + RMS_EPS) * g_ref[...]).astype(bf16)
    o_ref[...] = _dot(h, w_ref[...])


def rms_matmul(x, g, w, tm=1024, tn=1024):
    S, D = x.shape
    N = w.shape[1]
    return pl.pallas_call(
        _rms_matmul_kernel,
        grid=(N // tn, S // tm),
        in_specs=[pl.BlockSpec((tm, D), lambda j, i: (i, 0)),
                  pl.BlockSpec((1, D), lambda j, i: (0, 0)),
                  pl.BlockSpec((D, tn), lambda j, i: (0, j))],
        out_specs=pl.BlockSpec((tm, tn), lambda j, i: (i, j)),
        out_shape=jax.ShapeDtypeStruct((S, N), f32),
        compiler_params=_cparams(("arbitrary", "arbitrary")),
        name="rms_matmul",
    )(x, g, w)


def _prep_kernel(p_ref, cos_ref, sin_ref, gain_ref, bf_ref,
                 mq_ref, mk_ref, nq_ref, ksl_ref, kw_ref, fq_ref, fk_ref,
                 mv_ref, fv_ref, vsl_ref, vw_ref, zs_ref, carry_sc, *, ts):
    i = pl.program_id(0)

    @pl.when(i == 0)
    def _():
        carry_sc[...] = jnp.zeros_like(carry_sc)

    lane = _iota2((ts, LANES), 1)
    pos = _iota2((ts, LANES), 0) + i * ts
    blockdiag = jnp.where(_div(_iota2((LANES, LANES), 0), HEAD_DIM) == _div(_iota2((LANES, LANES), 1), HEAD_DIM),
                          1.0, 0.0).astype(bf16)
    to64 = [_place(LANES, HEAD_DIM, 0, 0), _place(LANES, HEAD_DIM, HEAD_DIM, 0)]
    to128 = [_place(LANES, AUG, 0, 0), _place(LANES, AUG, HEAD_DIM, 0)]
    rows_q = [_place_t(HEAD_DIM, LANES, 0, 0), _place_t(HEAD_DIM, LANES, HEAD_DIM, 0)]
    rows_aug = [_place_t(AUG, LANES, 0, 0), _place_t(AUG, LANES, HEAD_DIM, 0)]
    rows_v = [_place_t(V_ROWS, LANES, 0, 0), _place_t(V_ROWS, LANES, HEAD_DIM, 0)]
    rowi = _iota2((AUG, ts), 0)
    ones_row = jnp.where(_iota2((V_ROWS, ts), 0) == HEAD_DIM, 1.0, 0.0)
    first_half = _mod(lane, HEAD_DIM) < (ROPE_DIM // 2)
    cos = cos_ref[...]
    sin = sin_ref[...]

    def normed(c, rope, scale=None):
        x = p_ref[:, c * LANES:(c + 1) * LANES]
        x2 = x * x
        hi = x2.astype(bf16)
        lo = (x2 - hi.astype(f32)).astype(bf16)
        ss = _dot(hi, blockdiag) + _dot(lo, blockdiag)
        y = x * lax.rsqrt(ss * (1.0 / HEAD_DIM) + RMS_EPS) * gain_ref[:, c * LANES:(c + 1) * LANES]
        if rope:
            up = pltpu.roll(y, LANES - ROPE_DIM // 2, 1)
            dn = pltpu.roll(y, ROPE_DIM // 2, 1)
            y = y * cos + jnp.where(first_half, up, dn) * sin
        if scale is not None:
            y = y * scale
        return y.astype(bf16)

    d = p_ref[:, OFF_D:OFF_D + LANES] + bf_ref[...]
    logf = jnp.minimum(d, 0.0) - jnp.log(1.0 + jnp.exp(-jnp.abs(d)))
    tri = jnp.where(_iota2((ts, ts), 1) <= _iota2((ts, ts), 0), 1.0, 0.0).astype(bf16)
    lh, lm, ll = _split3(logf)
    c = carry_sc[0:1, :] + (_dot(tri, lh) + _dot(tri, lm) + _dot(tri, ll))
    carry_sc[...] = jnp.broadcast_to(c[ts - 1:ts, :], carry_sc.shape)
    ch, cm, cl = _split3(c * LOG2E)
    one_q = jnp.where((rowi >= HEAD_DIM + 3) & (rowi < HEAD_DIM + 6), 1.0, 0.0)
    one_k = jnp.where((lane >= HEAD_DIM) & (lane < HEAD_DIM + 3), 1.0, 0.0)

    def decay_cols(h, base):
        src = FF_LANE + h
        return (_dot(ch, _place(LANES, AUG, src, base, width=1))
                + _dot(cm, _place(LANES, AUG, src, base + 1, width=1))
                + _dot(cl, _place(LANES, AUG, src, base + 2, width=1)))

    def decay_rows(h, base):
        src = FF_LANE + h
        return (_dot_nt(_place_t(AUG, LANES, src, base, width=1), ch)
                + _dot_nt(_place_t(AUG, LANES, src, base + 1, width=1), cm)
                + _dot_nt(_place_t(AUG, LANES, src, base + 2, width=1), cl))

    moba_code = jnp.where((lane >= HEAD_DIM) & ((lane - HEAD_DIM) == _div(pos, MOBA_BLOCK)), 1.0, 0.0)
    for c_i in range(3):
        yq = normed(c_i, True, Q_SCALE)
        yk = normed(3 + c_i, True)
        for half in range(2):
            h = 2 * c_i + half
            mq_ref[h] = _dot_nt(rows_q[half], yq).astype(bf16)
            mk_ref[h] = (_dot(yk, to128[half]) + moba_code).astype(bf16)
    for c_i in range(2):
        y = normed(6 + c_i, True, Q_SCALE)
        for half in range(2):
            h = 2 * c_i + half
            yh = _dot_nt(rows_q[half], y).astype(bf16)
            for qb in range(ts // Q_BLOCK):
                nq_ref[:, (qb * NSA_HEADS + h) * Q_BLOCK:(qb * NSA_HEADS + h + 1) * Q_BLOCK] = (
                    yh[:, qb * Q_BLOCK:(qb + 1) * Q_BLOCK])
    y = normed(8, True)
    slc_code = jnp.where((lane >= HEAD_DIM) & ((lane - HEAD_DIM) == _mod(_div(pos, NSA_SLC_BLOCK), CODE_BLOCKS)), 1.0, 0.0)
    ksl_ref[...] = (_dot(y, to128[0]) + slc_code).astype(bf16)
    kw_ref[...] = _dot(y, to128[1]).astype(bf16)
    for c_i in range(3):
        yq = normed(9 + c_i, False, Q_SCALE)
        yk = normed(12 + c_i, False)
        for half in range(2):
            h = 2 * c_i + half
            fq_ref[h] = (_dot_nt(rows_aug[half], yq) + decay_rows(h, HEAD_DIM) + one_q).astype(bf16)
            fk_ref[h] = (_dot(yk, to128[half]) - decay_cols(h, HEAD_DIM + 3) + one_k).astype(bf16)
    for c_i in range(3):
        xm = p_ref[:, OFF_B + c_i * LANES:OFF_B + (c_i + 1) * LANES].astype(bf16)
        xf = p_ref[:, OFF_B + MOBA_W + c_i * LANES:OFF_B + MOBA_W + (c_i + 1) * LANES].astype(bf16)
        for half in range(2):
            mv_ref[2 * c_i + half] = (_dot_nt(rows_v[half], xm) + ones_row).astype(bf16)
            fv_ref[2 * c_i + half] = (_dot_nt(rows_v[half], xf) + ones_row).astype(bf16)
    xs = p_ref[:, OFF_B + 2 * MOBA_W:OFF_B + 2 * MOBA_W + LANES].astype(bf16)
    vsl_ref[...] = (_dot_nt(rows_v[0], xs) + ones_row).astype(bf16)
    vw_ref[...] = (_dot_nt(rows_v[1], xs) + ones_row).astype(bf16)
    for c_i in range(SEG_C // LANES):
        z = p_ref[:, OFF_C + c_i * LANES:OFF_C + (c_i + 1) * LANES]
        zs = (z * (1.0 / (1.0 + jnp.exp(-z)))).astype(bf16)
        for half in range(2):
            zs_ref[2 * c_i + half] = _dot(zs, to64[half]).astype(bf16)


def prep(proj, cos_t, sin_t, gain_row, bf_row, ts=256):
    S = proj.shape[0]
    head64 = lambda n: jax.ShapeDtypeStruct((n, S, HEAD_DIM), bf16)
    head128 = lambda n: jax.ShapeDtypeStruct((n, S, AUG), bf16)
    spec_h = lambda n, w: pl.BlockSpec((n, ts, w), lambda i: (0, i, 0))
    spec_r = lambda w: pl.BlockSpec((ts, w), lambda i: (i, 0))
    head_t = lambda n, r: jax.ShapeDtypeStruct((n, r, S), bf16)
    spec_ht = lambda n, r: pl.BlockSpec((n, r, ts), lambda i: (0, 0, i))
    spec_t = pl.BlockSpec((V_ROWS, ts), lambda i: (0, i))
    out_shape = (head_t(MOBA_HEADS, HEAD_DIM), head128(MOBA_HEADS),
                 jax.ShapeDtypeStruct((HEAD_DIM, NSA_HEADS * S), bf16),
                 jax.ShapeDtypeStruct((S, AUG), bf16), jax.ShapeDtypeStruct((S, AUG), bf16),
                 head_t(FOX_HEADS, AUG), head128(FOX_HEADS),
                 head_t(MOBA_HEADS, V_ROWS), head_t(FOX_HEADS, V_ROWS),
                 jax.ShapeDtypeStruct((V_ROWS, S), bf16), jax.ShapeDtypeStruct((V_ROWS, S), bf16),
                 head64(N_ZHEADS))
    out_specs = (spec_ht(MOBA_HEADS, HEAD_DIM), spec_h(MOBA_HEADS, AUG),
                 pl.BlockSpec((HEAD_DIM, NSA_HEADS * ts), lambda i: (0, i)),
                 spec_r(AUG), spec_r(AUG),
                 spec_ht(FOX_HEADS, AUG), spec_h(FOX_HEADS, AUG),
                 spec_ht(MOBA_HEADS, V_ROWS), spec_ht(FOX_HEADS, V_ROWS),
                 spec_t, spec_t,
                 spec_h(N_ZHEADS, HEAD_DIM))
    return pl.pallas_call(
        functools.partial(_prep_kernel, ts=ts),
        grid=(S // ts,),
        in_specs=[pl.BlockSpec((ts, MAIN_COLS), lambda i: (i, 0)),
                  spec_r(LANES), spec_r(LANES),
                  pl.BlockSpec((1, SEG_A), lambda i: (0, 0)),
                  pl.BlockSpec((1, LANES), lambda i: (0, 0))],
        out_specs=out_specs,
        out_shape=out_shape,
        scratch_shapes=[pltpu.VMEM((8, LANES), f32)],
        compiler_params=_cparams(("arbitrary",)),
        name="prep",
    )(proj, cos_t, sin_t, gain_row, bf_row)


def _kmean_kernel(k_ref, o_ref, *, rows):
    n = rows // MOBA_BLOCK
    avg = jnp.where(_div(_iota2((n, rows), 1), MOBA_BLOCK) == _iota2((n, rows), 0),
                    1.0 / MOBA_BLOCK, 0.0).astype(bf16)
    o_ref[0] = _dot(avg, k_ref[0])[:, :HEAD_DIM]


def moba_kmean(mk_aug):
    H, S, _ = mk_aug.shape
    rows = 8 * MOBA_BLOCK
    return pl.pallas_call(
        functools.partial(_kmean_kernel, rows=rows),
        grid=(H, S // rows),
        in_specs=[pl.BlockSpec((1, rows, AUG), lambda h, i: (h, i, 0))],
        out_specs=pl.BlockSpec((1, 8, HEAD_DIM), lambda h, i: (h, i, 0)),
        out_shape=jax.ShapeDtypeStruct((H, CODE_BLOCKS, HEAD_DIM), f32),
        compiler_params=_cparams(("arbitrary", "arbitrary")),
        name="moba_kmean",
    )(mk_aug)


def _top_select(score, idx, n_pick, floor):
    big = jnp.int32(2 ** 30)
    sel = jnp.zeros(score.shape, f32)
    for _ in range(n_pick):
        m = jnp.max(score, axis=0, keepdims=True)
        first = jnp.min(jnp.where(score == m, idx, big), axis=0, keepdims=True)
        pick = (idx == first) & (m > floor)
        sel = jnp.where(pick, 1.0, sel)
        score = jnp.where(pick, -jnp.inf, score)
    return sel


def _moba_select_kernel(q_ref, km_ref, o_ref, *, tq):
    i = pl.program_id(1)
    q = q_ref[0]
    km = km_ref[0]
    km_hi = km.astype(bf16)
    km_lo = (km - km_hi.astype(f32)).astype(bf16)
    gate = _dot(km_hi, q) + _dot(km_lo, q)
    blk = _iota2((CODE_BLOCKS, tq), 0)
    cur = _div(_iota2((CODE_BLOCKS, tq), 1) + i * tq, MOBA_BLOCK)
    score = jnp.where(blk < cur, gate, NEG_INF)
    sel = _top_select(score, blk, MOBA_TOPK, NEG_INF)
    bias = jnp.where((sel > 0.0) | (blk == cur), 0.0, NEG_INF).astype(bf16)
    o_ref[0, 0, 0:HEAD_DIM, :] = q
    o_ref[0, 0, HEAD_DIM:AUG, :] = bias


def moba_select(mq, kmean, tq=512):
    H, _, S = mq.shape
    return pl.pallas_call(
        functools.partial(_moba_select_kernel, tq=tq),
        grid=(H, S // tq),
        in_specs=[pl.BlockSpec((1, HEAD_DIM, tq), lambda h, i: (h, 0, i)),
                  pl.BlockSpec((1, CODE_BLOCKS, HEAD_DIM), lambda h, i: (h, 0, 0))],
        out_specs=pl.BlockSpec((1, 1, AUG, tq), lambda h, i: (h, 0, 0, i)),
        out_shape=jax.ShapeDtypeStruct((H, 1, AUG, S), bf16),
        compiler_params=_cparams(("arbitrary", "arbitrary")),
        name="moba_select",
    )(mq, kmean)


def _cmp_kernel(x_ref, pe_ref, w1_ref, w2_ref, gain_ref, cos_ref, sin_ref, kc_ref, vc_ref, *, ncp):
    top = jnp.zeros((ncp, 2 * NSA_CMP_HIDDEN), f32)
    nxt = jnp.zeros((ncp, 2 * NSA_CMP_HIDDEN), f32)
    for j in range(NSA_CMP_STRIDE):
        xj = x_ref[pl.ds(j, ncp, stride=NSA_CMP_STRIDE), :]
        top = top + _dot((xj + pe_ref[j:j + 1, :]).astype(bf16), w1_ref[j])
        nxt = nxt + _dot((xj + pe_ref[NSA_CMP_STRIDE + j:NSA_CMP_STRIDE + j + 1, :]).astype(bf16),
                         w1_ref[NSA_CMP_STRIDE + j])
    hid = top + pltpu.roll(nxt, ncp - 1, 0)
    act = (hid * (1.0 / (1.0 + jnp.exp(-hid)))).astype(bf16)
    k = _dot(act[:, :NSA_CMP_HIDDEN], w2_ref[0])
    v = _dot(act[:, NSA_CMP_HIDDEN:], w2_ref[1])
    ms = jnp.sum(k * k, axis=-1, keepdims=True) * (1.0 / HEAD_DIM)
    y = k * lax.rsqrt(ms + RMS_EPS) * gain_ref[...]
    lane = _iota2((ncp, LANES), 1)
    up = pltpu.roll(y, LANES - ROPE_DIM // 2, 1)
    dn = pltpu.roll(y, ROPE_DIM // 2, 1)
    y = y * cos_ref[...] + jnp.where(lane < ROPE_DIM // 2, up, dn) * sin_ref[...]
    kc_ref[...] = y[:, :HEAD_DIM].astype(bf16)
    vc_ref[...] = _dot_nt(_place_t(V_ROWS, LANES, 0, 0), v.astype(bf16)).astype(bf16)


def nsa_compress(proj, pe, w1, w2, gain, cos_c, sin_c):
    S = proj.shape[0]
    ncp = S // NSA_CMP_STRIDE
    full = lambda a: pl.BlockSpec(a.shape, lambda i: (0,) * a.ndim)
    return pl.pallas_call(
        functools.partial(_cmp_kernel, ncp=ncp),
        grid=(1,),
        in_specs=[pl.BlockSpec((S, LANES), lambda i: (0, KV_CMP_OFF // LANES)),
                  full(pe), full(w1), full(w2), full(gain), full(cos_c), full(sin_c)],
        out_specs=(pl.BlockSpec((ncp, HEAD_DIM), lambda i: (0, 0)), pl.BlockSpec((V_ROWS, ncp), lambda i: (0, 0))),
        out_shape=(jax.ShapeDtypeStruct((ncp, HEAD_DIM), bf16), jax.ShapeDtypeStruct((V_ROWS, ncp), bf16)),
        compiler_params=_cparams(("arbitrary",)),
        name="nsa_compress",
    )(proj, pe, w1, w2, gain, cos_c, sin_c)


def _nsa_cmp_select_kernel(q_ref, kc_ref, vc_ref, oc_ref, qa_ref, *, ncp, nsp, n_win, n_qb):
    i = pl.program_id(0)
    cols = NSA_HEADS * Q_BLOCK
    kc = kc_ref[...]
    vc = vc_ref[...]
    c0 = _iota2((nsp, ncp), 1) * NSA_CMP_STRIDE
    b0 = _iota2((nsp, ncp), 0) * NSA_SLC_BLOCK
    overlap = jnp.where((c0 <= b0 + (NSA_SLC_BLOCK - 1)) & (c0 + (NSA_CMP_LEN - 1) >= b0), 1.0, 0.0).astype(bf16)
    kend = _iota2((ncp, cols), 0) * NSA_CMP_STRIDE + (NSA_CMP_LEN - 1)
    col_pos = _mod(_iota2((ncp, cols), 1), Q_BLOCK)
    blk = _iota2((nsp, Q_BLOCK), 0)
    pad_rows = jnp.zeros((LANES - V_ROWS, cols), f32)

    def scores(b):
        return _dot(kc, q_ref[:, b * cols:(b + 1) * cols])

    def probs(b, s):
        ok = kend <= (i * n_qb + b) * Q_BLOCK + col_pos
        sm = jnp.where(ok, s, NEG_INF)
        e = jnp.exp2(sm - jnp.max(sm, axis=0, keepdims=True))
        return jnp.where(ok, e * (1.0 / jnp.sum(e, axis=0, keepdims=True)), 0.0)

    def select(b, p):
        o = _dot(vc, p.astype(bf16))
        oc_ref[b * cols:(b + 1) * cols, :] = jnp.concatenate([o, pad_rows], axis=0).T[:, :HEAD_DIM].astype(bf16)
        psum = (p[:, 0:Q_BLOCK] + p[:, Q_BLOCK:2 * Q_BLOCK]
                + p[:, 2 * Q_BLOCK:3 * Q_BLOCK] + p[:, 3 * Q_BLOCK:4 * Q_BLOCK])
        ph, pm, plo = _split3(psum)
        imp = _dot(overlap, ph) + _dot(overlap, pm) + _dot(overlap, plo)
        cur = _div(_iota2((nsp, Q_BLOCK), 1) + (i * n_qb + b) * Q_BLOCK, NSA_SLC_BLOCK)
        score = jnp.where(blk <= cur, imp, NEG_INF)
        forced = (blk == 0) | (blk == cur) | (blk == cur - 1)
        score = jnp.where(forced, POS_BIG, score)
        sel = _top_select(score, blk, NSA_SLC_TOPN, NEG_INF)
        bias = jnp.where(sel > 0.0, 0.0, NEG_INF).astype(bf16)
        for w in range(n_win):
            bw = bias[w * CODE_BLOCKS:(w + 1) * CODE_BLOCKS, :]
            qa_ref[0, w, 0:HEAD_DIM, b * cols:(b + 1) * cols] = q_ref[:, b * cols:(b + 1) * cols]
            qa_ref[0, w, HEAD_DIM:AUG, b * cols:(b + 1) * cols] = jnp.concatenate([bw] * NSA_HEADS, axis=1)

    s_next = scores(0)
    for b in range(n_qb):
        s = s_next
        if b + 1 < n_qb:
            s_next = scores(b + 1)
        select(b, probs(b, s))


def nsa_cmp_select(nq, kc, vc, S, n_qb=2):
    ncp = kc.shape[0]
    ns = S // NSA_SLC_BLOCK
    nsp = max(LANES, ns)
    n_win = max(1, ns // CODE_BLOCKS)
    cols = n_qb * NSA_HEADS * Q_BLOCK
    return pl.pallas_call(
        functools.partial(_nsa_cmp_select_kernel, ncp=ncp, nsp=nsp, n_win=n_win, n_qb=n_qb),
        grid=(S // (n_qb * Q_BLOCK),),
        in_specs=[pl.BlockSpec((HEAD_DIM, cols), lambda i: (0, i)),
                  pl.BlockSpec((ncp, HEAD_DIM), lambda i: (0, 0)),
                  pl.BlockSpec((V_ROWS, ncp), lambda i: (0, 0))],
        out_specs=(pl.BlockSpec((cols, HEAD_DIM), lambda i: (i, 0)),
                   pl.BlockSpec((1, n_win, AUG, cols), lambda i: (0, 0, 0, i))),
        out_shape=(jax.ShapeDtypeStruct((NSA_HEADS * S, HEAD_DIM), bf16),
                   jax.ShapeDtypeStruct((1, n_win, AUG, NSA_HEADS * S), bf16)),
        compiler_params=_cparams(("arbitrary",)),
        name="nsa_cmp_select",
    )(nq, kc, vc)


def _flash_kernel(qi_t, kb_t, win_t, first_t, last_t, var_t, q_ref, k_ref, v_ref, o_ref, m_sc, acc_sc,
                  *, n_heads, tq, qw, tq_pos, rep_cols, pos_block, window, variants):
    p_id = pl.program_id(0)
    qi = qi_t[p_id]
    kb = kb_t[p_id]

    @pl.when(first_t[p_id] == 1)
    def _():
        m_sc[...] = jnp.full(m_sc.shape, M_FLOOR, f32)
        acc_sc[...] = jnp.zeros_like(acc_sc)

    def tile(modes):
        chains = [(g, slice(c * qw, (c + 1) * qw), slice(kt * FLASH_TK, (kt + 1) * FLASH_TK))
                  for kt in range(FLASH_SUB) if modes[kt] for g in range(n_heads) for c in range(tq // qw)]
        if 2 in modes:
            rel = _mod(_iota2((FLASH_TK, qw), 1), pos_block) - _iota2((FLASH_TK, qw), 0)

        def scores(t):
            g, cols, keys = chains[t]
            s = _dot(k_ref[g, keys, :], q_ref[g, 0, :, cols])
            if modes[keys.start // FLASH_TK] == 1:
                return s
            base = (qi * tq_pos + (cols.start // rep_cols) * pos_block + cols.start % pos_block
                    - kb * (FLASH_SUB * FLASH_TK) - keys.start)
            delta = rel + base
            ok = (delta >= 0) if window is None else (delta.astype(jnp.uint32) < jnp.uint32(window))
            return jnp.where(ok, s, NEG_INF)

        def softmax(t, s):
            g, cols, _ = chains[t]
            m_prev = m_sc[g, 0:1, cols]
            m_new = jnp.maximum(m_prev, jnp.max(s, axis=0, keepdims=True))
            m_sc[g, :, cols] = jnp.broadcast_to(m_new, (m_sc.shape[1], qw))
            return jnp.exp2(s - m_new).astype(bf16), jnp.exp2(m_prev - m_new)

        def accumulate(t, p, alpha):
            g, cols, keys = chains[t]
            acc_sc[g, :, cols] = alpha * acc_sc[g, :, cols] + _dot(v_ref[g, :, keys], p)

        n = len(chains)
        s_next, staged = scores(0), None
        for t in range(n + 1):
            s = s_next
            if t + 1 < n:
                s_next = scores(t + 1)
            if staged is not None:
                accumulate(t - 1, *staged)
            staged = softmax(t, s) if t < n else None

    for vid, modes in enumerate(variants):
        pl.when(var_t[p_id] == vid)(functools.partial(tile, modes))

    @pl.when(last_t[p_id] == 1)
    def _():
        for g in range(n_heads):
            acc = acc_sc[g]
            l = acc[HEAD_DIM:HEAD_DIM + 1, :]
            inv = jnp.where(l > 0.0, 1.0 / l, 0.0)
            o = jnp.concatenate([acc * inv, jnp.zeros((LANES - V_ROWS, tq), f32)], axis=0)
            o_ref[g] = o.T[:, :HEAD_DIM].astype(o_ref.dtype)


def _step_tables(n_q, key_tiles, win_keys):
    variants, cols = [], ([], [], [], [], [], [])
    for a in range(n_q):
        tiles = key_tiles(a)
        groups = sorted({t // FLASH_SUB for t in tiles})
        for kb in groups:
            modes = tuple(tiles.get(kb * FLASH_SUB + j, 0) for j in range(FLASH_SUB))
            if modes not in variants:
                variants.append(modes)
            row = (a, kb, (kb * FLASH_SUB * FLASH_TK) // win_keys, int(kb == groups[0]), int(kb == groups[-1]),
                   variants.index(modes))
            for c, v in zip(cols, row):
                c.append(v)
    return tuple(jnp.asarray(np.asarray(c, np.int32)) for c in cols), tuple(variants)


def flash(q, k, v, steps, *, tq, pos_block, window, name):
    tables, variants = steps
    G, _, _, Sq = q.shape
    tq_pos = FLASH_TK
    rep_cols = (tq // tq_pos) * pos_block
    assert rep_cols % FLASH_QW == 0 and (pos_block % FLASH_QW == 0 or FLASH_QW % pos_block == 0)
    tk = FLASH_SUB * FLASH_TK
    grid_spec = pltpu.PrefetchScalarGridSpec(
        num_scalar_prefetch=6,
        grid=(tables[0].shape[0],),
        in_specs=[pl.BlockSpec((G, 1, AUG, tq), lambda p, qi, kb, wi, fi, la, va: (0, wi[p], 0, qi[p])),
                  pl.BlockSpec((G, tk, AUG), lambda p, qi, kb, wi, fi, la, va: (0, kb[p], 0)),
                  pl.BlockSpec((G, V_ROWS, tk), lambda p, qi, kb, wi, fi, la, va: (0, 0, kb[p]))],
        out_specs=pl.BlockSpec((G, tq, HEAD_DIM), lambda p, qi, kb, wi, fi, la, va: (0, qi[p], 0)),
        scratch_shapes=[pltpu.VMEM((G, 8, tq), f32), pltpu.VMEM((G, V_ROWS, tq), f32)],
    )
    return pl.pallas_call(
        functools.partial(_flash_kernel, n_heads=G, tq=tq, qw=FLASH_QW, tq_pos=tq_pos, rep_cols=rep_cols,
                          pos_block=pos_block, window=window, variants=variants),
        grid_spec=grid_spec,
        out_shape=jax.ShapeDtypeStruct((G, Sq, HEAD_DIM), bf16),
        compiler_params=_cparams(("arbitrary",)),
        name=name,
    )(*tables, q, k, v)


def _out_kernel(x_ref, gl0_ref, gl1_ref, gl2_ref, sd_ref, bg_ref, om_ref, of_ref, oc_ref, os_ref, ow_ref, zs_ref,
                wm_ref, wn_ref, wf_ref, wo_ref, o_ref, *, tm):
    def up(o_of_head, n_heads, z_off, w_ref):
        y = jnp.zeros((tm, D_MODEL), f32)
        for h in range(n_heads):
            g = (o_of_head(h) * zs_ref[z_off + h].astype(f32)).astype(bf16)
            y = y + _dot(g, w_ref[h])
        return y

    sg = 1.0 / (1.0 + jnp.exp(-sd_ref[...]))

    def nsa_head(h):
        def rows(ref):
            starts = [(qb * NSA_HEADS + h) * Q_BLOCK for qb in range(tm // Q_BLOCK)]
            return jnp.concatenate([ref[r0:r0 + Q_BLOCK, :] for r0 in starts], axis=0).astype(f32)

        return (sg[:, 3 * h:3 * h + 1] * rows(oc_ref) + sg[:, 3 * h + 1:3 * h + 2] * rows(os_ref)
                + sg[:, 3 * h + 2:3 * h + 3] * rows(ow_ref))

    y_m = up(lambda h: om_ref[h].astype(f32), MOBA_HEADS, 0, wm_ref)
    y_n = up(nsa_head, NSA_HEADS, MOBA_HEADS, wn_ref)
    y_f = up(lambda h: of_ref[h].astype(f32), FOX_HEADS, MOBA_HEADS + NSA_HEADS, wf_ref)
    merged = jnp.zeros((tm, D_MODEL), f32)
    for b, (y, gl_ref) in enumerate(((y_m, gl0_ref), (y_n, gl1_ref), (y_f, gl2_ref))):
        cols = slice(b * D_MODEL, (b + 1) * D_MODEL)
        merged = merged + y * (1.0 / (1.0 + jnp.exp(-(gl_ref[...] + bg_ref[:, cols]))))
    o_ref[...] = x_ref[...] + _dot(merged.astype(bf16), wo_ref[...])


def out_proj(x, proj, b_gate, o_m, o_f, o_c, o_s, o_w, zs, w_m, w_n, w_f, w_o):
    S = x.shape[0]
    tm = 2 * Q_BLOCK
    rows = NSA_HEADS * tm
    full = lambda a: pl.BlockSpec(a.shape, lambda i: (0,) * a.ndim)
    gl_spec = lambda b: pl.BlockSpec((tm, D_MODEL), lambda i: (i, MAIN_COLS // D_MODEL + b))
    return pl.pallas_call(
        functools.partial(_out_kernel, tm=tm),
        grid=(S // tm,),
        in_specs=[pl.BlockSpec((tm, D_MODEL), lambda i: (i, 0)),
                  gl_spec(0), gl_spec(1), gl_spec(2),
                  pl.BlockSpec((tm, LANES), lambda i: (i, OFF_D // LANES)),
                  full(b_gate),
                  pl.BlockSpec((MOBA_HEADS, tm, HEAD_DIM), lambda i: (0, i, 0)),
                  pl.BlockSpec((FOX_HEADS, tm, HEAD_DIM), lambda i: (0, i, 0)),
                  pl.BlockSpec((rows, HEAD_DIM), lambda i: (i, 0)),
                  pl.BlockSpec((rows, HEAD_DIM), lambda i: (i, 0)),
                  pl.BlockSpec((rows, HEAD_DIM), lambda i: (i, 0)),
                  pl.BlockSpec((N_ZHEADS, tm, HEAD_DIM), lambda i: (0, i, 0)),
                  full(w_m), full(w_n), full(w_f), full(w_o)],
        out_specs=pl.BlockSpec((tm, D_MODEL), lambda i: (i, 0)),
        out_shape=jax.ShapeDtypeStruct((S, D_MODEL), f32),
        compiler_params=_cparams(("arbitrary",)),
        name="out_proj",
    )(x, proj, proj, proj, proj, b_gate, o_m, o_f, o_c, o_s, o_w, zs, w_m, w_n, w_f, w_o)


def _rope_tables(pos, rows):
    inv = ROPE_THETA ** (-jnp.arange(0, ROPE_DIM, 2, dtype=f32) / ROPE_DIM)
    ang = pos.astype(f32)[:, None] * inv[None, :]
    cos, sin = jnp.cos(ang), jnp.sin(ang)
    n = pos.shape[0]
    rest = HEAD_DIM - ROPE_DIM
    cos_h = jnp.concatenate([cos, cos, jnp.ones((n, rest), f32)], axis=1)
    sin_h = jnp.concatenate([-sin, sin, jnp.zeros((n, rest), f32)], axis=1)
    cos_t = jnp.concatenate([cos_h, cos_h], axis=1)
    sin_t = jnp.concatenate([sin_h, sin_h], axis=1)
    pad = ((0, rows - n), (0, 0))
    return jnp.pad(cos_t, pad), jnp.pad(sin_t, pad)


def _layer(x, norm_g, w_main, b_f, b_gate, moba_qk_g, nsa_q_g, nsa_k_g, fox_qk_g,
           cmp_pe, cmp_w1, cmp_w2, w_up_moba, w_up_nsa, w_up_fox, w_out, tables, rope):
    S = x.shape[0]
    proj = rms_matmul(x, norm_g.reshape(1, D_MODEL), w_main)

    gain_row = jnp.concatenate([jnp.tile(moba_qk_g[0], MOBA_HEADS), jnp.tile(moba_qk_g[1], MOBA_HEADS),
                                jnp.tile(nsa_q_g, NSA_HEADS), nsa_k_g[1], nsa_k_g[2],
                                jnp.tile(fox_qk_g[0], FOX_HEADS), jnp.tile(fox_qk_g[1], FOX_HEADS)]).reshape(1, SEG_A)
    bf_row = jnp.zeros((1, LANES), f32).at[0, FF_LANE:FF_LANE + FOX_HEADS].set(b_f)
    (mq, mk, nq, ksl, kw, fq, fk, mv, fv, vsl, vw, zs) = prep(proj, rope[0], rope[1], gain_row, bf_row)

    q_moba = moba_select(mq, moba_kmean(mk))
    o_m = flash(q_moba, mk, mv, tables["causal"], tq=FLASH_TK, pos_block=FLASH_TK, window=None, name="flash_moba")

    pe = jnp.concatenate([cmp_pe[0], cmp_pe[1]], axis=1)
    w1 = cmp_w1.astype(bf16).reshape(2, NSA_CMP_LEN, HEAD_DIM, NSA_CMP_HIDDEN)
    zero = jnp.zeros_like(w1[0])
    w1 = jnp.concatenate([jnp.concatenate([w1[0], zero], axis=2),
                          jnp.concatenate([zero, w1[1]], axis=2)], axis=1)
    w2p = jnp.pad(cmp_w2, ((0, 0), (0, 0), (0, LANES - HEAD_DIM))).astype(bf16)
    gain_c = jnp.pad(nsa_k_g[0], (0, LANES - HEAD_DIM)).reshape(1, LANES)
    kc, vc = nsa_compress(proj, pe, w1, w2p, gain_c, rope[2], rope[3])
    o_c, q_slc = nsa_cmp_select(nq, kc, vc, S)
    nsa_cols = NSA_HEADS * FLASH_TK
    o_s = flash(q_slc, ksl[None], vsl[None], tables["slc"], tq=nsa_cols,
                pos_block=Q_BLOCK, window=None, name="flash_nsa_slc")[0]
    o_w = flash(q_slc, kw[None], vw[None], tables["win"], tq=nsa_cols,
                pos_block=Q_BLOCK, window=NSA_WINDOW, name="flash_nsa_win")[0]

    o_f = flash(fq[:, None], fk, fv, tables["causal"], tq=FLASH_TK, pos_block=FLASH_TK, window=None, name="flash_fox")
    return out_proj(x, proj, b_gate.reshape(1, N_BRANCH * D_MODEL), o_m, o_f, o_c, o_s, o_w, zs,
                    w_up_moba.astype(bf16).reshape(MOBA_HEADS, HEAD_DIM, D_MODEL),
                    w_up_nsa.astype(bf16).reshape(NSA_HEADS, HEAD_DIM, D_MODEL),
                    w_up_fox.astype(bf16).reshape(FOX_HEADS, HEAD_DIM, D_MODEL),
                    w_out.astype(bf16))


def kernel(x, norm_g, w_in, b_f, b_gate, moba_qk_g, nsa_q_g, nsa_k_g, fox_qk_g, cmp_pe, cmp_w1, cmp_w2,
           w_up_moba, w_up_nsa, w_up_fox, w_out):
    B, S, _ = x.shape
    assert B == 1 and S % 2048 == 0 and S // MOBA_BLOCK <= CODE_BLOCKS
    depth = norm_g.shape[0]
    win_keys = CODE_BLOCKS * NSA_SLC_BLOCK
    n_q = S // FLASH_TK
    causal = lambda a: {t: (2 if t == a else 1) for t in range(a + 1)}
    band = lambda a: {t: 2 for t in range(max(0, a - NSA_WINDOW // FLASH_TK), a + 1)}
    tables = {"causal": _step_tables(n_q, causal, S), "slc": _step_tables(n_q, causal, win_keys),
              "win": _step_tables(n_q, band, S)}
    ncp = S // NSA_CMP_STRIDE
    cos_t, sin_t = _rope_tables(jnp.arange(S), S)
    cmp_end = jnp.arange(ncp - 1) * NSA_CMP_STRIDE + (NSA_CMP_LEN - 1)
    cos_c, sin_c = _rope_tables(cmp_end, ncp)
    rope = (cos_t, sin_t, cos_c, sin_c)
    h = x[0]
    for l in range(depth):
        h = _layer(h, norm_g[l], repack_w_in(w_in, l), b_f[l], b_gate[l], moba_qk_g[l], nsa_q_g[l], nsa_k_g[l], fox_qk_g[l],
                   cmp_pe[l], cmp_w1[l], cmp_w2[l], w_up_moba[l], w_up_nsa[l], w_up_fox[l], w_out[l], tables, rope)
    return h[None]
```

```python
import functools

import numpy as np
import jax
import jax.numpy as jnp
from jax import lax
from jax.experimental import pallas as pl
from jax.experimental.pallas import tpu as pltpu

D_MODEL = 1024
HEAD_DIM = 64
ROPE_DIM = HEAD_DIM // 4
ROPE_THETA = 500000.0
RMS_EPS = 1e-6
NEG_INF = -1e30
M_FLOOR = -1e20

MOBA_HEADS = 6
MOBA_BLOCK = 256
MOBA_TOPK = 3
NSA_HEADS = 4
NSA_CMP_LEN = 32
NSA_CMP_STRIDE = 16
NSA_CMP_HIDDEN = 4 * HEAD_DIM
NSA_SLC_BLOCK = 64
NSA_SLC_TOPN = 16
NSA_WINDOW = 512
FOX_HEADS = 6
N_BRANCH = 3
MOBA_W = MOBA_HEADS * HEAD_DIM
NSA_W = NSA_HEADS * HEAD_DIM
FOX_W = FOX_HEADS * HEAD_DIM
IN_SPLITS = (MOBA_W,) * 4 + (NSA_W,) + (HEAD_DIM,) * 6 + (3 * NSA_HEADS, NSA_W) + (FOX_W,) * 3 + (FOX_HEADS, FOX_W, N_BRANCH * D_MODEL)
ATTN_SCALE = HEAD_DIM ** -0.5

LANES = 128
AUG = 2 * HEAD_DIM
CODE_BLOCKS = AUG - HEAD_DIM
Q_BLOCK = 128
N_ZHEADS = MOBA_HEADS + NSA_HEADS + FOX_HEADS

SEG_A = 2 * MOBA_W + NSA_W + 2 * HEAD_DIM + 2 * FOX_W
SEG_B = 2 * MOBA_W + 4 * HEAD_DIM
SEG_C = MOBA_W + NSA_W + FOX_W
SEG_D = LANES
OFF_B = SEG_A
OFF_C = OFF_B + SEG_B
OFF_D = OFF_C + SEG_C
MAIN_COLS = OFF_D + SEG_D
FF_LANE = 3 * NSA_HEADS
KV_CMP_OFF = OFF_B + 2 * MOBA_W + 2 * HEAD_DIM
ALL_COLS = MAIN_COLS + N_BRANCH * D_MODEL

VMEM_LIMIT = 56 * 1024 * 1024
FLASH_QW = 512
FLASH_TQ = 512
FLASH_TK = 256
FLASH_SUB = 4
LOG2E = 1.4426950408889634
Q_SCALE = ATTN_SCALE * LOG2E
V_ROWS = 80

f32 = jnp.float32
bf16 = jnp.bfloat16


def _cparams(sem):
    return pltpu.CompilerParams(dimension_semantics=sem, vmem_limit_bytes=VMEM_LIMIT)


def _iota2(shape, dim):
    return lax.broadcasted_iota(jnp.int32, shape, dim)


def _div(x, d):
    return jnp.right_shift(x, int(d).bit_length() - 1)


def _mod(x, d):
    return jnp.bitwise_and(x, d - 1)


def _place(n_in, n_out, in_off, out_off, width=HEAD_DIM, val=1.0):
    r = _iota2((n_in, n_out), 0) - in_off
    c = _iota2((n_in, n_out), 1) - out_off
    hit = (r == c) & (r >= 0) & (r < width)
    return jnp.where(hit, val, 0.0).astype(bf16)


def _place_t(n_out, n_in, in_off, out_off, width=HEAD_DIM, val=1.0):
    r = _iota2((n_out, n_in), 0) - out_off
    c = _iota2((n_out, n_in), 1) - in_off
    hit = (r == c) & (r >= 0) & (r < width)
    return jnp.where(hit, val, 0.0).astype(bf16)


def _split3(x):
    hi = x.astype(bf16)
    r = x - hi.astype(f32)
    mid = r.astype(bf16)
    lo = (r - mid.astype(f32)).astype(bf16)
    return hi, mid, lo


def _dot(a, b):
    return jnp.dot(a, b, preferred_element_type=f32)


def _dot_nt(a, b):
    return lax.dot_general(a, b, (((1,), (1,)), ((), ())), preferred_element_type=f32)


def _w_in_plan():
    offs = np.concatenate([[0], np.cumsum(IN_SPLITS)])
    names = ("mq", "mk", "mv", "mz", "nq", "kc", "vc", "ksl", "vsl", "kw", "vw", "ng", "nz", "fq", "fk", "fv", "ff", "fz", "gl")
    start = {n: int(offs[j]) for j, n in enumerate(names)}
    width = {n: int(IN_SPLITS[j]) for j, n in enumerate(names)}
    order = ("mq", "mk", "nq", "ksl", "kw", "fq", "fk", "mv", "fv", "vsl", "vw", "kc", "vc", "mz", "nz", "fz", "ng", "ff")
    plan = [[] for _ in range(ALL_COLS // LANES)]
    new = 0
    for n in order + ("pad", "gl"):
        if n == "pad":
            new = MAIN_COLS
            continue
        src, left = start[n], width[n]
        while left > 0:
            w = min(left, LANES - new % LANES, LANES - src % LANES)
            plan[new // LANES].append((src // LANES, src % LANES, new % LANES, w))
            src, new, left = src + w, new + w, left - w
    assert new == ALL_COLS
    return plan


def _repack_kernel(w_ref, o_ref, *, plan, tr, n_cols):
    lane = _iota2((tr, LANES), 1)
    loaded = {}

    def source(a):
        if a not in loaded:
            x = w_ref[0, :, a * LANES:(a + 1) * LANES]
            if (a + 1) * LANES > n_cols:
                x = jnp.where(lane < n_cols - a * LANES, x, 0.0)
            loaded[a] = x.astype(bf16)
        return loaded[a]

    for b, pieces in enumerate(plan):
        acc = jnp.zeros((tr, LANES), f32)
        for a, lane_in, lane_out, w in pieces:
            acc = acc + _dot(source(a), _place(LANES, LANES, lane_in, lane_out, width=w))
        o_ref[:, b * LANES:(b + 1) * LANES] = acc.astype(bf16)


def repack_w_in(w_in, layer, tr=128):
    _, D, n_cols = w_in.shape
    padded = pl.cdiv(n_cols, LANES) * LANES
    return pl.pallas_call(
        functools.partial(_repack_kernel, plan=_w_in_plan(), tr=tr, n_cols=n_cols),
        grid=(D // tr,),
        in_specs=[pl.BlockSpec((1, tr, padded), lambda i: (layer, i, 0))],
        out_specs=pl.BlockSpec((tr, ALL_COLS), lambda i: (i, 0)),
        out_shape=jax.ShapeDtypeStruct((D, ALL_COLS), bf16),
        compiler_params=_cparams(("arbitrary",)),
        name="repack_w_in",
    )(w_in)


def _rms_matmul_kernel(x_ref, g_ref, w_ref, o_ref):
    x = x_ref[...]
    ms = jnp.mean(x * x, axis=-1, keepdims=True)
    h = (x * lax.rsqrt(ms + RMS_EPS) * g_ref[...]).astype(bf16)
    o_ref[...] = _dot(h, w_ref[...])


def rms_matmul(x, g, w, tm=1024, tn=1024):
    S, D = x.shape
    N = w.shape[1]
    return pl.pallas_call(
        _rms_matmul_kernel,
        grid=(N // tn, S // tm),
        in_specs=[pl.BlockSpec((tm, D), lambda j, i: (i, 0)),
                  pl.BlockSpec((1, D), lambda j, i: (0, 0)),
                  pl.BlockSpec((D, tn), lambda j, i: (0, j))],
        out_specs=pl.BlockSpec((tm, tn), lambda j, i: (i, j)),
        out_shape=jax.ShapeDtypeStruct((S, N), f32),
        compiler_params=_cparams(("arbitrary", "arbitrary")),
        name="rms_matmul",
    )(x, g, w)


def _prep_kernel(p_ref, cos_ref, sin_ref, gain_ref, bf_ref,
                 mq_ref, mk_ref, nq_ref, ksl_ref, kw_ref, fq_ref, fk_ref,
                 mv_ref, fv_ref, vsl_ref, vw_ref, zs_ref, carry_sc, *, ts):
    i = pl.program_id(0)

    @pl.when(i == 0)
    def _():
        carry_sc[...] = jnp.zeros_like(carry_sc)

    lane = _iota2((ts, LANES), 1)
    pos = _iota2((ts, LANES), 0) + i * ts
    blockdiag = jnp.where(_div(_iota2((LANES, LANES), 0), HEAD_DIM) == _div(_iota2((LANES, LANES), 1), HEAD_DIM),
                          1.0, 0.0).astype(bf16)
    to64 = [_place(LANES, HEAD_DIM, 0, 0), _place(LANES, HEAD_DIM, HEAD_DIM, 0)]
    to128 = [_place(LANES, AUG, 0, 0), _place(LANES, AUG, HEAD_DIM, 0)]
    rows_q = [_place_t(HEAD_DIM, LANES, 0, 0), _place_t(HEAD_DIM, LANES, HEAD_DIM, 0)]
    rows_aug = [_place_t(AUG, LANES, 0, 0), _place_t(AUG, LANES, HEAD_DIM, 0)]
    rows_v = [_place_t(V_ROWS, LANES, 0, 0), _place_t(V_ROWS, LANES, HEAD_DIM, 0)]
    rowi = _iota2((AUG, ts), 0)
    ones_row = jnp.where(_iota2((V_ROWS, ts), 0) == HEAD_DIM, 1.0, 0.0)
    first_half = _mod(lane, HEAD_DIM) < (ROPE_DIM // 2)
    cos = cos_ref[...]
    sin = sin_ref[...]

    def normed(c, rope, scale=None):
        x = p_ref[:, c * LANES:(c + 1) * LANES]
        x2 = x * x
        hi = x2.astype(bf16)
        lo = (x2 - hi.astype(f32)).astype(bf16)
        ss = _dot(hi, blockdiag) + _dot(lo, blockdiag)
        y = x * lax.rsqrt(ss * (1.0 / HEAD_DIM) + RMS_EPS) * gain_ref[:, c * LANES:(c + 1) * LANES]
        if rope:
            up = pltpu.roll(y, LANES - ROPE_DIM // 2, 1)
            dn = pltpu.roll(y, ROPE_DIM // 2, 1)
            y = y * cos + jnp.where(first_half, up, dn) * sin
        if scale is not None:
            y = y * scale
        return y.astype(bf16)

    d = p_ref[:, OFF_D:OFF_D + LANES] + bf_ref[...]
    logf = jnp.minimum(d, 0.0) - jnp.log(1.0 + jnp.exp(-jnp.abs(d)))
    tri = jnp.where(_iota2((ts, ts), 1) <= _iota2((ts, ts), 0), 1.0, 0.0).astype(bf16)
    lh, lm, ll = _split3(logf)
    c = carry_sc[0:1, :] + (_dot(tri, lh) + _dot(tri, lm) + _dot(tri, ll))
    carry_sc[...] = jnp.broadcast_to(c[ts - 1:ts, :], carry_sc.shape)
    ch, cm, cl = _split3(c * LOG2E)
    one_q = jnp.where((rowi >= HEAD_DIM + 3) & (rowi < HEAD_DIM + 6), 1.0, 0.0)
    one_k = jnp.where((lane >= HEAD_DIM) & (lane < HEAD_DIM + 3), 1.0, 0.0)

    def decay_cols(h, base):
        src = FF_LANE + h
        return (_dot(ch, _place(LANES, AUG, src, base, width=1))
                + _dot(cm, _place(LANES, AUG, src, base + 1, width=1))
                + _dot(cl, _place(LANES, AUG, src, base + 2, width=1)))

    def decay_rows(h, base):
        src = FF_LANE + h
        return (_dot_nt(_place_t(AUG, LANES, src, base, width=1), ch)
                + _dot_nt(_place_t(AUG, LANES, src, base + 1, width=1), cm)
                + _dot_nt(_place_t(AUG, LANES, src, base + 2, width=1), cl))

    moba_code = jnp.where((lane >= HEAD_DIM) & ((lane - HEAD_DIM) == _div(pos, MOBA_BLOCK)), 1.0, 0.0)
    for c_i in range(3):
        yq = normed(c_i, True, Q_SCALE)
        yk = normed(3 + c_i, True)
        for half in range(2):
            h = 2 * c_i + half
            mq_ref[h] = _dot_nt(rows_q[half], yq).astype(bf16)
            mk_ref[h] = (_dot(yk, to128[half]) + moba_code).astype(bf16)
    for c_i in range(2):
        y = normed(6 + c_i, True, Q_SCALE)
        for half in range(2):
            h = 2 * c_i + half
            yh = _dot_nt(rows_q[half], y).astype(bf16)
            for qb in range(ts // Q_BLOCK):
                nq_ref[:, (qb * NSA_HEADS + h) * Q_BLOCK:(qb * NSA_HEADS + h + 1) * Q_BLOCK] = (
                    yh[:, qb * Q_BLOCK:(qb + 1) * Q_BLOCK])
    y = normed(8, True)
    slc_code = jnp.where((lane >= HEAD_DIM) & ((lane - HEAD_DIM) == _mod(_div(pos, NSA_SLC_BLOCK), CODE_BLOCKS)), 1.0, 0.0)
    ksl_ref[...] = (_dot(y, to128[0]) + slc_code).astype(bf16)
    kw_ref[...] = _dot(y, to128[1]).astype(bf16)
    for c_i in range(3):
        yq = normed(9 + c_i, False, Q_SCALE)
        yk = normed(12 + c_i, False)
        for half in range(2):
            h = 2 * c_i + half
            fq_ref[h] = (_dot_nt(rows_aug[half], yq) + decay_rows(h, HEAD_DIM) + one_q).astype(bf16)
            fk_ref[h] = (_dot(yk, to128[half]) - decay_cols(h, HEAD_DIM + 3) + one_k).astype(bf16)
    for c_i in range(3):
        xm = p_ref[:, OFF_B + c_i * LANES:OFF_B + (c_i + 1) * LANES].astype(bf16)
        xf = p_ref[:, OFF_B + MOBA_W + c_i * LANES:OFF_B + MOBA_W + (c_i + 1) * LANES].astype(bf16)
        for half in range(2):
            mv_ref[2 * c_i + half] = (_dot_nt(rows_v[half], xm) + ones_row).astype(bf16)
            fv_ref[2 * c_i + half] = (_dot_nt(rows_v[half], xf) + ones_row).astype(bf16)
    xs = p_ref[:, OFF_B + 2 * MOBA_W:OFF_B + 2 * MOBA_W + LANES].astype(bf16)
    vsl_ref[...] = (_dot_nt(rows_v[0], xs) + ones_row).astype(bf16)
    vw_ref[...] = (_dot_nt(rows_v[1], xs) + ones_row).astype(bf16)
    for c_i in range(SEG_C // LANES):
        z = p_ref[:, OFF_C + c_i * LANES:OFF_C + (c_i + 1) * LANES]
        zs = (z * (1.0 / (1.0 + jnp.exp(-z)))).astype(bf16)
        for half in range(2):
            zs_ref[2 * c_i + half] = _dot(zs, to64[half]).astype(bf16)


def prep(proj, cos_t, sin_t, gain_row, bf_row, ts=256):
    S = proj.shape[0]
    head64 = lambda n: jax.ShapeDtypeStruct((n, S, HEAD_DIM), bf16)
    head128 = lambda n: jax.ShapeDtypeStruct((n, S, AUG), bf16)
    spec_h = lambda n, w: pl.BlockSpec((n, ts, w), lambda i: (0, i, 0))
    spec_r = lambda w: pl.BlockSpec((ts, w), lambda i: (i, 0))
    head_t = lambda n, r: jax.ShapeDtypeStruct((n, r, S), bf16)
    spec_ht = lambda n, r: pl.BlockSpec((n, r, ts), lambda i: (0, 0, i))
    spec_t = pl.BlockSpec((V_ROWS, ts), lambda i: (0, i))
    out_shape = (head_t(MOBA_HEADS, HEAD_DIM), head128(MOBA_HEADS),
                 jax.ShapeDtypeStruct((HEAD_DIM, NSA_HEADS * S), bf16),
                 jax.ShapeDtypeStruct((S, AUG), bf16), jax.ShapeDtypeStruct((S, AUG), bf16),
                 head_t(FOX_HEADS, AUG), head128(FOX_HEADS),
                 head_t(MOBA_HEADS, V_ROWS), head_t(FOX_HEADS, V_ROWS),
                 jax.ShapeDtypeStruct((V_ROWS, S), bf16), jax.ShapeDtypeStruct((V_ROWS, S), bf16),
                 head64(N_ZHEADS))
    out_specs = (spec_ht(MOBA_HEADS, HEAD_DIM), spec_h(MOBA_HEADS, AUG),
                 pl.BlockSpec((HEAD_DIM, NSA_HEADS * ts), lambda i: (0, i)),
                 spec_r(AUG), spec_r(AUG),
                 spec_ht(FOX_HEADS, AUG), spec_h(FOX_HEADS, AUG),
                 spec_ht(MOBA_HEADS, V_ROWS), spec_ht(FOX_HEADS, V_ROWS),
                 spec_t, spec_t,
                 spec_h(N_ZHEADS, HEAD_DIM))
    return pl.pallas_call(
        functools.partial(_prep_kernel, ts=ts),
        grid=(S // ts,),
        in_specs=[pl.BlockSpec((ts, MAIN_COLS), lambda i: (i, 0)),
                  spec_r(LANES), spec_r(LANES),
                  pl.BlockSpec((1, SEG_A), lambda i: (0, 0)),
                  pl.BlockSpec((1, LANES), lambda i: (0, 0))],
        out_specs=out_specs,
        out_shape=out_shape,
        scratch_shapes=[pltpu.VMEM((8, LANES), f32)],
        compiler_params=_cparams(("arbitrary",)),
        name="prep",
    )(proj, cos_t, sin_t, gain_row, bf_row)


def _kmean_kernel(k_ref, o_ref, *, rows):
    n = rows // MOBA_BLOCK
    avg = jnp.where(_div(_iota2((n, rows), 1), MOBA_BLOCK) == _iota2((n, rows), 0),
                    1.0 / MOBA_BLOCK, 0.0).astype(bf16)
    o_ref[0] = _dot(avg, k_ref[0])[:, :HEAD_DIM]


def moba_kmean(mk_aug):
    H, S, _ = mk_aug.shape
    rows = 8 * MOBA_BLOCK
    return pl.pallas_call(
        functools.partial(_kmean_kernel, rows=rows),
        grid=(H, S // rows),
        in_specs=[pl.BlockSpec((1, rows, AUG), lambda h, i: (h, i, 0))],
        out_specs=pl.BlockSpec((1, 8, HEAD_DIM), lambda h, i: (h, i, 0)),
        out_shape=jax.ShapeDtypeStruct((H, CODE_BLOCKS, HEAD_DIM), f32),
        compiler_params=_cparams(("arbitrary", "arbitrary")),
        name="moba_kmean",
    )(mk_aug)


def _top_select(scores, idx, n_pick, floor):
    big = jnp.int32(2 ** 30)
    scores = list(scores)
    for _ in range(n_pick):
        for j, score in enumerate(scores):
            m = jnp.max(score, axis=0, keepdims=True)
            first = jnp.min(jnp.where(score == m, idx, big), axis=0, keepdims=True)
            first = jnp.where(m > floor, first, big)
            scores[j] = jnp.where(idx == first, -jnp.inf, score)
    return [s == -jnp.inf for s in scores]


def _moba_select_kernel(q_ref, km_ref, o_ref, *, tq):
    i = pl.program_id(1)
    q = q_ref[0]
    km = km_ref[0]
    km_hi = km.astype(bf16)
    km_lo = (km - km_hi.astype(f32)).astype(bf16)
    gate = _dot(km_hi, q) + _dot(km_lo, q)
    blk = _iota2((CODE_BLOCKS, tq), 0)
    cur = _div(_iota2((CODE_BLOCKS, tq), 1) + i * tq, MOBA_BLOCK)
    score = jnp.where(blk < cur, gate, NEG_INF)
    sel, = _top_select([score], blk, MOBA_TOPK, NEG_INF)
    bias = jnp.where(sel | (blk == cur), 0.0, NEG_INF).astype(bf16)
    o_ref[0, 0, 0:HEAD_DIM, :] = q
    o_ref[0, 0, HEAD_DIM:AUG, :] = bias


def moba_select(mq, kmean, tq=512):
    H, _, S = mq.shape
    return pl.pallas_call(
        functools.partial(_moba_select_kernel, tq=tq),
        grid=(H, S // tq),
        in_specs=[pl.BlockSpec((1, HEAD_DIM, tq), lambda h, i: (h, 0, i)),
                  pl.BlockSpec((1, CODE_BLOCKS, HEAD_DIM), lambda h, i: (h, 0, 0))],
        out_specs=pl.BlockSpec((1, 1, AUG, tq), lambda h, i: (h, 0, 0, i)),
        out_shape=jax.ShapeDtypeStruct((H, 1, AUG, S), bf16),
        compiler_params=_cparams(("arbitrary", "arbitrary")),
        name="moba_select",
    )(mq, kmean)


def _cmp_kernel(x_ref, pe_ref, w1_ref, w2_ref, gain_ref, cos_ref, sin_ref, kc_ref, vc_ref, *, ncp):
    top = jnp.zeros((ncp, 2 * NSA_CMP_HIDDEN), f32)
    nxt = jnp.zeros((ncp, 2 * NSA_CMP_HIDDEN), f32)
    for j in range(NSA_CMP_STRIDE):
        xj = x_ref[pl.ds(j, ncp, stride=NSA_CMP_STRIDE), :]
        top = top + _dot((xj + pe_ref[j:j + 1, :]).astype(bf16), w1_ref[j])
        nxt = nxt + _dot((xj + pe_ref[NSA_CMP_STRIDE + j:NSA_CMP_STRIDE + j + 1, :]).astype(bf16),
                         w1_ref[NSA_CMP_STRIDE + j])
    hid = top + pltpu.roll(nxt, ncp - 1, 0)
    act = (hid * (1.0 / (1.0 + jnp.exp(-hid)))).astype(bf16)
    k = _dot(act[:, :NSA_CMP_HIDDEN], w2_ref[0])
    v = _dot(act[:, NSA_CMP_HIDDEN:], w2_ref[1])
    ms = jnp.sum(k * k, axis=-1, keepdims=True) * (1.0 / HEAD_DIM)
    y = k * lax.rsqrt(ms + RMS_EPS) * gain_ref[...]
    lane = _iota2((ncp, LANES), 1)
    up = pltpu.roll(y, LANES - ROPE_DIM // 2, 1)
    dn = pltpu.roll(y, ROPE_DIM // 2, 1)
    y = y * cos_ref[...] + jnp.where(lane < ROPE_DIM // 2, up, dn) * sin_ref[...]
    kc_ref[...] = y[:, :HEAD_DIM].astype(bf16)
    vc_ref[...] = _dot_nt(_place_t(V_ROWS, LANES, 0, 0), v.astype(bf16)).astype(bf16)


def nsa_compress(proj, pe, w1, w2, gain, cos_c, sin_c):
    S = proj.shape[0]
    ncp = S // NSA_CMP_STRIDE
    full = lambda a: pl.BlockSpec(a.shape, lambda i: (0,) * a.ndim)
    return pl.pallas_call(
        functools.partial(_cmp_kernel, ncp=ncp),
        grid=(1,),
        in_specs=[pl.BlockSpec((S, LANES), lambda i: (0, KV_CMP_OFF // LANES)),
                  full(pe), full(w1), full(w2), full(gain), full(cos_c), full(sin_c)],
        out_specs=(pl.BlockSpec((ncp, HEAD_DIM), lambda i: (0, 0)), pl.BlockSpec((V_ROWS, ncp), lambda i: (0, 0))),
        out_shape=(jax.ShapeDtypeStruct((ncp, HEAD_DIM), bf16), jax.ShapeDtypeStruct((V_ROWS, ncp), bf16)),
        compiler_params=_cparams(("arbitrary",)),
        name="nsa_compress",
    )(proj, pe, w1, w2, gain, cos_c, sin_c)


def _nsa_cmp_select_kernel(q_ref, kc_ref, vc_ref, oc_ref, qa_ref, *, ncp, nsp, n_win, n_qb):
    i = pl.program_id(0)
    cols = NSA_HEADS * Q_BLOCK
    kc = kc_ref[...]
    vc = vc_ref[...]
    c0 = _iota2((nsp, ncp), 1) * NSA_CMP_STRIDE
    b0 = _iota2((nsp, ncp), 0) * NSA_SLC_BLOCK
    overlap = jnp.where((c0 <= b0 + (NSA_SLC_BLOCK - 1)) & (c0 + (NSA_CMP_LEN - 1) >= b0), 1.0, 0.0).astype(bf16)
    kend = _iota2((ncp, cols), 0) * NSA_CMP_STRIDE + (NSA_CMP_LEN - 1)
    col_pos = _mod(_iota2((ncp, cols), 1), Q_BLOCK)
    blk = _iota2((nsp, Q_BLOCK), 0)
    pad_rows = jnp.zeros((LANES - V_ROWS, cols), f32)

    def scores(b):
        return _dot(kc, q_ref[:, b * cols:(b + 1) * cols])

    def probs(b, s):
        ok = kend <= (i * n_qb + b) * Q_BLOCK + col_pos
        sm = jnp.where(ok, s, NEG_INF)
        e = jnp.exp2(sm - jnp.max(sm, axis=0, keepdims=True))
        return jnp.where(ok, e * (1.0 / jnp.sum(e, axis=0, keepdims=True)), 0.0)

    def cur(b):
        return _div(_iota2((nsp, Q_BLOCK), 1) + (i * n_qb + b) * Q_BLOCK, NSA_SLC_BLOCK)

    def forced(b):
        return (blk == 0) | (blk == cur(b)) | (blk == cur(b) - 1)

    def importance(b, p):
        o = _dot(vc, p.astype(bf16))
        oc_ref[b * cols:(b + 1) * cols, :] = jnp.concatenate([o, pad_rows], axis=0).T[:, :HEAD_DIM].astype(bf16)
        psum = (p[:, 0:Q_BLOCK] + p[:, Q_BLOCK:2 * Q_BLOCK]
                + p[:, 2 * Q_BLOCK:3 * Q_BLOCK] + p[:, 3 * Q_BLOCK:4 * Q_BLOCK])
        ph, pm, plo = _split3(psum)
        imp = _dot(overlap, ph) + _dot(overlap, pm) + _dot(overlap, plo)
        return jnp.where((blk <= cur(b)) & jnp.logical_not(forced(b)), imp, NEG_INF)

    block_scores = []
    s_next = scores(0)
    for b in range(n_qb):
        s = s_next
        if b + 1 < n_qb:
            s_next = scores(b + 1)
        block_scores.append(importance(b, probs(b, s)))
    n_free = NSA_SLC_TOPN - 3
    for b, sel in enumerate(_top_select(block_scores, blk, n_free, NEG_INF)):
        bias = jnp.where(sel | forced(b), 0.0, NEG_INF).astype(bf16)
        for w in range(n_win):
            bw = bias[w * CODE_BLOCKS:(w + 1) * CODE_BLOCKS, :]
            qa_ref[0, w, 0:HEAD_DIM, b * cols:(b + 1) * cols] = q_ref[:, b * cols:(b + 1) * cols]
            qa_ref[0, w, HEAD_DIM:AUG, b * cols:(b + 1) * cols] = jnp.concatenate([bw] * NSA_HEADS, axis=1)


def nsa_cmp_select(nq, kc, vc, S, n_qb=4):
    ncp = kc.shape[0]
    ns = S // NSA_SLC_BLOCK
    nsp = max(LANES, ns)
    n_win = max(1, ns // CODE_BLOCKS)
    cols = n_qb * NSA_HEADS * Q_BLOCK
    return pl.pallas_call(
        functools.partial(_nsa_cmp_select_kernel, ncp=ncp, nsp=nsp, n_win=n_win, n_qb=n_qb),
        grid=(S // (n_qb * Q_BLOCK),),
        in_specs=[pl.BlockSpec((HEAD_DIM, cols), lambda i: (0, i)),
                  pl.BlockSpec((ncp, HEAD_DIM), lambda i: (0, 0)),
                  pl.BlockSpec((V_ROWS, ncp), lambda i: (0, 0))],
        out_specs=(pl.BlockSpec((cols, HEAD_DIM), lambda i: (i, 0)),
                   pl.BlockSpec((1, n_win, AUG, cols), lambda i: (0, 0, 0, i))),
        out_shape=(jax.ShapeDtypeStruct((NSA_HEADS * S, HEAD_DIM), bf16),
                   jax.ShapeDtypeStruct((1, n_win, AUG, NSA_HEADS * S), bf16)),
        compiler_params=_cparams(("arbitrary",)),
        name="nsa_cmp_select",
    )(nq, kc, vc)


def _flash_kernel(qi_t, kb_t, win_t, first_t, last_t, var_t, q_ref, k_ref, v_ref, o_ref, m_sc, acc_sc,
                  *, n_heads, tq, qw, tq_pos, rep_cols, pos_block, window, variants):
    p_id = pl.program_id(0)
    qi = qi_t[p_id]
    kb = kb_t[p_id]

    @pl.when(first_t[p_id] == 1)
    def _():
        m_sc[...] = jnp.full(m_sc.shape, M_FLOOR, f32)
        acc_sc[...] = jnp.zeros_like(acc_sc)

    def tile(modes):
        chains = [(g, slice(c * qw, (c + 1) * qw), slice(kt * FLASH_TK, (kt + 1) * FLASH_TK))
                  for kt in range(FLASH_SUB) if modes[kt] for g in range(n_heads) for c in range(tq // qw)]
        if 2 in modes:
            rel = _mod(_iota2((FLASH_TK, qw), 1), pos_block) - _iota2((FLASH_TK, qw), 0)

        def scores(t):
            g, cols, keys = chains[t]
            s = _dot(k_ref[g, keys, :], q_ref[g, 0, :, cols])
            if modes[keys.start // FLASH_TK] == 1:
                return s
            base = (qi * tq_pos + (cols.start // rep_cols) * pos_block + cols.start % pos_block
                    - kb * (FLASH_SUB * FLASH_TK) - keys.start)
            delta = rel + base
            ok = (delta >= 0) if window is None else (delta.astype(jnp.uint32) < jnp.uint32(window))
            return jnp.where(ok, s, NEG_INF)

        def softmax(t, s):
            g, cols, _ = chains[t]
            m_prev = m_sc[g, 0:1, cols]
            m_new = jnp.maximum(m_prev, jnp.max(s, axis=0, keepdims=True))
            m_sc[g, :, cols] = jnp.broadcast_to(m_new, (m_sc.shape[1], qw))
            return jnp.exp2(s - m_new).astype(bf16), jnp.exp2(m_prev - m_new)

        def accumulate(t, p, alpha):
            g, cols, keys = chains[t]
            acc_sc[g, :, cols] = alpha * acc_sc[g, :, cols] + _dot(v_ref[g, :, keys], p)

        n = len(chains)
        s_next, staged = scores(0), None
        for t in range(n + 1):
            s = s_next
            if t + 1 < n:
                s_next = scores(t + 1)
            if staged is not None:
                accumulate(t - 1, *staged)
            staged = softmax(t, s) if t < n else None

    for vid, modes in enumerate(variants):
        pl.when(var_t[p_id] == vid)(functools.partial(tile, modes))

    @pl.when(last_t[p_id] == 1)
    def _():
        for g in range(n_heads):
            acc = acc_sc[g]
            l = acc[HEAD_DIM:HEAD_DIM + 1, :]
            inv = jnp.where(l > 0.0, 1.0 / l, 0.0)
            o = jnp.concatenate([acc * inv, jnp.zeros((LANES - V_ROWS, tq), f32)], axis=0)
            o_ref[g] = o.T[:, :HEAD_DIM].astype(o_ref.dtype)


def _step_tables(n_q, key_tiles, win_keys):
    variants, cols = [], ([], [], [], [], [], [])
    for a in range(n_q):
        tiles = key_tiles(a)
        groups = sorted({t // FLASH_SUB for t in tiles})
        for kb in groups:
            modes = tuple(tiles.get(kb * FLASH_SUB + j, 0) for j in range(FLASH_SUB))
            if modes not in variants:
                variants.append(modes)
            row = (a, kb, (kb * FLASH_SUB * FLASH_TK) // win_keys, int(kb == groups[0]), int(kb == groups[-1]),
                   variants.index(modes))
            for c, v in zip(cols, row):
                c.append(v)
    return tuple(jnp.asarray(np.asarray(c, np.int32)) for c in cols), tuple(variants)


def flash(q, k, v, steps, *, tq, pos_block, window, name):
    tables, variants = steps
    G, _, _, Sq = q.shape
    tq_pos = FLASH_TQ
    rep_cols = (tq // tq_pos) * pos_block
    assert rep_cols % FLASH_QW == 0 and (pos_block % FLASH_QW == 0 or FLASH_QW % pos_block == 0)
    tk = FLASH_SUB * FLASH_TK
    grid_spec = pltpu.PrefetchScalarGridSpec(
        num_scalar_prefetch=6,
        grid=(tables[0].shape[0],),
        in_specs=[pl.BlockSpec((G, 1, AUG, tq), lambda p, qi, kb, wi, fi, la, va: (0, wi[p], 0, qi[p])),
                  pl.BlockSpec((G, tk, AUG), lambda p, qi, kb, wi, fi, la, va: (0, kb[p], 0)),
                  pl.BlockSpec((G, V_ROWS, tk), lambda p, qi, kb, wi, fi, la, va: (0, 0, kb[p]))],
        out_specs=pl.BlockSpec((G, tq, HEAD_DIM), lambda p, qi, kb, wi, fi, la, va: (0, qi[p], 0)),
        scratch_shapes=[pltpu.VMEM((G, 8, tq), f32), pltpu.VMEM((G, V_ROWS, tq), f32)],
    )
    return pl.pallas_call(
        functools.partial(_flash_kernel, n_heads=G, tq=tq, qw=FLASH_QW, tq_pos=tq_pos, rep_cols=rep_cols,
                          pos_block=pos_block, window=window, variants=variants),
        grid_spec=grid_spec,
        out_shape=jax.ShapeDtypeStruct((G, Sq, HEAD_DIM), bf16),
        compiler_params=_cparams(("arbitrary",)),
        name=name,
    )(*tables, q, k, v)


def _out_kernel(x_ref, gl0_ref, gl1_ref, gl2_ref, sd_ref, bg_ref, om_ref, of_ref, oc_ref, os_ref, ow_ref, zs_ref,
                wm_ref, wn_ref, wf_ref, wo_ref, o_ref, *, tm):
    def up(o_of_head, n_heads, z_off, w_ref):
        y = jnp.zeros((tm, D_MODEL), f32)
        for h in range(n_heads):
            g = (o_of_head(h) * zs_ref[z_off + h].astype(f32)).astype(bf16)
            y = y + _dot(g, w_ref[h])
        return y

    sg = 1.0 / (1.0 + jnp.exp(-sd_ref[...]))

    def nsa_head(h):
        def rows(ref):
            starts = [(qb * NSA_HEADS + h) * Q_BLOCK for qb in range(tm // Q_BLOCK)]
            return jnp.concatenate([ref[r0:r0 + Q_BLOCK, :] for r0 in starts], axis=0).astype(f32)

        return (sg[:, 3 * h:3 * h + 1] * rows(oc_ref) + sg[:, 3 * h + 1:3 * h + 2] * rows(os_ref)
                + sg[:, 3 * h + 2:3 * h + 3] * rows(ow_ref))

    y_m = up(lambda h: om_ref[h].astype(f32), MOBA_HEADS, 0, wm_ref)
    y_n = up(nsa_head, NSA_HEADS, MOBA_HEADS, wn_ref)
    y_f = up(lambda h: of_ref[h].astype(f32), FOX_HEADS, MOBA_HEADS + NSA_HEADS, wf_ref)
    merged = jnp.zeros((tm, D_MODEL), f32)
    for b, (y, gl_ref) in enumerate(((y_m, gl0_ref), (y_n, gl1_ref), (y_f, gl2_ref))):
        cols = slice(b * D_MODEL, (b + 1) * D_MODEL)
        merged = merged + y * (1.0 / (1.0 + jnp.exp(-(gl_ref[...] + bg_ref[:, cols]))))
    o_ref[...] = x_ref[...] + _dot(merged.astype(bf16), wo_ref[...])


def out_proj(x, proj, b_gate, o_m, o_f, o_c, o_s, o_w, zs, w_m, w_n, w_f, w_o):
    S = x.shape[0]
    tm = 2 * Q_BLOCK
    rows = NSA_HEADS * tm
    full = lambda a: pl.BlockSpec(a.shape, lambda i: (0,) * a.ndim)
    gl_spec = lambda b: pl.BlockSpec((tm, D_MODEL), lambda i: (i, MAIN_COLS // D_MODEL + b))
    return pl.pallas_call(
        functools.partial(_out_kernel, tm=tm),
        grid=(S // tm,),
        in_specs=[pl.BlockSpec((tm, D_MODEL), lambda i: (i, 0)),
                  gl_spec(0), gl_spec(1), gl_spec(2),
                  pl.BlockSpec((tm, LANES), lambda i: (i, OFF_D // LANES)),
                  full(b_gate),
                  pl.BlockSpec((MOBA_HEADS, tm, HEAD_DIM), lambda i: (0, i, 0)),
                  pl.BlockSpec((FOX_HEADS, tm, HEAD_DIM), lambda i: (0, i, 0)),
                  pl.BlockSpec((rows, HEAD_DIM), lambda i: (i, 0)),
                  pl.BlockSpec((rows, HEAD_DIM), lambda i: (i, 0)),
                  pl.BlockSpec((rows, HEAD_DIM), lambda i: (i, 0)),
                  pl.BlockSpec((N_ZHEADS, tm, HEAD_DIM), lambda i: (0, i, 0)),
                  full(w_m), full(w_n), full(w_f), full(w_o)],
        out_specs=pl.BlockSpec((tm, D_MODEL), lambda i: (i, 0)),
        out_shape=jax.ShapeDtypeStruct((S, D_MODEL), f32),
        compiler_params=_cparams(("arbitrary",)),
        name="out_proj",
    )(x, proj, proj, proj, proj, b_gate, o_m, o_f, o_c, o_s, o_w, zs, w_m, w_n, w_f, w_o)


def _rope_tables(pos, rows):
    half = ROPE_DIM // 2
    inv = ROPE_THETA ** (-jnp.arange(0, ROPE_DIM, 2, dtype=f32) / ROPE_DIM)
    d = np.arange(LANES) % HEAD_DIM
    rotated = d < ROPE_DIM
    inv_row = jnp.where(rotated, inv[d % half], 0.0)
    sign_row = np.where(d < half, -1.0, 1.0).astype(np.float32)
    ang = pos.astype(f32)[:, None] * inv_row[None, :]
    pad = ((0, rows - pos.shape[0]), (0, 0))
    return jnp.pad(jnp.cos(ang), pad), jnp.pad(jnp.sin(ang) * sign_row[None, :], pad)


def _layer(x, norm_g, w_main, b_f, b_gate, moba_qk_g, nsa_q_g, nsa_k_g, fox_qk_g,
           cmp_pe, cmp_w1, cmp_w2, w_up_moba, w_up_nsa, w_up_fox, w_out, tables, rope):
    S = x.shape[0]
    proj = rms_matmul(x, norm_g.reshape(1, D_MODEL), w_main)

    gain_row = jnp.concatenate([jnp.tile(moba_qk_g[0], MOBA_HEADS), jnp.tile(moba_qk_g[1], MOBA_HEADS),
                                jnp.tile(nsa_q_g, NSA_HEADS), nsa_k_g[1], nsa_k_g[2],
                                jnp.tile(fox_qk_g[0], FOX_HEADS), jnp.tile(fox_qk_g[1], FOX_HEADS)]).reshape(1, SEG_A)
    bf_row = jnp.zeros((1, LANES), f32).at[0, FF_LANE:FF_LANE + FOX_HEADS].set(b_f)
    (mq, mk, nq, ksl, kw, fq, fk, mv, fv, vsl, vw, zs) = prep(proj, rope[0], rope[1], gain_row, bf_row)

    q_moba = moba_select(mq, moba_kmean(mk))
    o_m = flash(q_moba, mk, mv, tables["causal"], tq=FLASH_TQ, pos_block=FLASH_TQ, window=None, name="flash_moba")

    pe = jnp.concatenate([cmp_pe[0], cmp_pe[1]], axis=1)
    w1 = cmp_w1.astype(bf16).reshape(2, NSA_CMP_LEN, HEAD_DIM, NSA_CMP_HIDDEN)
    zero = jnp.zeros_like(w1[0])
    w1 = jnp.concatenate([jnp.concatenate([w1[0], zero], axis=2),
                          jnp.concatenate([zero, w1[1]], axis=2)], axis=1)
    w2p = jnp.pad(cmp_w2, ((0, 0), (0, 0), (0, LANES - HEAD_DIM))).astype(bf16)
    gain_c = jnp.pad(nsa_k_g[0], (0, LANES - HEAD_DIM)).reshape(1, LANES)
    kc, vc = nsa_compress(proj, pe, w1, w2p, gain_c, rope[2], rope[3])
    o_c, q_slc = nsa_cmp_select(nq, kc, vc, S)
    nsa_cols = NSA_HEADS * FLASH_TQ
    o_s = flash(q_slc, ksl[None], vsl[None], tables["slc"], tq=nsa_cols,
                pos_block=Q_BLOCK, window=None, name="flash_nsa_slc")[0]
    o_w = flash(q_slc, kw[None], vw[None], tables["win"], tq=nsa_cols,
                pos_block=Q_BLOCK, window=NSA_WINDOW, name="flash_nsa_win")[0]

    o_f = flash(fq[:, None], fk, fv, tables["causal"], tq=FLASH_TQ, pos_block=FLASH_TQ, window=None, name="flash_fox")
    return out_proj(x, proj, b_gate.reshape(1, N_BRANCH * D_MODEL), o_m, o_f, o_c, o_s, o_w, zs,
                    w_up_moba.astype(bf16).reshape(MOBA_HEADS, HEAD_DIM, D_MODEL),
                    w_up_nsa.astype(bf16).reshape(NSA_HEADS, HEAD_DIM, D_MODEL),
                    w_up_fox.astype(bf16).reshape(FOX_HEADS, HEAD_DIM, D_MODEL),
                    w_out.astype(bf16))


def kernel(x, norm_g, w_in, b_f, b_gate, moba_qk_g, nsa_q_g, nsa_k_g, fox_qk_g, cmp_pe, cmp_w1, cmp_w2,
           w_up_moba, w_up_nsa, w_up_fox, w_out):
    B, S, _ = x.shape
    assert B == 1 and S % 2048 == 0 and S // MOBA_BLOCK <= CODE_BLOCKS
    depth = norm_g.shape[0]
    win_keys = CODE_BLOCKS * NSA_SLC_BLOCK
    n_q = S // FLASH_TQ
    causal = lambda a: {t: (2 if (t + 1) * FLASH_TK > a * FLASH_TQ else 1)
                        for t in range((a + 1) * FLASH_TQ // FLASH_TK)}
    band = lambda a: {t: 2 for t in range(max(0, a * FLASH_TQ - NSA_WINDOW + 1) // FLASH_TK,
                                          (a + 1) * FLASH_TQ // FLASH_TK)}
    tables = {"causal": _step_tables(n_q, causal, S), "slc": _step_tables(n_q, causal, win_keys),
              "win": _step_tables(n_q, band, S)}
    ncp = S // NSA_CMP_STRIDE
    cos_t, sin_t = _rope_tables(jnp.arange(S), S)
    cmp_end = jnp.arange(ncp - 1) * NSA_CMP_STRIDE + (NSA_CMP_LEN - 1)
    cos_c, sin_c = _rope_tables(cmp_end, ncp)
    rope = (cos_t, sin_t, cos_c, sin_c)
    h = x[0]
    for l in range(depth):
        h = _layer(h, norm_g[l], repack_w_in(w_in, l), b_f[l], b_gate[l], moba_qk_g[l], nsa_q_g[l], nsa_k_g[l], fox_qk_g[l],
                   cmp_pe[l], cmp_w1[l], cmp_w2[l], w_up_moba[l], w_up_nsa[l], w_up_fox[l], w_out[l], tables, rope)
    return h[None]
```

```python
import functools

import numpy as np
import jax
import jax.numpy as jnp
from jax import lax
from jax.experimental import pallas as pl
from jax.experimental.pallas import tpu as pltpu

D_MODEL = 1024
HEAD_DIM = 64
ROPE_DIM = HEAD_DIM // 4
ROPE_THETA = 500000.0
RMS_EPS = 1e-6
NEG_INF = -1e30
M_FLOOR = -1e20

MOBA_HEADS = 6
MOBA_BLOCK = 256
MOBA_TOPK = 3
NSA_HEADS = 4
NSA_CMP_LEN = 32
NSA_CMP_STRIDE = 16
NSA_CMP_HIDDEN = 4 * HEAD_DIM
NSA_SLC_BLOCK = 64
NSA_SLC_TOPN = 16
NSA_WINDOW = 512
FOX_HEADS = 6
N_BRANCH = 3
MOBA_W = MOBA_HEADS * HEAD_DIM
NSA_W = NSA_HEADS * HEAD_DIM
FOX_W = FOX_HEADS * HEAD_DIM
IN_SPLITS = (MOBA_W,) * 4 + (NSA_W,) + (HEAD_DIM,) * 6 + (3 * NSA_HEADS, NSA_W) + (FOX_W,) * 3 + (FOX_HEADS, FOX_W, N_BRANCH * D_MODEL)
ATTN_SCALE = HEAD_DIM ** -0.5

LANES = 128
AUG = 2 * HEAD_DIM
CODE_BLOCKS = AUG - HEAD_DIM
Q_BLOCK = 128
N_ZHEADS = MOBA_HEADS + NSA_HEADS + FOX_HEADS

SEG_A = 2 * MOBA_W + NSA_W + 2 * HEAD_DIM + 2 * FOX_W
SEG_B = 2 * MOBA_W + 4 * HEAD_DIM
SEG_C = MOBA_W + NSA_W + FOX_W
SEG_D = LANES
OFF_B = SEG_A
OFF_C = OFF_B + SEG_B
OFF_D = OFF_C + SEG_C
MAIN_COLS = OFF_D + SEG_D
FF_LANE = 3 * NSA_HEADS
KV_CMP_OFF = OFF_B + 2 * MOBA_W + 2 * HEAD_DIM
ALL_COLS = MAIN_COLS + N_BRANCH * D_MODEL

VMEM_LIMIT = 56 * 1024 * 1024
FLASH_QW = 512
FLASH_TQ = 512
FLASH_TK = 256
FLASH_SUB = 8
LOG2E = 1.4426950408889634
Q_SCALE = ATTN_SCALE * LOG2E
V_ROWS = 80

f32 = jnp.float32
bf16 = jnp.bfloat16


def _cparams(sem):
    return pltpu.CompilerParams(dimension_semantics=sem, vmem_limit_bytes=VMEM_LIMIT)


def _iota2(shape, dim):
    return lax.broadcasted_iota(jnp.int32, shape, dim)


def _div(x, d):
    return jnp.right_shift(x, int(d).bit_length() - 1)


def _mod(x, d):
    return jnp.bitwise_and(x, d - 1)


def _place(n_in, n_out, in_off, out_off, width=HEAD_DIM, val=1.0):
    r = _iota2((n_in, n_out), 0) - in_off
    c = _iota2((n_in, n_out), 1) - out_off
    hit = (r == c) & (r >= 0) & (r < width)
    return jnp.where(hit, val, 0.0).astype(bf16)


def _place_t(n_out, n_in, in_off, out_off, width=HEAD_DIM, val=1.0):
    r = _iota2((n_out, n_in), 0) - out_off
    c = _iota2((n_out, n_in), 1) - in_off
    hit = (r == c) & (r >= 0) & (r < width)
    return jnp.where(hit, val, 0.0).astype(bf16)


def _split3(x):
    hi = x.astype(bf16)
    r = x - hi.astype(f32)
    mid = r.astype(bf16)
    lo = (r - mid.astype(f32)).astype(bf16)
    return hi, mid, lo


def _dot(a, b):
    return jnp.dot(a, b, preferred_element_type=f32)


def _dot_nt(a, b):
    return lax.dot_general(a, b, (((1,), (1,)), ((), ())), preferred_element_type=f32)


def _w_in_plan():
    offs = np.concatenate([[0], np.cumsum(IN_SPLITS)])
    names = ("mq", "mk", "mv", "mz", "nq", "kc", "vc", "ksl", "vsl", "kw", "vw", "ng", "nz", "fq", "fk", "fv", "ff", "fz", "gl")
    start = {n: int(offs[j]) for j, n in enumerate(names)}
    width = {n: int(IN_SPLITS[j]) for j, n in enumerate(names)}
    order = ("mq", "mk", "nq", "ksl", "kw", "fq", "fk", "mv", "fv", "vsl", "vw", "kc", "vc", "mz", "nz", "fz", "ng", "ff")
    plan = [[] for _ in range(ALL_COLS // LANES)]
    new = 0
    for n in order + ("pad", "gl"):
        if n == "pad":
            new = MAIN_COLS
            continue
        src, left = start[n], width[n]
        while left > 0:
            w = min(left, LANES - new % LANES, LANES - src % LANES)
            plan[new // LANES].append((src // LANES, src % LANES, new % LANES, w))
            src, new, left = src + w, new + w, left - w
    assert new == ALL_COLS
    return plan


def _repack_kernel(w_ref, o_ref, *, plan, tr, n_cols):
    lane = _iota2((tr, LANES), 1)
    loaded = {}

    def source(a):
        if a not in loaded:
            x = w_ref[0, :, a * LANES:(a + 1) * LANES]
            if (a + 1) * LANES > n_cols:
                x = jnp.where(lane < n_cols - a * LANES, x, 0.0)
            loaded[a] = x.astype(bf16)
        return loaded[a]

    for b, pieces in enumerate(plan):
        acc = jnp.zeros((tr, LANES), f32)
        for a, lane_in, lane_out, w in pieces:
            acc = acc + _dot(source(a), _place(LANES, LANES, lane_in, lane_out, width=w))
        o_ref[:, b * LANES:(b + 1) * LANES] = acc.astype(bf16)


def repack_w_in(w_in, layer, tr=128):
    _, D, n_cols = w_in.shape
    padded = pl.cdiv(n_cols, LANES) * LANES
    return pl.pallas_call(
        functools.partial(_repack_kernel, plan=_w_in_plan(), tr=tr, n_cols=n_cols),
        grid=(D // tr,),
        in_specs=[pl.BlockSpec((1, tr, padded), lambda i: (layer, i, 0))],
        out_specs=pl.BlockSpec((tr, ALL_COLS), lambda i: (i, 0)),
        out_shape=jax.ShapeDtypeStruct((D, ALL_COLS), bf16),
        compiler_params=_cparams(("arbitrary",)),
        name="repack_w_in",
    )(w_in)


def _rms_matmul_kernel(x_ref, g_ref, w_ref, o_ref, h_sc):
    @pl.when(pl.program_id(1) == 0)
    def _():
        x = x_ref[...]
        ms = jnp.mean(x * x, axis=-1, keepdims=True)
        h_sc[...] = (x * lax.rsqrt(ms + RMS_EPS) * g_ref[...]).astype(bf16)

    o_ref[...] = _dot(h_sc[...], w_ref[...])


def rms_matmul(x, g, w, tm=1024, tn=1024):
    S, D = x.shape
    N = w.shape[1]
    return pl.pallas_call(
        _rms_matmul_kernel,
        grid=(S // tm, N // tn),
        in_specs=[pl.BlockSpec((tm, D), lambda i, j: (i, 0)),
                  pl.BlockSpec((1, D), lambda i, j: (0, 0)),
                  pl.BlockSpec((D, tn), lambda i, j: (0, j))],
        out_specs=pl.BlockSpec((tm, tn), lambda i, j: (i, j)),
        out_shape=jax.ShapeDtypeStruct((S, N), f32),
        scratch_shapes=[pltpu.VMEM((tm, D), bf16)],
        compiler_params=_cparams(("arbitrary", "arbitrary")),
        name="rms_matmul",
    )(x, g, w)


def _prep_kernel(p_ref, cos_ref, sin_ref, gain_ref, bf_ref,
                 mq_ref, mk_ref, nq_ref, ksl_ref, kw_ref, fq_ref, fk_ref,
                 mv_ref, fv_ref, vsl_ref, vw_ref, zs_ref, carry_sc, *, ts):
    i = pl.program_id(0)

    @pl.when(i == 0)
    def _():
        carry_sc[...] = jnp.zeros_like(carry_sc)

    lane = _iota2((ts, LANES), 1)
    pos = _iota2((ts, LANES), 0) + i * ts
    blockdiag = jnp.where(_div(_iota2((LANES, LANES), 0), HEAD_DIM) == _div(_iota2((LANES, LANES), 1), HEAD_DIM),
                          1.0, 0.0).astype(bf16)
    first_half = _mod(lane, HEAD_DIM) < (ROPE_DIM // 2)
    low_lanes = lane < HEAD_DIM
    cos = cos_ref[...]
    sin = sin_ref[...]

    def normed(c, rope, scale=None):
        x = p_ref[:, c * LANES:(c + 1) * LANES]
        x2 = x * x
        hi = x2.astype(bf16)
        lo = (x2 - hi.astype(f32)).astype(bf16)
        ss = _dot(hi, blockdiag) + _dot(lo, blockdiag)
        y = x * lax.rsqrt(ss * (1.0 / HEAD_DIM) + RMS_EPS) * gain_ref[:, c * LANES:(c + 1) * LANES]
        if rope:
            up = pltpu.roll(y, LANES - ROPE_DIM // 2, 1)
            dn = pltpu.roll(y, ROPE_DIM // 2, 1)
            y = y * cos + jnp.where(first_half, up, dn) * sin
        return y if scale is None else y * scale

    def head_rows(y_t, half):
        return y_t[half * HEAD_DIM:(half + 1) * HEAD_DIM, :]

    def head_lanes(y, half):
        return y if half == 0 else pltpu.roll(y, HEAD_DIM, 1)

    d = p_ref[:, OFF_D:OFF_D + LANES] + bf_ref[...]
    logf = jnp.minimum(d, 0.0) - jnp.log(1.0 + jnp.exp(-jnp.abs(d)))
    tri = jnp.where(_iota2((ts, ts), 1) <= _iota2((ts, ts), 0), 1.0, 0.0).astype(bf16)
    lh, lm, ll = _split3(logf)
    c = carry_sc[0:1, :] + (_dot(tri, lh) + _dot(tri, lm) + _dot(tri, ll))
    carry_sc[...] = jnp.broadcast_to(c[ts - 1:ts, :], carry_sc.shape)
    c2 = c * LOG2E
    pieces = [p.astype(f32) for p in _split3(c2)]
    pieces_t = [p.astype(f32) for p in _split3(c2.T)]
    row64 = _iota2((HEAD_DIM, ts), 0)

    def decay_rows(h):
        r = FF_LANE + h
        hi, mid, lo = (jnp.broadcast_to(p[r:r + 1, :], (HEAD_DIM, ts)) for p in pieces_t)
        return jnp.where(row64 == 0, hi, jnp.where(row64 == 1, mid, jnp.where(row64 == 2, lo,
                         jnp.where(row64 < 6, 1.0, 0.0))))

    def decay_lanes(h):
        r = FF_LANE + h
        hi, mid, lo = (jnp.broadcast_to(p[:, r:r + 1], (ts, LANES)) for p in pieces)
        return jnp.where(lane == HEAD_DIM + 3, -hi, jnp.where(lane == HEAD_DIM + 4, -mid,
                         jnp.where(lane == HEAD_DIM + 5, -lo, jnp.where(lane < HEAD_DIM + 3, 1.0, 0.0))))

    moba_code = jnp.where((lane - HEAD_DIM) == _div(pos, MOBA_BLOCK), 1.0, 0.0)
    for c_i in range(3):
        yq_t = normed(c_i, True, Q_SCALE).T
        yk = normed(3 + c_i, True)
        for half in range(2):
            h = 2 * c_i + half
            mq_ref[h] = head_rows(yq_t, half).astype(bf16)
            mk_ref[h] = jnp.where(low_lanes, head_lanes(yk, half), moba_code).astype(bf16)
    for c_i in range(2):
        y_t = normed(6 + c_i, True, Q_SCALE).T
        for half in range(2):
            h = 2 * c_i + half
            yh = head_rows(y_t, half).astype(bf16)
            for qb in range(ts // Q_BLOCK):
                nq_ref[:, (qb * NSA_HEADS + h) * Q_BLOCK:(qb * NSA_HEADS + h + 1) * Q_BLOCK] = (
                    yh[:, qb * Q_BLOCK:(qb + 1) * Q_BLOCK])
    y = normed(8, True)
    slc_code = jnp.where((lane - HEAD_DIM) == _mod(_div(pos, NSA_SLC_BLOCK), CODE_BLOCKS), 1.0, 0.0)
    ksl_ref[...] = jnp.where(low_lanes, y, slc_code).astype(bf16)
    kw_ref[...] = jnp.where(low_lanes, head_lanes(y, 1), 0.0).astype(bf16)
    for c_i in range(3):
        yq_t = normed(9 + c_i, False, Q_SCALE).T
        yk = normed(12 + c_i, False)
        for half in range(2):
            h = 2 * c_i + half
            fq_ref[h] = jnp.concatenate([head_rows(yq_t, half), decay_rows(h)], axis=0).astype(bf16)
            fk_ref[h] = jnp.where(low_lanes, head_lanes(yk, half), decay_lanes(h)).astype(bf16)
    ones_rows = jnp.where(_iota2((V_ROWS - HEAD_DIM, ts), 0) == 0, 1.0, 0.0)

    def value_rows(x_t, half):
        return jnp.concatenate([head_rows(x_t, half), ones_rows], axis=0).astype(bf16)

    for c_i in range(3):
        xm_t = p_ref[:, OFF_B + c_i * LANES:OFF_B + (c_i + 1) * LANES].T
        xf_t = p_ref[:, OFF_B + MOBA_W + c_i * LANES:OFF_B + MOBA_W + (c_i + 1) * LANES].T
        for half in range(2):
            mv_ref[2 * c_i + half] = value_rows(xm_t, half)
            fv_ref[2 * c_i + half] = value_rows(xf_t, half)
    xs_t = p_ref[:, OFF_B + 2 * MOBA_W:OFF_B + 2 * MOBA_W + LANES].T
    vsl_ref[...] = value_rows(xs_t, 0)
    vw_ref[...] = value_rows(xs_t, 1)
    for c_i in range(SEG_C // LANES):
        z = p_ref[:, OFF_C + c_i * LANES:OFF_C + (c_i + 1) * LANES]
        zs = z * (1.0 / (1.0 + jnp.exp(-z)))
        for half in range(2):
            zs_ref[2 * c_i + half] = head_lanes(zs, half)[:, :HEAD_DIM].astype(bf16)


def prep(proj, cos_t, sin_t, gain_row, bf_row, ts=256):
    S = proj.shape[0]
    head64 = lambda n: jax.ShapeDtypeStruct((n, S, HEAD_DIM), bf16)
    head128 = lambda n: jax.ShapeDtypeStruct((n, S, AUG), bf16)
    spec_h = lambda n, w: pl.BlockSpec((n, ts, w), lambda i: (0, i, 0))
    spec_r = lambda w: pl.BlockSpec((ts, w), lambda i: (i, 0))
    head_t = lambda n, r: jax.ShapeDtypeStruct((n, r, S), bf16)
    spec_ht = lambda n, r: pl.BlockSpec((n, r, ts), lambda i: (0, 0, i))
    spec_t = pl.BlockSpec((V_ROWS, ts), lambda i: (0, i))
    out_shape = (head_t(MOBA_HEADS, HEAD_DIM), head128(MOBA_HEADS),
                 jax.ShapeDtypeStruct((HEAD_DIM, NSA_HEADS * S), bf16),
                 jax.ShapeDtypeStruct((S, AUG), bf16), jax.ShapeDtypeStruct((S, AUG), bf16),
                 head_t(FOX_HEADS, AUG), head128(FOX_HEADS),
                 head_t(MOBA_HEADS, V_ROWS), head_t(FOX_HEADS, V_ROWS),
                 jax.ShapeDtypeStruct((V_ROWS, S), bf16), jax.ShapeDtypeStruct((V_ROWS, S), bf16),
                 head64(N_ZHEADS))
    out_specs = (spec_ht(MOBA_HEADS, HEAD_DIM), spec_h(MOBA_HEADS, AUG),
                 pl.BlockSpec((HEAD_DIM, NSA_HEADS * ts), lambda i: (0, i)),
                 spec_r(AUG), spec_r(AUG),
                 spec_ht(FOX_HEADS, AUG), spec_h(FOX_HEADS, AUG),
                 spec_ht(MOBA_HEADS, V_ROWS), spec_ht(FOX_HEADS, V_ROWS),
                 spec_t, spec_t,
                 spec_h(N_ZHEADS, HEAD_DIM))
    return pl.pallas_call(
        functools.partial(_prep_kernel, ts=ts),
        grid=(S // ts,),
        in_specs=[pl.BlockSpec((ts, MAIN_COLS), lambda i: (i, 0)),
                  spec_r(LANES), spec_r(LANES),
                  pl.BlockSpec((1, SEG_A), lambda i: (0, 0)),
                  pl.BlockSpec((1, LANES), lambda i: (0, 0))],
        out_specs=out_specs,
        out_shape=out_shape,
        scratch_shapes=[pltpu.VMEM((8, LANES), f32)],
        compiler_params=_cparams(("arbitrary",)),
        name="prep",
    )(proj, cos_t, sin_t, gain_row, bf_row)


def _kmean_kernel(k_ref, o_ref, *, rows):
    n = rows // MOBA_BLOCK
    avg = jnp.where(_div(_iota2((n, rows), 1), MOBA_BLOCK) == _iota2((n, rows), 0),
                    1.0 / MOBA_BLOCK, 0.0).astype(bf16)
    o_ref[0] = _dot(avg, k_ref[0])[:, :HEAD_DIM]


def moba_kmean(mk_aug):
    H, S, _ = mk_aug.shape
    rows = 8 * MOBA_BLOCK
    return pl.pallas_call(
        functools.partial(_kmean_kernel, rows=rows),
        grid=(H, S // rows),
        in_specs=[pl.BlockSpec((1, rows, AUG), lambda h, i: (h, i, 0))],
        out_specs=pl.BlockSpec((1, 8, HEAD_DIM), lambda h, i: (h, i, 0)),
        out_shape=jax.ShapeDtypeStruct((H, CODE_BLOCKS, HEAD_DIM), f32),
        compiler_params=_cparams(("arbitrary", "arbitrary")),
        name="moba_kmean",
    )(mk_aug)


def _top_select(scores, idx, n_pick, floor):
    big = jnp.int32(2 ** 30)
    scores = list(scores)
    for _ in range(n_pick):
        for j, score in enumerate(scores):
            m = jnp.max(score, axis=0, keepdims=True)
            first = jnp.min(jnp.where(score == m, idx, big), axis=0, keepdims=True)
            first = jnp.where(m > floor, first, big)
            scores[j] = jnp.where(idx == first, -jnp.inf, score)
    return [s == -jnp.inf for s in scores]


def _moba_select_kernel(q_ref, km_ref, o_ref, *, tq):
    i = pl.program_id(1)
    q = q_ref[0]
    km = km_ref[0]
    km_hi = km.astype(bf16)
    km_lo = (km - km_hi.astype(f32)).astype(bf16)
    gate = _dot(km_hi, q) + _dot(km_lo, q)
    blk = _iota2((CODE_BLOCKS, tq), 0)
    cur = _div(_iota2((CODE_BLOCKS, tq), 1) + i * tq, MOBA_BLOCK)
    score = jnp.where(blk < cur, gate, NEG_INF)
    sel, = _top_select([score], blk, MOBA_TOPK, NEG_INF)
    bias = jnp.where(sel | (blk == cur), 0.0, NEG_INF).astype(bf16)
    o_ref[0, 0, 0:HEAD_DIM, :] = q
    o_ref[0, 0, HEAD_DIM:AUG, :] = bias


def moba_select(mq, kmean, tq=512):
    H, _, S = mq.shape
    return pl.pallas_call(
        functools.partial(_moba_select_kernel, tq=tq),
        grid=(H, S // tq),
        in_specs=[pl.BlockSpec((1, HEAD_DIM, tq), lambda h, i: (h, 0, i)),
                  pl.BlockSpec((1, CODE_BLOCKS, HEAD_DIM), lambda h, i: (h, 0, 0))],
        out_specs=pl.BlockSpec((1, 1, AUG, tq), lambda h, i: (h, 0, 0, i)),
        out_shape=jax.ShapeDtypeStruct((H, 1, AUG, S), bf16),
        compiler_params=_cparams(("arbitrary", "arbitrary")),
        name="moba_select",
    )(mq, kmean)


def _cmp_kernel(x_ref, pe_ref, w1_ref, w2_ref, gain_ref, cos_ref, sin_ref, kc_ref, vc_ref, *, ncp):
    top = jnp.zeros((ncp, 2 * NSA_CMP_HIDDEN), f32)
    nxt = jnp.zeros((ncp, 2 * NSA_CMP_HIDDEN), f32)
    for j in range(NSA_CMP_STRIDE):
        xj = x_ref[pl.ds(j, ncp, stride=NSA_CMP_STRIDE), :]
        top = top + _dot((xj + pe_ref[j:j + 1, :]).astype(bf16), w1_ref[j])
        nxt = nxt + _dot((xj + pe_ref[NSA_CMP_STRIDE + j:NSA_CMP_STRIDE + j + 1, :]).astype(bf16),
                         w1_ref[NSA_CMP_STRIDE + j])
    hid = top + pltpu.roll(nxt, ncp - 1, 0)
    act = (hid * (1.0 / (1.0 + jnp.exp(-hid)))).astype(bf16)
    k = _dot(act[:, :NSA_CMP_HIDDEN], w2_ref[0])
    v = _dot(act[:, NSA_CMP_HIDDEN:], w2_ref[1])
    ms = jnp.sum(k * k, axis=-1, keepdims=True) * (1.0 / HEAD_DIM)
    y = k * lax.rsqrt(ms + RMS_EPS) * gain_ref[...]
    lane = _iota2((ncp, LANES), 1)
    up = pltpu.roll(y, LANES - ROPE_DIM // 2, 1)
    dn = pltpu.roll(y, ROPE_DIM // 2, 1)
    y = y * cos_ref[...] + jnp.where(lane < ROPE_DIM // 2, up, dn) * sin_ref[...]
    kc_ref[...] = y[:, :HEAD_DIM].astype(bf16)
    vc_ref[...] = _dot_nt(_place_t(V_ROWS, LANES, 0, 0), v.astype(bf16)).astype(bf16)


def nsa_compress(proj, pe, w1, w2, gain, cos_c, sin_c):
    S = proj.shape[0]
    ncp = S // NSA_CMP_STRIDE
    full = lambda a: pl.BlockSpec(a.shape, lambda i: (0,) * a.ndim)
    return pl.pallas_call(
        functools.partial(_cmp_kernel, ncp=ncp),
        grid=(1,),
        in_specs=[pl.BlockSpec((S, LANES), lambda i: (0, KV_CMP_OFF // LANES)),
                  full(pe), full(w1), full(w2), full(gain), full(cos_c), full(sin_c)],
        out_specs=(pl.BlockSpec((ncp, HEAD_DIM), lambda i: (0, 0)), pl.BlockSpec((V_ROWS, ncp), lambda i: (0, 0))),
        out_shape=(jax.ShapeDtypeStruct((ncp, HEAD_DIM), bf16), jax.ShapeDtypeStruct((V_ROWS, ncp), bf16)),
        compiler_params=_cparams(("arbitrary",)),
        name="nsa_compress",
    )(proj, pe, w1, w2, gain, cos_c, sin_c)


def _nsa_cmp_select_kernel(q_ref, kc_ref, vc_ref, oc_ref, qa_ref, *, ncp, nsp, n_win, n_qb):
    i = pl.program_id(0)
    cols = NSA_HEADS * Q_BLOCK
    kc = kc_ref[...]
    vc = vc_ref[...]
    c0 = _iota2((nsp, ncp), 1) * NSA_CMP_STRIDE
    b0 = _iota2((nsp, ncp), 0) * NSA_SLC_BLOCK
    overlap = jnp.where((c0 <= b0 + (NSA_SLC_BLOCK - 1)) & (c0 + (NSA_CMP_LEN - 1) >= b0), 1.0, 0.0).astype(bf16)
    kend = _iota2((ncp, cols), 0) * NSA_CMP_STRIDE + (NSA_CMP_LEN - 1)
    col_pos = _mod(_iota2((ncp, cols), 1), Q_BLOCK)
    blk = _iota2((nsp, Q_BLOCK), 0)
    pad_rows = jnp.zeros((LANES - V_ROWS, cols), f32)

    def scores(b):
        return _dot(kc, q_ref[:, b * cols:(b + 1) * cols])

    def probs(b, s):
        ok = kend <= (i * n_qb + b) * Q_BLOCK + col_pos
        sm = jnp.where(ok, s, NEG_INF)
        e = jnp.exp2(sm - jnp.max(sm, axis=0, keepdims=True))
        return jnp.where(ok, e * (1.0 / jnp.sum(e, axis=0, keepdims=True)), 0.0)

    def cur(b):
        return _div(_iota2((nsp, Q_BLOCK), 1) + (i * n_qb + b) * Q_BLOCK, NSA_SLC_BLOCK)

    def forced(b):
        return (blk == 0) | (blk == cur(b)) | (blk == cur(b) - 1)

    def importance(b, p):
        o = _dot(vc, p.astype(bf16))
        oc_ref[b * cols:(b + 1) * cols, :] = jnp.concatenate([o, pad_rows], axis=0).T[:, :HEAD_DIM].astype(bf16)
        psum = (p[:, 0:Q_BLOCK] + p[:, Q_BLOCK:2 * Q_BLOCK]
                + p[:, 2 * Q_BLOCK:3 * Q_BLOCK] + p[:, 3 * Q_BLOCK:4 * Q_BLOCK])
        ph, pm, plo = _split3(psum)
        imp = _dot(overlap, ph) + _dot(overlap, pm) + _dot(overlap, plo)
        return jnp.where((blk <= cur(b)) & jnp.logical_not(forced(b)), imp, NEG_INF)

    block_scores = []
    s_next = scores(0)
    for b in range(n_qb):
        s = s_next
        if b + 1 < n_qb:
            s_next = scores(b + 1)
        block_scores.append(importance(b, probs(b, s)))
    n_free = NSA_SLC_TOPN - 3
    for b, sel in enumerate(_top_select(block_scores, blk, n_free, NEG_INF)):
        bias = jnp.where(sel | forced(b), 0.0, NEG_INF).astype(bf16)
        for w in range(n_win):
            bw = bias[w * CODE_BLOCKS:(w + 1) * CODE_BLOCKS, :]
            qa_ref[0, w, 0:HEAD_DIM, b * cols:(b + 1) * cols] = q_ref[:, b * cols:(b + 1) * cols]
            qa_ref[0, w, HEAD_DIM:AUG, b * cols:(b + 1) * cols] = jnp.concatenate([bw] * NSA_HEADS, axis=1)


def nsa_cmp_select(nq, kc, vc, S, n_qb=4):
    ncp = kc.shape[0]
    ns = S // NSA_SLC_BLOCK
    nsp = max(LANES, ns)
    n_win = max(1, ns // CODE_BLOCKS)
    cols = n_qb * NSA_HEADS * Q_BLOCK
    return pl.pallas_call(
        functools.partial(_nsa_cmp_select_kernel, ncp=ncp, nsp=nsp, n_win=n_win, n_qb=n_qb),
        grid=(S // (n_qb * Q_BLOCK),),
        in_specs=[pl.BlockSpec((HEAD_DIM, cols), lambda i: (0, i)),
                  pl.BlockSpec((ncp, HEAD_DIM), lambda i: (0, 0)),
                  pl.BlockSpec((V_ROWS, ncp), lambda i: (0, 0))],
        out_specs=(pl.BlockSpec((cols, HEAD_DIM), lambda i: (i, 0)),
                   pl.BlockSpec((1, n_win, AUG, cols), lambda i: (0, 0, 0, i))),
        out_shape=(jax.ShapeDtypeStruct((NSA_HEADS * S, HEAD_DIM), bf16),
                   jax.ShapeDtypeStruct((1, n_win, AUG, NSA_HEADS * S), bf16)),
        compiler_params=_cparams(("arbitrary",)),
        name="nsa_cmp_select",
    )(nq, kc, vc)


def _flash_kernel(qi_t, kb_t, win_t, first_t, last_t, var_t, q_ref, k_ref, v_ref, o_ref, m_sc, acc_sc,
                  *, n_heads, tq, qw, tq_pos, rep_cols, pos_block, window, variants):
    p_id = pl.program_id(0)
    qi = qi_t[p_id]
    kb = kb_t[p_id]

    @pl.when(first_t[p_id] == 1)
    def _():
        m_sc[...] = jnp.full(m_sc.shape, M_FLOOR, f32)
        acc_sc[...] = jnp.zeros_like(acc_sc)

    def tile(modes):
        chains = [(g, slice(c * qw, (c + 1) * qw), slice(kt * FLASH_TK, (kt + 1) * FLASH_TK))
                  for kt in range(FLASH_SUB) if modes[kt] for g in range(n_heads) for c in range(tq // qw)]
        if 2 in modes:
            rel = _mod(_iota2((FLASH_TK, qw), 1), pos_block) - _iota2((FLASH_TK, qw), 0)

        def scores(t):
            g, cols, keys = chains[t]
            s = _dot(k_ref[g, keys, :], q_ref[g, 0, :, cols])
            if modes[keys.start // FLASH_TK] == 1:
                return s
            base = (qi * tq_pos + (cols.start // rep_cols) * pos_block + cols.start % pos_block
                    - kb * (FLASH_SUB * FLASH_TK) - keys.start)
            delta = rel + base
            ok = (delta >= 0) if window is None else (delta.astype(jnp.uint32) < jnp.uint32(window))
            return jnp.where(ok, s, NEG_INF)

        def softmax(t, s):
            g, cols, _ = chains[t]
            m_prev = m_sc[g, 0:1, cols]
            m_new = jnp.maximum(m_prev, jnp.max(s, axis=0, keepdims=True))
            m_sc[g, :, cols] = jnp.broadcast_to(m_new, (m_sc.shape[1], qw))
            return jnp.exp2(s - m_new).astype(bf16), jnp.exp2(m_prev - m_new)

        def accumulate(t, p, alpha):
            g, cols, keys = chains[t]
            acc_sc[g, :, cols] = alpha * acc_sc[g, :, cols] + _dot(v_ref[g, :, keys], p)

        n = len(chains)
        s_next, staged = scores(0), None
        for t in range(n + 1):
            s = s_next
            if t + 1 < n:
                s_next = scores(t + 1)
            if staged is not None:
                accumulate(t - 1, *staged)
            staged = softmax(t, s) if t < n else None

    for vid, modes in enumerate(variants):
        pl.when(var_t[p_id] == vid)(functools.partial(tile, modes))

    @pl.when(last_t[p_id] == 1)
    def _():
        for g in range(n_heads):
            acc = acc_sc[g]
            l = acc[HEAD_DIM:HEAD_DIM + 1, :]
            inv = jnp.where(l > 0.0, 1.0 / l, 0.0)
            o = jnp.concatenate([acc * inv, jnp.zeros((LANES - V_ROWS, tq), f32)], axis=0)
            o_ref[g] = o.T[:, :HEAD_DIM].astype(o_ref.dtype)


def _step_tables(n_q, key_tiles, win_keys):
    variants, cols = [], ([], [], [], [], [], [])
    for a in range(n_q):
        tiles = key_tiles(a)
        groups = sorted({t // FLASH_SUB for t in tiles})
        for kb in groups:
            modes = tuple(tiles.get(kb * FLASH_SUB + j, 0) for j in range(FLASH_SUB))
            if modes not in variants:
                variants.append(modes)
            row = (a, kb, (kb * FLASH_SUB * FLASH_TK) // win_keys, int(kb == groups[0]), int(kb == groups[-1]),
                   variants.index(modes))
            for c, v in zip(cols, row):
                c.append(v)
    return tuple(jnp.asarray(np.asarray(c, np.int32)) for c in cols), tuple(variants)


def flash(q, k, v, steps, *, tq, pos_block, window, name):
    tables, variants = steps
    G, _, _, Sq = q.shape
    tq_pos = FLASH_TQ
    rep_cols = (tq // tq_pos) * pos_block
    assert rep_cols % FLASH_QW == 0 and (pos_block % FLASH_QW == 0 or FLASH_QW % pos_block == 0)
    tk = FLASH_SUB * FLASH_TK
    grid_spec = pltpu.PrefetchScalarGridSpec(
        num_scalar_prefetch=6,
        grid=(tables[0].shape[0],),
        in_specs=[pl.BlockSpec((G, 1, AUG, tq), lambda p, qi, kb, wi, fi, la, va: (0, wi[p], 0, qi[p])),
                  pl.BlockSpec((G, tk, AUG), lambda p, qi, kb, wi, fi, la, va: (0, kb[p], 0)),
                  pl.BlockSpec((G, V_ROWS, tk), lambda p, qi, kb, wi, fi, la, va: (0, 0, kb[p]))],
        out_specs=pl.BlockSpec((G, tq, HEAD_DIM), lambda p, qi, kb, wi, fi, la, va: (0, qi[p], 0)),
        scratch_shapes=[pltpu.VMEM((G, 8, tq), f32), pltpu.VMEM((G, V_ROWS, tq), f32)],
    )
    return pl.pallas_call(
        functools.partial(_flash_kernel, n_heads=G, tq=tq, qw=FLASH_QW, tq_pos=tq_pos, rep_cols=rep_cols,
                          pos_block=pos_block, window=window, variants=variants),
        grid_spec=grid_spec,
        out_shape=jax.ShapeDtypeStruct((G, Sq, HEAD_DIM), bf16),
        compiler_params=_cparams(("arbitrary",)),
        name=name,
    )(*tables, q, k, v)


def _out_kernel(x_ref, gl0_ref, gl1_ref, gl2_ref, sd_ref, bg_ref, om_ref, of_ref, oc_ref, os_ref, ow_ref, zs_ref,
                wm_ref, wn_ref, wf_ref, wo_ref, o_ref, *, tm):
    def up(o_of_head, n_heads, z_off, w_ref):
        y = jnp.zeros((tm, D_MODEL), f32)
        for h in range(n_heads):
            g = (o_of_head(h) * zs_ref[z_off + h].astype(f32)).astype(bf16)
            y = y + _dot(g, w_ref[h])
        return y

    sg = 1.0 / (1.0 + jnp.exp(-sd_ref[...]))

    def nsa_head(h):
        def rows(ref):
            starts = [(qb * NSA_HEADS + h) * Q_BLOCK for qb in range(tm // Q_BLOCK)]
            return jnp.concatenate([ref[r0:r0 + Q_BLOCK, :] for r0 in starts], axis=0).astype(f32)

        return (sg[:, 3 * h:3 * h + 1] * rows(oc_ref) + sg[:, 3 * h + 1:3 * h + 2] * rows(os_ref)
                + sg[:, 3 * h + 2:3 * h + 3] * rows(ow_ref))

    y_m = up(lambda h: om_ref[h].astype(f32), MOBA_HEADS, 0, wm_ref)
    y_n = up(nsa_head, NSA_HEADS, MOBA_HEADS, wn_ref)
    y_f = up(lambda h: of_ref[h].astype(f32), FOX_HEADS, MOBA_HEADS + NSA_HEADS, wf_ref)
    merged = jnp.zeros((tm, D_MODEL), f32)
    for b, (y, gl_ref) in enumerate(((y_m, gl0_ref), (y_n, gl1_ref), (y_f, gl2_ref))):
        cols = slice(b * D_MODEL, (b + 1) * D_MODEL)
        merged = merged + y * (1.0 / (1.0 + jnp.exp(-(gl_ref[...] + bg_ref[:, cols]))))
    o_ref[...] = x_ref[...] + _dot(merged.astype(bf16), wo_ref[...])


def out_proj(x, proj, b_gate, o_m, o_f, o_c, o_s, o_w, zs, w_m, w_n, w_f, w_o):
    S = x.shape[0]
    tm = 2 * Q_BLOCK
    rows = NSA_HEADS * tm
    full = lambda a: pl.BlockSpec(a.shape, lambda i: (0,) * a.ndim)
    gl_spec = lambda b: pl.BlockSpec((tm, D_MODEL), lambda i: (i, MAIN_COLS // D_MODEL + b))
    return pl.pallas_call(
        functools.partial(_out_kernel, tm=tm),
        grid=(S // tm,),
        in_specs=[pl.BlockSpec((tm, D_MODEL), lambda i: (i, 0)),
                  gl_spec(0), gl_spec(1), gl_spec(2),
                  pl.BlockSpec((tm, LANES), lambda i: (i, OFF_D // LANES)),
                  full(b_gate),
                  pl.BlockSpec((MOBA_HEADS, tm, HEAD_DIM), lambda i: (0, i, 0)),
                  pl.BlockSpec((FOX_HEADS, tm, HEAD_DIM), lambda i: (0, i, 0)),
                  pl.BlockSpec((rows, HEAD_DIM), lambda i: (i, 0)),
                  pl.BlockSpec((rows, HEAD_DIM), lambda i: (i, 0)),
                  pl.BlockSpec((rows, HEAD_DIM), lambda i: (i, 0)),
                  pl.BlockSpec((N_ZHEADS, tm, HEAD_DIM), lambda i: (0, i, 0)),
                  full(w_m), full(w_n), full(w_f), full(w_o)],
        out_specs=pl.BlockSpec((tm, D_MODEL), lambda i: (i, 0)),
        out_shape=jax.ShapeDtypeStruct((S, D_MODEL), f32),
        compiler_params=_cparams(("arbitrary",)),
        name="out_proj",
    )(x, proj, proj, proj, proj, b_gate, o_m, o_f, o_c, o_s, o_w, zs, w_m, w_n, w_f, w_o)


def _rope_tables(pos, rows):
    half = ROPE_DIM // 2
    inv = ROPE_THETA ** (-jnp.arange(0, ROPE_DIM, 2, dtype=f32) / ROPE_DIM)
    d = np.arange(LANES) % HEAD_DIM
    rotated = d < ROPE_DIM
    inv_row = jnp.where(rotated, inv[d % half], 0.0)
    sign_row = np.where(d < half, -1.0, 1.0).astype(np.float32)
    ang = pos.astype(f32)[:, None] * inv_row[None, :]
    pad = ((0, rows - pos.shape[0]), (0, 0))
    return jnp.pad(jnp.cos(ang), pad), jnp.pad(jnp.sin(ang) * sign_row[None, :], pad)


def _layer(x, norm_g, w_main, b_f, b_gate, moba_qk_g, nsa_q_g, nsa_k_g, fox_qk_g,
           cmp_pe, cmp_w1, cmp_w2, w_up_moba, w_up_nsa, w_up_fox, w_out, tables, rope):
    S = x.shape[0]
    proj = rms_matmul(x, norm_g.reshape(1, D_MODEL), w_main)

    gain_row = jnp.concatenate([jnp.tile(moba_qk_g[0], MOBA_HEADS), jnp.tile(moba_qk_g[1], MOBA_HEADS),
                                jnp.tile(nsa_q_g, NSA_HEADS), nsa_k_g[1], nsa_k_g[2],
                                jnp.tile(fox_qk_g[0], FOX_HEADS), jnp.tile(fox_qk_g[1], FOX_HEADS)]).reshape(1, SEG_A)
    bf_row = jnp.zeros((1, LANES), f32).at[0, FF_LANE:FF_LANE + FOX_HEADS].set(b_f)
    (mq, mk, nq, ksl, kw, fq, fk, mv, fv, vsl, vw, zs) = prep(proj, rope[0], rope[1], gain_row, bf_row)

    q_moba = moba_select(mq, moba_kmean(mk))
    o_m = flash(q_moba, mk, mv, tables["causal"], tq=FLASH_TQ, pos_block=FLASH_TQ, window=None, name="flash_moba")

    pe = jnp.concatenate([cmp_pe[0], cmp_pe[1]], axis=1)
    w1 = cmp_w1.astype(bf16).reshape(2, NSA_CMP_LEN, HEAD_DIM, NSA_CMP_HIDDEN)
    zero = jnp.zeros_like(w1[0])
    w1 = jnp.concatenate([jnp.concatenate([w1[0], zero], axis=2),
                          jnp.concatenate([zero, w1[1]], axis=2)], axis=1)
    w2p = jnp.pad(cmp_w2, ((0, 0), (0, 0), (0, LANES - HEAD_DIM))).astype(bf16)
    gain_c = jnp.pad(nsa_k_g[0], (0, LANES - HEAD_DIM)).reshape(1, LANES)
    kc, vc = nsa_compress(proj, pe, w1, w2p, gain_c, rope[2], rope[3])
    o_c, q_slc = nsa_cmp_select(nq, kc, vc, S)
    nsa_cols = NSA_HEADS * FLASH_TQ
    o_s = flash(q_slc, ksl[None], vsl[None], tables["slc"], tq=nsa_cols,
                pos_block=Q_BLOCK, window=None, name="flash_nsa_slc")[0]
    o_w = flash(q_slc, kw[None], vw[None], tables["win"], tq=nsa_cols,
                pos_block=Q_BLOCK, window=NSA_WINDOW, name="flash_nsa_win")[0]

    o_f = flash(fq[:, None], fk, fv, tables["causal"], tq=FLASH_TQ, pos_block=FLASH_TQ, window=None, name="flash_fox")
    return out_proj(x, proj, b_gate.reshape(1, N_BRANCH * D_MODEL), o_m, o_f, o_c, o_s, o_w, zs,
                    w_up_moba.astype(bf16).reshape(MOBA_HEADS, HEAD_DIM, D_MODEL),
                    w_up_nsa.astype(bf16).reshape(NSA_HEADS, HEAD_DIM, D_MODEL),
                    w_up_fox.astype(bf16).reshape(FOX_HEADS, HEAD_DIM, D_MODEL),
                    w_out.astype(bf16))


def kernel(x, norm_g, w_in, b_f, b_gate, moba_qk_g, nsa_q_g, nsa_k_g, fox_qk_g, cmp_pe, cmp_w1, cmp_w2,
           w_up_moba, w_up_nsa, w_up_fox, w_out):
    B, S, _ = x.shape
    assert B == 1 and S % 2048 == 0 and S // MOBA_BLOCK <= CODE_BLOCKS
    depth = norm_g.shape[0]
    win_keys = CODE_BLOCKS * NSA_SLC_BLOCK
    n_q = S // FLASH_TQ
    causal = lambda a: {t: (2 if (t + 1) * FLASH_TK > a * FLASH_TQ else 1)
                        for t in range((a + 1) * FLASH_TQ // FLASH_TK)}
    band = lambda a: {t: 2 for t in range(max(0, a * FLASH_TQ - NSA_WINDOW + 1) // FLASH_TK,
                                          (a + 1) * FLASH_TQ // FLASH_TK)}
    tables = {"causal": _step_tables(n_q, causal, S), "slc": _step_tables(n_q, causal, win_keys),
              "win": _step_tables(n_q, band, S)}
    ncp = S // NSA_CMP_STRIDE
    cos_t, sin_t = _rope_tables(jnp.arange(S), S)
    cmp_end = jnp.arange(ncp - 1) * NSA_CMP_STRIDE + (NSA_CMP_LEN - 1)
    cos_c, sin_c = _rope_tables(cmp_end, ncp)
    rope = (cos_t, sin_t, cos_c, sin_c)
    h = x[0]
    for l in range(depth):
        h = _layer(h, norm_g[l], repack_w_in(w_in, l), b_f[l], b_gate[l], moba_qk_g[l], nsa_q_g[l], nsa_k_g[l], fox_qk_g[l],
                   cmp_pe[l], cmp_w1[l], cmp_w2[l], w_up_moba[l], w_up_nsa[l], w_up_fox[l], w_out[l], tables, rope)
    return h[None]
```

```python
import functools

import numpy as np
import jax
import jax.numpy as jnp
from jax import lax
from jax.experimental import pallas as pl
from jax.experimental.pallas import tpu as pltpu

D_MODEL = 1024
HEAD_DIM = 64
ROPE_DIM = HEAD_DIM // 4
ROPE_THETA = 500000.0
RMS_EPS = 1e-6
NEG_INF = -1e30
M_FLOOR = -1e20

MOBA_HEADS = 6
MOBA_BLOCK = 256
MOBA_TOPK = 3
NSA_HEADS = 4
NSA_CMP_LEN = 32
NSA_CMP_STRIDE = 16
NSA_CMP_HIDDEN = 4 * HEAD_DIM
NSA_SLC_BLOCK = 64
NSA_SLC_TOPN = 16
NSA_WINDOW = 512
FOX_HEADS = 6
N_BRANCH = 3
MOBA_W = MOBA_HEADS * HEAD_DIM
NSA_W = NSA_HEADS * HEAD_DIM
FOX_W = FOX_HEADS * HEAD_DIM
IN_SPLITS = (MOBA_W,) * 4 + (NSA_W,) + (HEAD_DIM,) * 6 + (3 * NSA_HEADS, NSA_W) + (FOX_W,) * 3 + (FOX_HEADS, FOX_W, N_BRANCH * D_MODEL)
ATTN_SCALE = HEAD_DIM ** -0.5

LANES = 128
AUG = 2 * HEAD_DIM
CODE_BLOCKS = AUG - HEAD_DIM
Q_BLOCK = 128

SEG_A = 2 * MOBA_W + NSA_W + 2 * HEAD_DIM + 2 * FOX_W
SEG_B = 2 * MOBA_W + 4 * HEAD_DIM
SEG_C = MOBA_W + NSA_W + FOX_W
SEG_D = LANES
OFF_B = SEG_A
OFF_C = OFF_B + SEG_B
OFF_D = OFF_C + SEG_C
MAIN_COLS = OFF_D + SEG_D
FF_LANE = 3 * NSA_HEADS
KV_CMP_OFF = OFF_B + 2 * MOBA_W + 2 * HEAD_DIM
ALL_COLS = MAIN_COLS + N_BRANCH * D_MODEL

VMEM_LIMIT = 56 * 1024 * 1024
FLASH_QW = 512
FLASH_TQ = 512
FLASH_TK = 256
FLASH_SUB = 8
LOG2E = 1.4426950408889634
Q_SCALE = ATTN_SCALE * LOG2E
V_ROWS = 80

f32 = jnp.float32
bf16 = jnp.bfloat16


def _cparams(sem):
    return pltpu.CompilerParams(dimension_semantics=sem, vmem_limit_bytes=VMEM_LIMIT)


def _iota2(shape, dim):
    return lax.broadcasted_iota(jnp.int32, shape, dim)


def _div(x, d):
    return jnp.right_shift(x, int(d).bit_length() - 1)


def _mod(x, d):
    return jnp.bitwise_and(x, d - 1)


def _place(n_in, n_out, in_off, out_off, width=HEAD_DIM, val=1.0):
    r = _iota2((n_in, n_out), 0) - in_off
    c = _iota2((n_in, n_out), 1) - out_off
    hit = (r == c) & (r >= 0) & (r < width)
    return jnp.where(hit, val, 0.0).astype(bf16)


def _place_t(n_out, n_in, in_off, out_off, width=HEAD_DIM, val=1.0):
    r = _iota2((n_out, n_in), 0) - out_off
    c = _iota2((n_out, n_in), 1) - in_off
    hit = (r == c) & (r >= 0) & (r < width)
    return jnp.where(hit, val, 0.0).astype(bf16)


def _split3(x):
    hi = x.astype(bf16)
    r = x - hi.astype(f32)
    mid = r.astype(bf16)
    lo = (r - mid.astype(f32)).astype(bf16)
    return hi, mid, lo


def _heads_to_lanes(pieces):
    rows = len(pieces) * HEAD_DIM
    pad = [jnp.zeros((-rows % LANES, pieces[0].shape[1]), f32)] if rows % LANES else []
    return jnp.concatenate(list(pieces) + pad, axis=0).T[:, :rows]


def _dot(a, b):
    return jnp.dot(a, b, preferred_element_type=f32)


def _dot_nt(a, b):
    return lax.dot_general(a, b, (((1,), (1,)), ((), ())), preferred_element_type=f32)


def _w_in_plan():
    offs = np.concatenate([[0], np.cumsum(IN_SPLITS)])
    names = ("mq", "mk", "mv", "mz", "nq", "kc", "vc", "ksl", "vsl", "kw", "vw", "ng", "nz", "fq", "fk", "fv", "ff", "fz", "gl")
    start = {n: int(offs[j]) for j, n in enumerate(names)}
    width = {n: int(IN_SPLITS[j]) for j, n in enumerate(names)}
    order = ("mq", "mk", "nq", "ksl", "kw", "fq", "fk", "mv", "fv", "vsl", "vw", "kc", "vc", "mz", "nz", "fz", "ng", "ff")
    plan = [[] for _ in range(ALL_COLS // LANES)]
    new = 0
    for n in order + ("pad", "gl"):
        if n == "pad":
            new = MAIN_COLS
            continue
        src, left = start[n], width[n]
        while left > 0:
            w = min(left, LANES - new % LANES, LANES - src % LANES)
            plan[new // LANES].append((src // LANES, src % LANES, new % LANES, w))
            src, new, left = src + w, new + w, left - w
    assert new == ALL_COLS
    return plan


def _repack_kernel(w_ref, o_ref, *, plan, tr, n_cols):
    lane = _iota2((tr, LANES), 1)
    loaded = {}

    def source(a):
        if a not in loaded:
            x = w_ref[0, :, a * LANES:(a + 1) * LANES]
            if (a + 1) * LANES > n_cols:
                x = jnp.where(lane < n_cols - a * LANES, x, 0.0)
            loaded[a] = x.astype(bf16)
        return loaded[a]

    for b, pieces in enumerate(plan):
        acc = jnp.zeros((tr, LANES), f32)
        for a, lane_in, lane_out, w in pieces:
            acc = acc + _dot(source(a), _place(LANES, LANES, lane_in, lane_out, width=w))
        o_ref[:, b * LANES:(b + 1) * LANES] = acc.astype(bf16)


def repack_w_in(w_in, layer, tr=128):
    _, D, n_cols = w_in.shape
    padded = pl.cdiv(n_cols, LANES) * LANES
    return pl.pallas_call(
        functools.partial(_repack_kernel, plan=_w_in_plan(), tr=tr, n_cols=n_cols),
        grid=(D // tr,),
        in_specs=[pl.BlockSpec((1, tr, padded), lambda i: (layer, i, 0))],
        out_specs=pl.BlockSpec((tr, ALL_COLS), lambda i: (i, 0)),
        out_shape=jax.ShapeDtypeStruct((D, ALL_COLS), bf16),
        compiler_params=_cparams(("arbitrary",)),
        name="repack_w_in",
    )(w_in)


def _rms_matmul_kernel(x_ref, g_ref, w_ref, o_ref, h_sc):
    @pl.when(pl.program_id(1) == 0)
    def _():
        x = x_ref[...]
        ms = jnp.mean(x * x, axis=-1, keepdims=True)
        h_sc[...] = (x * lax.rsqrt(ms + RMS_EPS) * g_ref[...]).astype(bf16)

    o_ref[...] = _dot(h_sc[...], w_ref[...])


def rms_matmul(x, g, w, tm=1024, tn=1024):
    S, D = x.shape
    N = w.shape[1]
    return pl.pallas_call(
        _rms_matmul_kernel,
        grid=(S // tm, N // tn),
        in_specs=[pl.BlockSpec((tm, D), lambda i, j: (i, 0)),
                  pl.BlockSpec((1, D), lambda i, j: (0, 0)),
                  pl.BlockSpec((D, tn), lambda i, j: (0, j))],
        out_specs=pl.BlockSpec((tm, tn), lambda i, j: (i, j)),
        out_shape=jax.ShapeDtypeStruct((S, N), f32),
        scratch_shapes=[pltpu.VMEM((tm, D), bf16)],
        compiler_params=_cparams(("arbitrary", "arbitrary")),
        name="rms_matmul",
    )(x, g, w)


def _prep_kernel(p_ref, cos_ref, sin_ref, gain_ref, bf_ref,
                 mq_ref, mk_ref, nq_ref, ksl_ref, kw_ref, fq_ref, fk_ref,
                 mv_ref, fv_ref, vsl_ref, vw_ref, zs_ref, carry_sc, *, ts):
    i = pl.program_id(0)

    @pl.when(i == 0)
    def _():
        carry_sc[...] = jnp.zeros_like(carry_sc)

    lane = _iota2((ts, LANES), 1)
    pos = _iota2((ts, LANES), 0) + i * ts
    blockdiag = jnp.where(_div(_iota2((LANES, LANES), 0), HEAD_DIM) == _div(_iota2((LANES, LANES), 1), HEAD_DIM),
                          1.0, 0.0).astype(bf16)
    first_half = _mod(lane, HEAD_DIM) < (ROPE_DIM // 2)
    low_lanes = lane < HEAD_DIM
    cos = cos_ref[...]
    sin = sin_ref[...]

    def normed(c, rope, scale=None):
        x = p_ref[:, c * LANES:(c + 1) * LANES]
        x2 = x * x
        hi = x2.astype(bf16)
        lo = (x2 - hi.astype(f32)).astype(bf16)
        ss = _dot(hi, blockdiag) + _dot(lo, blockdiag)
        y = x * lax.rsqrt(ss * (1.0 / HEAD_DIM) + RMS_EPS) * gain_ref[:, c * LANES:(c + 1) * LANES]
        if rope:
            up = pltpu.roll(y, LANES - ROPE_DIM // 2, 1)
            dn = pltpu.roll(y, ROPE_DIM // 2, 1)
            y = y * cos + jnp.where(first_half, up, dn) * sin
        return y if scale is None else y * scale

    def head_rows(y_t, half):
        return y_t[half * HEAD_DIM:(half + 1) * HEAD_DIM, :]

    def head_lanes(y, half):
        return y if half == 0 else pltpu.roll(y, HEAD_DIM, 1)

    d = p_ref[:, OFF_D:OFF_D + LANES] + bf_ref[...]
    logf = jnp.minimum(d, 0.0) - jnp.log(1.0 + jnp.exp(-jnp.abs(d)))
    tri = jnp.where(_iota2((ts, ts), 1) <= _iota2((ts, ts), 0), 1.0, 0.0).astype(bf16)
    lh, lm, ll = _split3(logf)
    c = carry_sc[0:1, :] + (_dot(tri, lh) + _dot(tri, lm) + _dot(tri, ll))
    carry_sc[...] = jnp.broadcast_to(c[ts - 1:ts, :], carry_sc.shape)
    c2 = c * LOG2E
    pieces = [p.astype(f32) for p in _split3(c2)]
    pieces_t = [p.astype(f32) for p in _split3(c2.T)]
    row64 = _iota2((HEAD_DIM, ts), 0)

    def decay_rows(h):
        r = FF_LANE + h
        hi, mid, lo = (jnp.broadcast_to(p[r:r + 1, :], (HEAD_DIM, ts)) for p in pieces_t)
        return jnp.where(row64 == 0, hi, jnp.where(row64 == 1, mid, jnp.where(row64 == 2, lo,
                         jnp.where(row64 < 6, 1.0, 0.0))))

    def decay_lanes(h):
        r = FF_LANE + h
        hi, mid, lo = (jnp.broadcast_to(p[:, r:r + 1], (ts, LANES)) for p in pieces)
        return jnp.where(lane == HEAD_DIM + 3, -hi, jnp.where(lane == HEAD_DIM + 4, -mid,
                         jnp.where(lane == HEAD_DIM + 5, -lo, jnp.where(lane < HEAD_DIM + 3, 1.0, 0.0))))

    moba_code = jnp.where((lane - HEAD_DIM) == _div(pos, MOBA_BLOCK), 1.0, 0.0)
    for c_i in range(3):
        yq_t = normed(c_i, True, Q_SCALE).T
        yk = normed(3 + c_i, True)
        for half in range(2):
            h = 2 * c_i + half
            mq_ref[h] = head_rows(yq_t, half).astype(bf16)
            mk_ref[h] = jnp.where(low_lanes, head_lanes(yk, half), moba_code).astype(bf16)
    for c_i in range(2):
        y_t = normed(6 + c_i, True, Q_SCALE).T
        for half in range(2):
            h = 2 * c_i + half
            yh = head_rows(y_t, half).astype(bf16)
            for qb in range(ts // Q_BLOCK):
                nq_ref[:, (qb * NSA_HEADS + h) * Q_BLOCK:(qb * NSA_HEADS + h + 1) * Q_BLOCK] = (
                    yh[:, qb * Q_BLOCK:(qb + 1) * Q_BLOCK])
    y = normed(8, True)
    slc_code = jnp.where((lane - HEAD_DIM) == _mod(_div(pos, NSA_SLC_BLOCK), CODE_BLOCKS), 1.0, 0.0)
    ksl_ref[...] = jnp.where(low_lanes, y, slc_code).astype(bf16)
    kw_ref[...] = jnp.where(low_lanes, head_lanes(y, 1), 0.0).astype(bf16)
    for c_i in range(3):
        yq_t = normed(9 + c_i, False, Q_SCALE).T
        yk = normed(12 + c_i, False)
        for half in range(2):
            h = 2 * c_i + half
            fq_ref[h] = jnp.concatenate([head_rows(yq_t, half), decay_rows(h)], axis=0).astype(bf16)
            fk_ref[h] = jnp.where(low_lanes, head_lanes(yk, half), decay_lanes(h)).astype(bf16)
    ones_rows = jnp.where(_iota2((V_ROWS - HEAD_DIM, ts), 0) == 0, 1.0, 0.0)

    def value_rows(x_t, half):
        return jnp.concatenate([head_rows(x_t, half), ones_rows], axis=0).astype(bf16)

    for c_i in range(3):
        xm_t = p_ref[:, OFF_B + c_i * LANES:OFF_B + (c_i + 1) * LANES].T
        xf_t = p_ref[:, OFF_B + MOBA_W + c_i * LANES:OFF_B + MOBA_W + (c_i + 1) * LANES].T
        for half in range(2):
            mv_ref[2 * c_i + half] = value_rows(xm_t, half)
            fv_ref[2 * c_i + half] = value_rows(xf_t, half)
    xs_t = p_ref[:, OFF_B + 2 * MOBA_W:OFF_B + 2 * MOBA_W + LANES].T
    vsl_ref[...] = value_rows(xs_t, 0)
    vw_ref[...] = value_rows(xs_t, 1)
    z = p_ref[:, OFF_C:OFF_C + SEG_C]
    zs_ref[...] = (z * (1.0 / (1.0 + jnp.exp(-z)))).astype(bf16)


def prep(proj, cos_t, sin_t, gain_row, bf_row, ts=256):
    S = proj.shape[0]
    head128 = lambda n: jax.ShapeDtypeStruct((n, S, AUG), bf16)
    spec_h = lambda n, w: pl.BlockSpec((n, ts, w), lambda i: (0, i, 0))
    spec_r = lambda w: pl.BlockSpec((ts, w), lambda i: (i, 0))
    head_t = lambda n, r: jax.ShapeDtypeStruct((n, r, S), bf16)
    spec_ht = lambda n, r: pl.BlockSpec((n, r, ts), lambda i: (0, 0, i))
    spec_t = pl.BlockSpec((V_ROWS, ts), lambda i: (0, i))
    out_shape = (head_t(MOBA_HEADS, HEAD_DIM), head128(MOBA_HEADS),
                 jax.ShapeDtypeStruct((HEAD_DIM, NSA_HEADS * S), bf16),
                 jax.ShapeDtypeStruct((S, AUG), bf16), jax.ShapeDtypeStruct((S, AUG), bf16),
                 head_t(FOX_HEADS, AUG), head128(FOX_HEADS),
                 head_t(MOBA_HEADS, V_ROWS), head_t(FOX_HEADS, V_ROWS),
                 jax.ShapeDtypeStruct((V_ROWS, S), bf16), jax.ShapeDtypeStruct((V_ROWS, S), bf16),
                 jax.ShapeDtypeStruct((S, SEG_C), bf16))
    out_specs = (spec_ht(MOBA_HEADS, HEAD_DIM), spec_h(MOBA_HEADS, AUG),
                 pl.BlockSpec((HEAD_DIM, NSA_HEADS * ts), lambda i: (0, i)),
                 spec_r(AUG), spec_r(AUG),
                 spec_ht(FOX_HEADS, AUG), spec_h(FOX_HEADS, AUG),
                 spec_ht(MOBA_HEADS, V_ROWS), spec_ht(FOX_HEADS, V_ROWS),
                 spec_t, spec_t,
                 spec_r(SEG_C))
    return pl.pallas_call(
        functools.partial(_prep_kernel, ts=ts),
        grid=(S // ts,),
        in_specs=[pl.BlockSpec((ts, MAIN_COLS), lambda i: (i, 0)),
                  spec_r(LANES), spec_r(LANES),
                  pl.BlockSpec((1, SEG_A), lambda i: (0, 0)),
                  pl.BlockSpec((1, LANES), lambda i: (0, 0))],
        out_specs=out_specs,
        out_shape=out_shape,
        scratch_shapes=[pltpu.VMEM((8, LANES), f32)],
        compiler_params=_cparams(("arbitrary",)),
        name="prep",
    )(proj, cos_t, sin_t, gain_row, bf_row)


def _kmean_kernel(k_ref, o_ref, *, rows):
    n = rows // MOBA_BLOCK
    avg = jnp.where(_div(_iota2((n, rows), 1), MOBA_BLOCK) == _iota2((n, rows), 0),
                    1.0 / MOBA_BLOCK, 0.0).astype(bf16)
    o_ref[0] = _dot(avg, k_ref[0])[:, :HEAD_DIM]


def moba_kmean(mk_aug):
    H, S, _ = mk_aug.shape
    rows = 8 * MOBA_BLOCK
    return pl.pallas_call(
        functools.partial(_kmean_kernel, rows=rows),
        grid=(H, S // rows),
        in_specs=[pl.BlockSpec((1, rows, AUG), lambda h, i: (h, i, 0))],
        out_specs=pl.BlockSpec((1, 8, HEAD_DIM), lambda h, i: (h, i, 0)),
        out_shape=jax.ShapeDtypeStruct((H, CODE_BLOCKS, HEAD_DIM), f32),
        compiler_params=_cparams(("arbitrary", "arbitrary")),
        name="moba_kmean",
    )(mk_aug)


def _top_select(scores, idx, n_pick, floor):
    big = jnp.int32(2 ** 30)
    scores = list(scores)
    for _ in range(n_pick):
        for j, score in enumerate(scores):
            m = jnp.max(score, axis=0, keepdims=True)
            first = jnp.min(jnp.where(score == m, idx, big), axis=0, keepdims=True)
            first = jnp.where(m > floor, first, big)
            scores[j] = jnp.where(idx == first, -jnp.inf, score)
    return [s == -jnp.inf for s in scores]


def _moba_select_kernel(q_ref, km_ref, o_ref, *, tq, n_heads):
    i = pl.program_id(0)
    blk = _iota2((CODE_BLOCKS, tq), 0)
    cur = _div(_iota2((CODE_BLOCKS, tq), 1) + i * tq, MOBA_BLOCK)
    scores = []
    for h in range(n_heads):
        q = q_ref[h]
        km = km_ref[h]
        km_hi = km.astype(bf16)
        km_lo = (km - km_hi.astype(f32)).astype(bf16)
        gate = _dot(km_hi, q) + _dot(km_lo, q)
        scores.append(jnp.where(blk < cur, gate, NEG_INF))
    for h, sel in enumerate(_top_select(scores, blk, MOBA_TOPK, NEG_INF)):
        o_ref[h, 0, 0:HEAD_DIM, :] = q_ref[h]
        o_ref[h, 0, HEAD_DIM:AUG, :] = jnp.where(sel | (blk == cur), 0.0, NEG_INF).astype(bf16)


def moba_select(mq, kmean, tq=512):
    H, _, S = mq.shape
    return pl.pallas_call(
        functools.partial(_moba_select_kernel, tq=tq, n_heads=H),
        grid=(S // tq,),
        in_specs=[pl.BlockSpec((H, HEAD_DIM, tq), lambda i: (0, 0, i)),
                  pl.BlockSpec((H, CODE_BLOCKS, HEAD_DIM), lambda i: (0, 0, 0))],
        out_specs=pl.BlockSpec((H, 1, AUG, tq), lambda i: (0, 0, 0, i)),
        out_shape=jax.ShapeDtypeStruct((H, 1, AUG, S), bf16),
        compiler_params=_cparams(("arbitrary",)),
        name="moba_select",
    )(mq, kmean)


def _cmp_kernel(x_ref, pe_ref, w1_ref, w2_ref, gain_ref, cos_ref, sin_ref, kc_ref, vc_ref, *, ncp):
    top = jnp.zeros((ncp, 2 * NSA_CMP_HIDDEN), f32)
    nxt = jnp.zeros((ncp, 2 * NSA_CMP_HIDDEN), f32)
    for j in range(NSA_CMP_STRIDE):
        xj = x_ref[pl.ds(j, ncp, stride=NSA_CMP_STRIDE), :]
        top = top + _dot((xj + pe_ref[j:j + 1, :]).astype(bf16), w1_ref[j])
        nxt = nxt + _dot((xj + pe_ref[NSA_CMP_STRIDE + j:NSA_CMP_STRIDE + j + 1, :]).astype(bf16),
                         w1_ref[NSA_CMP_STRIDE + j])
    hid = top + pltpu.roll(nxt, ncp - 1, 0)
    act = (hid * (1.0 / (1.0 + jnp.exp(-hid)))).astype(bf16)
    k = _dot(act[:, :NSA_CMP_HIDDEN], w2_ref[0])
    v = _dot(act[:, NSA_CMP_HIDDEN:], w2_ref[1])
    ms = jnp.sum(k * k, axis=-1, keepdims=True) * (1.0 / HEAD_DIM)
    y = k * lax.rsqrt(ms + RMS_EPS) * gain_ref[...]
    lane = _iota2((ncp, LANES), 1)
    up = pltpu.roll(y, LANES - ROPE_DIM // 2, 1)
    dn = pltpu.roll(y, ROPE_DIM // 2, 1)
    y = y * cos_ref[...] + jnp.where(lane < ROPE_DIM // 2, up, dn) * sin_ref[...]
    kc_ref[...] = y[:, :HEAD_DIM].astype(bf16)
    vc_ref[...] = _dot_nt(_place_t(V_ROWS, LANES, 0, 0), v.astype(bf16)).astype(bf16)


def nsa_compress(proj, pe, w1, w2, gain, cos_c, sin_c):
    S = proj.shape[0]
    ncp = S // NSA_CMP_STRIDE
    full = lambda a: pl.BlockSpec(a.shape, lambda i: (0,) * a.ndim)
    return pl.pallas_call(
        functools.partial(_cmp_kernel, ncp=ncp),
        grid=(1,),
        in_specs=[pl.BlockSpec((S, LANES), lambda i: (0, KV_CMP_OFF // LANES)),
                  full(pe), full(w1), full(w2), full(gain), full(cos_c), full(sin_c)],
        out_specs=(pl.BlockSpec((ncp, HEAD_DIM), lambda i: (0, 0)), pl.BlockSpec((V_ROWS, ncp), lambda i: (0, 0))),
        out_shape=(jax.ShapeDtypeStruct((ncp, HEAD_DIM), bf16), jax.ShapeDtypeStruct((V_ROWS, ncp), bf16)),
        compiler_params=_cparams(("arbitrary",)),
        name="nsa_compress",
    )(proj, pe, w1, w2, gain, cos_c, sin_c)


def _nsa_cmp_select_kernel(q_ref, kc_ref, vc_ref, oc_ref, qa_ref, *, ncp, nsp, n_win, n_qb):
    i = pl.program_id(0)
    cols = NSA_HEADS * Q_BLOCK
    kc = kc_ref[...]
    vc = vc_ref[...]
    c0 = _iota2((nsp, ncp), 1) * NSA_CMP_STRIDE
    b0 = _iota2((nsp, ncp), 0) * NSA_SLC_BLOCK
    overlap = jnp.where((c0 <= b0 + (NSA_SLC_BLOCK - 1)) & (c0 + (NSA_CMP_LEN - 1) >= b0), 1.0, 0.0).astype(bf16)
    kend = _iota2((ncp, cols), 0) * NSA_CMP_STRIDE + (NSA_CMP_LEN - 1)
    col_pos = _mod(_iota2((ncp, cols), 1), Q_BLOCK)
    blk = _iota2((nsp, Q_BLOCK), 0)

    def scores(b):
        return _dot(kc, q_ref[:, b * cols:(b + 1) * cols])

    def probs(b, s):
        ok = kend <= (i * n_qb + b) * Q_BLOCK + col_pos
        sm = jnp.where(ok, s, NEG_INF)
        e = jnp.exp2(sm - jnp.max(sm, axis=0, keepdims=True))
        return jnp.where(ok, e * (1.0 / jnp.sum(e, axis=0, keepdims=True)), 0.0)

    def cur(b):
        return _div(_iota2((nsp, Q_BLOCK), 1) + (i * n_qb + b) * Q_BLOCK, NSA_SLC_BLOCK)

    def forced(b):
        return (blk == 0) | (blk == cur(b)) | (blk == cur(b) - 1)

    def importance(b, p):
        o = _dot(vc, p.astype(bf16))
        oc_ref[b * Q_BLOCK:(b + 1) * Q_BLOCK, :] = _heads_to_lanes(
            [o[0:HEAD_DIM, h * Q_BLOCK:(h + 1) * Q_BLOCK] for h in range(NSA_HEADS)]).astype(bf16)
        psum = (p[:, 0:Q_BLOCK] + p[:, Q_BLOCK:2 * Q_BLOCK]
                + p[:, 2 * Q_BLOCK:3 * Q_BLOCK] + p[:, 3 * Q_BLOCK:4 * Q_BLOCK])
        ph, pm, plo = _split3(psum)
        imp = _dot(overlap, ph) + _dot(overlap, pm) + _dot(overlap, plo)
        return jnp.where((blk <= cur(b)) & jnp.logical_not(forced(b)), imp, NEG_INF)

    block_scores = []
    s_next = scores(0)
    for b in range(n_qb):
        s = s_next
        if b + 1 < n_qb:
            s_next = scores(b + 1)
        block_scores.append(importance(b, probs(b, s)))
    n_free = NSA_SLC_TOPN - 3
    for b, sel in enumerate(_top_select(block_scores, blk, n_free, NEG_INF)):
        bias = jnp.where(sel | forced(b), 0.0, NEG_INF).astype(bf16)
        for w in range(n_win):
            bw = bias[w * CODE_BLOCKS:(w + 1) * CODE_BLOCKS, :]
            qa_ref[0, w, 0:HEAD_DIM, b * cols:(b + 1) * cols] = q_ref[:, b * cols:(b + 1) * cols]
            qa_ref[0, w, HEAD_DIM:AUG, b * cols:(b + 1) * cols] = jnp.concatenate([bw] * NSA_HEADS, axis=1)


def nsa_cmp_select(nq, kc, vc, S, n_qb=4):
    ncp = kc.shape[0]
    ns = S // NSA_SLC_BLOCK
    nsp = max(LANES, ns)
    n_win = max(1, ns // CODE_BLOCKS)
    cols = n_qb * NSA_HEADS * Q_BLOCK
    return pl.pallas_call(
        functools.partial(_nsa_cmp_select_kernel, ncp=ncp, nsp=nsp, n_win=n_win, n_qb=n_qb),
        grid=(S // (n_qb * Q_BLOCK),),
        in_specs=[pl.BlockSpec((HEAD_DIM, cols), lambda i: (0, i)),
                  pl.BlockSpec((ncp, HEAD_DIM), lambda i: (0, 0)),
                  pl.BlockSpec((V_ROWS, ncp), lambda i: (0, 0))],
        out_specs=(pl.BlockSpec((n_qb * Q_BLOCK, NSA_W), lambda i: (i, 0)),
                   pl.BlockSpec((1, n_win, AUG, cols), lambda i: (0, 0, 0, i))),
        out_shape=(jax.ShapeDtypeStruct((S, NSA_W), bf16),
                   jax.ShapeDtypeStruct((1, n_win, AUG, NSA_HEADS * S), bf16)),
        compiler_params=_cparams(("arbitrary",)),
        name="nsa_cmp_select",
    )(nq, kc, vc)


def _flash_kernel(qi_t, kb_t, win_t, first_t, last_t, var_t, q_ref, k_ref, v_ref, o_ref, m_sc, acc_sc,
                  *, n_heads, tq, qw, tq_pos, rep_cols, pos_block, window, variants):
    p_id = pl.program_id(0)
    qi = qi_t[p_id]
    kb = kb_t[p_id]

    @pl.when(first_t[p_id] == 1)
    def _():
        m_sc[...] = jnp.full(m_sc.shape, M_FLOOR, f32)
        acc_sc[...] = jnp.zeros_like(acc_sc)

    def tile(modes):
        chains = [(g, slice(c * qw, (c + 1) * qw), slice(kt * FLASH_TK, (kt + 1) * FLASH_TK))
                  for kt in range(FLASH_SUB) if modes[kt] for g in range(n_heads) for c in range(tq // qw)]
        if 2 in modes:
            rel = _mod(_iota2((FLASH_TK, qw), 1), pos_block) - _iota2((FLASH_TK, qw), 0)

        def scores(t):
            g, cols, keys = chains[t]
            s = _dot(k_ref[g, keys, :], q_ref[g, 0, :, cols])
            if modes[keys.start // FLASH_TK] == 1:
                return s
            base = (qi * tq_pos + (cols.start // rep_cols) * pos_block + cols.start % pos_block
                    - kb * (FLASH_SUB * FLASH_TK) - keys.start)
            delta = rel + base
            ok = (delta >= 0) if window is None else (delta.astype(jnp.uint32) < jnp.uint32(window))
            return jnp.where(ok, s, NEG_INF)

        def softmax(t, s):
            g, cols, _ = chains[t]
            m_prev = m_sc[g, 0:1, cols]
            m_new = jnp.maximum(m_prev, jnp.max(s, axis=0, keepdims=True))
            m_sc[g, :, cols] = jnp.broadcast_to(m_new, (m_sc.shape[1], qw))
            return jnp.exp2(s - m_new).astype(bf16), jnp.exp2(m_prev - m_new)

        def accumulate(t, p, alpha):
            g, cols, keys = chains[t]
            acc_sc[g, :, cols] = alpha * acc_sc[g, :, cols] + _dot(v_ref[g, :, keys], p)

        n = len(chains)
        s_next, staged = scores(0), None
        for t in range(n + 1):
            s = s_next
            if t + 1 < n:
                s_next = scores(t + 1)
            if staged is not None:
                accumulate(t - 1, *staged)
            staged = softmax(t, s) if t < n else None

    for vid, modes in enumerate(variants):
        pl.when(var_t[p_id] == vid)(functools.partial(tile, modes))

    @pl.when(last_t[p_id] == 1)
    def _():
        def normalised(g, cols):
            acc = acc_sc[g, :, cols]
            l = acc[HEAD_DIM:HEAD_DIM + 1, :]
            return acc[0:HEAD_DIM, :] * jnp.where(l > 0.0, 1.0 / l, 0.0)

        if n_heads > 1:
            o_ref[...] = _heads_to_lanes([normalised(g, slice(0, tq)) for g in range(n_heads)]).astype(o_ref.dtype)
        else:
            per_block = tq // tq_pos * pos_block
            for qb in range(tq // per_block):
                pieces = [normalised(0, slice(qb * per_block + h * pos_block, qb * per_block + (h + 1) * pos_block))
                          for h in range(per_block // pos_block)]
                o_ref[qb * pos_block:(qb + 1) * pos_block, :] = _heads_to_lanes(pieces).astype(o_ref.dtype)


def _step_tables(n_q, key_tiles, win_keys):
    variants, cols = [], ([], [], [], [], [], [])
    for a in range(n_q):
        tiles = key_tiles(a)
        groups = sorted({t // FLASH_SUB for t in tiles})
        for kb in groups:
            modes = tuple(tiles.get(kb * FLASH_SUB + j, 0) for j in range(FLASH_SUB))
            if modes not in variants:
                variants.append(modes)
            row = (a, kb, (kb * FLASH_SUB * FLASH_TK) // win_keys, int(kb == groups[0]), int(kb == groups[-1]),
                   variants.index(modes))
            for c, v in zip(cols, row):
                c.append(v)
    return tuple(jnp.asarray(np.asarray(c, np.int32)) for c in cols), tuple(variants)


def flash(q, k, v, steps, *, tq, pos_block, window, name):
    tables, variants = steps
    G, _, _, Sq = q.shape
    tq_pos = FLASH_TQ
    rep_cols = (tq // tq_pos) * pos_block
    assert rep_cols % FLASH_QW == 0 and (pos_block % FLASH_QW == 0 or FLASH_QW % pos_block == 0)
    tk = FLASH_SUB * FLASH_TK
    width = max(G, tq // tq_pos) * HEAD_DIM
    assert G == 1 or tq == tq_pos
    grid_spec = pltpu.PrefetchScalarGridSpec(
        num_scalar_prefetch=6,
        grid=(tables[0].shape[0],),
        in_specs=[pl.BlockSpec((G, 1, AUG, tq), lambda p, qi, kb, wi, fi, la, va: (0, wi[p], 0, qi[p])),
                  pl.BlockSpec((G, tk, AUG), lambda p, qi, kb, wi, fi, la, va: (0, kb[p], 0)),
                  pl.BlockSpec((G, V_ROWS, tk), lambda p, qi, kb, wi, fi, la, va: (0, 0, kb[p]))],
        out_specs=pl.BlockSpec((tq_pos, width), lambda p, qi, kb, wi, fi, la, va: (qi[p], 0)),
        scratch_shapes=[pltpu.VMEM((G, 8, tq), f32), pltpu.VMEM((G, V_ROWS, tq), f32)],
    )
    return pl.pallas_call(
        functools.partial(_flash_kernel, n_heads=G, tq=tq, qw=FLASH_QW, tq_pos=tq_pos, rep_cols=rep_cols,
                          pos_block=pos_block, window=window, variants=variants),
        grid_spec=grid_spec,
        out_shape=jax.ShapeDtypeStruct((Sq // (tq // tq_pos), width), bf16),
        compiler_params=_cparams(("arbitrary",)),
        name=name,
    )(*tables, q, k, v)


def _out_kernel(x_ref, gl0_ref, gl1_ref, gl2_ref, sd_ref, bg_ref, om_ref, of_ref, oc_ref, os_ref, ow_ref, zs_ref,
                wm_ref, wn_ref, wf_ref, wo_ref, o_ref):
    zs = zs_ref[...].astype(f32)
    sg = 1.0 / (1.0 + jnp.exp(-sd_ref[...]))
    sg_hi = sg.astype(bf16)
    sg_lo = (sg - sg_hi.astype(f32)).astype(bf16)
    src = _iota2((LANES, NSA_W), 0)
    head = _div(_iota2((LANES, NSA_W), 1), HEAD_DIM)
    o_n = jnp.zeros(oc_ref.shape, f32)
    for branch, ref in enumerate((oc_ref, os_ref, ow_ref)):
        spread = jnp.where(src == 3 * head + branch, 1.0, 0.0).astype(bf16)
        o_n = o_n + (_dot(sg_hi, spread) + _dot(sg_lo, spread)) * ref[...].astype(f32)
    mixers = ((om_ref[...].astype(f32), 0, wm_ref), (o_n, MOBA_W, wn_ref), (of_ref[...].astype(f32), MOBA_W + NSA_W, wf_ref))
    merged = jnp.zeros(x_ref.shape, f32)
    for (o, z_off, w_ref), gl_ref, b in zip(mixers, (gl0_ref, gl1_ref, gl2_ref), range(N_BRANCH)):
        y = _dot((o * zs[:, z_off:z_off + o.shape[1]]).astype(bf16), w_ref[...])
        gate = 1.0 / (1.0 + jnp.exp(-(gl_ref[...] + bg_ref[:, b * D_MODEL:(b + 1) * D_MODEL])))
        merged = merged + y * gate
    o_ref[...] = x_ref[...] + _dot(merged.astype(bf16), wo_ref[...])


def out_proj(x, proj, b_gate, o_m, o_f, o_c, o_s, o_w, zs, w_m, w_n, w_f, w_o, tm=256):
    S = x.shape[0]
    full = lambda a: pl.BlockSpec(a.shape, lambda i: (0,) * a.ndim)
    rows = lambda w: pl.BlockSpec((tm, w), lambda i: (i, 0))
    gl_spec = lambda b: pl.BlockSpec((tm, D_MODEL), lambda i: (i, MAIN_COLS // D_MODEL + b))
    return pl.pallas_call(
        _out_kernel,
        grid=(S // tm,),
        in_specs=[rows(D_MODEL), gl_spec(0), gl_spec(1), gl_spec(2),
                  pl.BlockSpec((tm, LANES), lambda i: (i, OFF_D // LANES)),
                  full(b_gate), rows(MOBA_W), rows(FOX_W), rows(NSA_W), rows(NSA_W), rows(NSA_W), rows(SEG_C),
                  full(w_m), full(w_n), full(w_f), full(w_o)],
        out_specs=rows(D_MODEL),
        out_shape=jax.ShapeDtypeStruct((S, D_MODEL), f32),
        compiler_params=_cparams(("arbitrary",)),
        name="out_proj",
    )(x, proj, proj, proj, proj, b_gate, o_m, o_f, o_c, o_s, o_w, zs, w_m, w_n, w_f, w_o)


def _rope_tables(pos, rows):
    half = ROPE_DIM // 2
    inv = ROPE_THETA ** (-jnp.arange(0, ROPE_DIM, 2, dtype=f32) / ROPE_DIM)
    d = np.arange(LANES) % HEAD_DIM
    rotated = d < ROPE_DIM
    inv_row = jnp.where(rotated, inv[d % half], 0.0)
    sign_row = np.where(d < half, -1.0, 1.0).astype(np.float32)
    ang = pos.astype(f32)[:, None] * inv_row[None, :]
    pad = ((0, rows - pos.shape[0]), (0, 0))
    return jnp.pad(jnp.cos(ang), pad), jnp.pad(jnp.sin(ang) * sign_row[None, :], pad)


def _layer(x, norm_g, w_main, b_f, b_gate, moba_qk_g, nsa_q_g, nsa_k_g, fox_qk_g,
           cmp_pe, cmp_w1, cmp_w2, w_up_moba, w_up_nsa, w_up_fox, w_out, tables, rope):
    S = x.shape[0]
    proj = rms_matmul(x, norm_g.reshape(1, D_MODEL), w_main)

    gain_row = jnp.concatenate([jnp.tile(moba_qk_g[0], MOBA_HEADS), jnp.tile(moba_qk_g[1], MOBA_HEADS),
                                jnp.tile(nsa_q_g, NSA_HEADS), nsa_k_g[1], nsa_k_g[2],
                                jnp.tile(fox_qk_g[0], FOX_HEADS), jnp.tile(fox_qk_g[1], FOX_HEADS)]).reshape(1, SEG_A)
    bf_row = jnp.zeros((1, LANES), f32).at[0, FF_LANE:FF_LANE + FOX_HEADS].set(b_f)
    (mq, mk, nq, ksl, kw, fq, fk, mv, fv, vsl, vw, zs) = prep(proj, rope[0], rope[1], gain_row, bf_row)

    q_moba = moba_select(mq, moba_kmean(mk))
    o_m = flash(q_moba, mk, mv, tables["causal"], tq=FLASH_TQ, pos_block=FLASH_TQ, window=None, name="flash_moba")

    pe = jnp.concatenate([cmp_pe[0], cmp_pe[1]], axis=1)
    w1 = cmp_w1.astype(bf16).reshape(2, NSA_CMP_LEN, HEAD_DIM, NSA_CMP_HIDDEN)
    zero = jnp.zeros_like(w1[0])
    w1 = jnp.concatenate([jnp.concatenate([w1[0], zero], axis=2),
                          jnp.concatenate([zero, w1[1]], axis=2)], axis=1)
    w2p = jnp.pad(cmp_w2, ((0, 0), (0, 0), (0, LANES - HEAD_DIM))).astype(bf16)
    gain_c = jnp.pad(nsa_k_g[0], (0, LANES - HEAD_DIM)).reshape(1, LANES)
    kc, vc = nsa_compress(proj, pe, w1, w2p, gain_c, rope[2], rope[3])
    o_c, q_slc = nsa_cmp_select(nq, kc, vc, S)
    nsa_cols = NSA_HEADS * FLASH_TQ
    o_s = flash(q_slc, ksl[None], vsl[None], tables["slc"], tq=nsa_cols,
                pos_block=Q_BLOCK, window=None, name="flash_nsa_slc")
    o_w = flash(q_slc, kw[None], vw[None], tables["win"], tq=nsa_cols,
                pos_block=Q_BLOCK, window=NSA_WINDOW, name="flash_nsa_win")

    o_f = flash(fq[:, None], fk, fv, tables["causal"], tq=FLASH_TQ, pos_block=FLASH_TQ, window=None, name="flash_fox")
    return out_proj(x, proj, b_gate.reshape(1, N_BRANCH * D_MODEL), o_m, o_f, o_c, o_s, o_w, zs,
                    w_up_moba.astype(bf16), w_up_nsa.astype(bf16), w_up_fox.astype(bf16), w_out.astype(bf16))


def kernel(x, norm_g, w_in, b_f, b_gate, moba_qk_g, nsa_q_g, nsa_k_g, fox_qk_g, cmp_pe, cmp_w1, cmp_w2,
           w_up_moba, w_up_nsa, w_up_fox, w_out):
    B, S, _ = x.shape
    assert B == 1 and S % 2048 == 0 and S // MOBA_BLOCK <= CODE_BLOCKS
    depth = norm_g.shape[0]
    win_keys = CODE_BLOCKS * NSA_SLC_BLOCK
    n_q = S // FLASH_TQ
    causal = lambda a: {t: (2 if (t + 1) * FLASH_TK > a * FLASH_TQ else 1)
                        for t in range((a + 1) * FLASH_TQ // FLASH_TK)}
    band = lambda a: {t: 2 for t in range(max(0, a * FLASH_TQ - NSA_WINDOW + 1) // FLASH_TK,
                                          (a + 1) * FLASH_TQ // FLASH_TK)}
    tables = {"causal": _step_tables(n_q, causal, S), "slc": _step_tables(n_q, causal, win_keys),
              "win": _step_tables(n_q, band, S)}
    ncp = S // NSA_CMP_STRIDE
    cos_t, sin_t = _rope_tables(jnp.arange(S), S)
    cmp_end = jnp.arange(ncp - 1) * NSA_CMP_STRIDE + (NSA_CMP_LEN - 1)
    cos_c, sin_c = _rope_tables(cmp_end, ncp)
    rope = (cos_t, sin_t, cos_c, sin_c)
    h = x[0]
    for l in range(depth):
        h = _layer(h, norm_g[l], repack_w_in(w_in, l), b_f[l], b_gate[l], moba_qk_g[l], nsa_q_g[l], nsa_k_g[l], fox_qk_g[l],
                   cmp_pe[l], cmp_w1[l], cmp_w2[l], w_up_moba[l], w_up_nsa[l], w_up_fox[l], w_out[l], tables, rope)
    return h[None]
```

```python
import functools

import numpy as np
import jax
import jax.numpy as jnp
from jax import lax
from jax.experimental import pallas as pl
from jax.experimental.pallas import tpu as pltpu

D_MODEL = 1024
HEAD_DIM = 64
ROPE_DIM = HEAD_DIM // 4
ROPE_THETA = 500000.0
RMS_EPS = 1e-6
NEG_INF = -1e30
MASKED_BELOW = 0.5 * NEG_INF
M_FLOOR = -1e20

MOBA_HEADS = 6
MOBA_BLOCK = 256
MOBA_TOPK = 3
NSA_HEADS = 4
NSA_CMP_LEN = 32
NSA_CMP_STRIDE = 16
NSA_CMP_HIDDEN = 4 * HEAD_DIM
NSA_SLC_BLOCK = 64
NSA_SLC_TOPN = 16
NSA_WINDOW = 512
FOX_HEADS = 6
N_BRANCH = 3
MOBA_W = MOBA_HEADS * HEAD_DIM
NSA_W = NSA_HEADS * HEAD_DIM
FOX_W = FOX_HEADS * HEAD_DIM
IN_SPLITS = (MOBA_W,) * 4 + (NSA_W,) + (HEAD_DIM,) * 6 + (3 * NSA_HEADS, NSA_W) + (FOX_W,) * 3 + (FOX_HEADS, FOX_W, N_BRANCH * D_MODEL)
ATTN_SCALE = HEAD_DIM ** -0.5

LANES = 128
AUG = 2 * HEAD_DIM
CODE_BLOCKS = AUG - HEAD_DIM
Q_BLOCK = 128

SEG_A = 2 * MOBA_W + NSA_W + 2 * HEAD_DIM + 2 * FOX_W
SEG_B = 2 * MOBA_W + 4 * HEAD_DIM
SEG_C = MOBA_W + NSA_W + FOX_W
SEG_D = LANES
OFF_B = SEG_A
OFF_C = OFF_B + SEG_B
OFF_D = OFF_C + SEG_C
MAIN_COLS = OFF_D + SEG_D
FF_LANE = 3 * NSA_HEADS
KV_CMP_OFF = OFF_B + 2 * MOBA_W + 2 * HEAD_DIM
ALL_COLS = MAIN_COLS + N_BRANCH * D_MODEL

VMEM_LIMIT = 56 * 1024 * 1024
FLASH_QW = 512
FLASH_TQ = 512
FLASH_TK = 256
FLASH_SUB = 8
LOG2E = 1.4426950408889634
Q_SCALE = ATTN_SCALE * LOG2E
V_ROWS = 80

f32 = jnp.float32
bf16 = jnp.bfloat16


def _cparams(sem):
    return pltpu.CompilerParams(dimension_semantics=sem, vmem_limit_bytes=VMEM_LIMIT)


def _iota2(shape, dim):
    return lax.broadcasted_iota(jnp.int32, shape, dim)


def _div(x, d):
    return jnp.right_shift(x, int(d).bit_length() - 1)


def _mod(x, d):
    return jnp.bitwise_and(x, d - 1)


def _place(n_in, n_out, in_off, out_off, width=HEAD_DIM, val=1.0):
    r = _iota2((n_in, n_out), 0) - in_off
    c = _iota2((n_in, n_out), 1) - out_off
    hit = (r == c) & (r >= 0) & (r < width)
    return jnp.where(hit, val, 0.0).astype(bf16)


def _place_t(n_out, n_in, in_off, out_off, width=HEAD_DIM, val=1.0):
    r = _iota2((n_out, n_in), 0) - out_off
    c = _iota2((n_out, n_in), 1) - in_off
    hit = (r == c) & (r >= 0) & (r < width)
    return jnp.where(hit, val, 0.0).astype(bf16)


def _split3(x):
    hi = x.astype(bf16)
    r = x - hi.astype(f32)
    mid = r.astype(bf16)
    lo = (r - mid.astype(f32)).astype(bf16)
    return hi, mid, lo


def _heads_to_lanes(pieces):
    rows = len(pieces) * HEAD_DIM
    pad = [jnp.zeros((-rows % LANES, pieces[0].shape[1]), f32)] if rows % LANES else []
    return jnp.concatenate(list(pieces) + pad, axis=0).T[:, :rows]


def _dot(a, b):
    return jnp.dot(a, b, preferred_element_type=f32)


def _dot_nt(a, b):
    return lax.dot_general(a, b, (((1,), (1,)), ((), ())), preferred_element_type=f32)


def _w_in_plan():
    offs = np.concatenate([[0], np.cumsum(IN_SPLITS)])
    names = ("mq", "mk", "mv", "mz", "nq", "kc", "vc", "ksl", "vsl", "kw", "vw", "ng", "nz", "fq", "fk", "fv", "ff", "fz", "gl")
    start = {n: int(offs[j]) for j, n in enumerate(names)}
    width = {n: int(IN_SPLITS[j]) for j, n in enumerate(names)}
    order = ("mq", "mk", "nq", "ksl", "kw", "fq", "fk", "mv", "fv", "vsl", "vw", "kc", "vc", "mz", "nz", "fz", "ng", "ff")
    plan = [[] for _ in range(ALL_COLS // LANES)]
    new = 0
    for n in order + ("pad", "gl"):
        if n == "pad":
            new = MAIN_COLS
            continue
        src, left = start[n], width[n]
        while left > 0:
            w = min(left, LANES - new % LANES, LANES - src % LANES)
            plan[new // LANES].append((src // LANES, src % LANES, new % LANES, w))
            src, new, left = src + w, new + w, left - w
    assert new == ALL_COLS
    return plan


def _repack_kernel(w_ref, o_ref, *, plan, tr, n_cols):
    lane = _iota2((tr, LANES), 1)
    loaded = {}

    def source(a):
        if a not in loaded:
            x = w_ref[0, :, a * LANES:(a + 1) * LANES]
            if (a + 1) * LANES > n_cols:
                x = jnp.where(lane < n_cols - a * LANES, x, 0.0)
            loaded[a] = x.astype(bf16)
        return loaded[a]

    for b, pieces in enumerate(plan):
        acc = jnp.zeros((tr, LANES), f32)
        for a, lane_in, lane_out, w in pieces:
            acc = acc + _dot(source(a), _place(LANES, LANES, lane_in, lane_out, width=w))
        o_ref[:, b * LANES:(b + 1) * LANES] = acc.astype(bf16)


def repack_w_in(w_in, layer, tr=128):
    _, D, n_cols = w_in.shape
    padded = pl.cdiv(n_cols, LANES) * LANES
    return pl.pallas_call(
        functools.partial(_repack_kernel, plan=_w_in_plan(), tr=tr, n_cols=n_cols),
        grid=(D // tr,),
        in_specs=[pl.BlockSpec((1, tr, padded), lambda i: (layer, i, 0))],
        out_specs=pl.BlockSpec((tr, ALL_COLS), lambda i: (i, 0)),
        out_shape=jax.ShapeDtypeStruct((D, ALL_COLS), bf16),
        compiler_params=_cparams(("arbitrary",)),
        name="repack_w_in",
    )(w_in)


def _rms_matmul_kernel(x_ref, g_ref, w_ref, o_ref, h_sc):
    @pl.when(pl.program_id(1) == 0)
    def _():
        x = x_ref[...]
        ms = jnp.mean(x * x, axis=-1, keepdims=True)
        h_sc[...] = (x * lax.rsqrt(ms + RMS_EPS) * g_ref[...]).astype(bf16)

    o_ref[...] = _dot(h_sc[...], w_ref[...]).astype(o_ref.dtype)


def rms_matmul(x, g, w, col0, n_cols, dtype, tm=1024, tn=1024):
    S, D = x.shape
    return pl.pallas_call(
        _rms_matmul_kernel,
        grid=(S // tm, n_cols // tn),
        in_specs=[pl.BlockSpec((tm, D), lambda i, j: (i, 0)),
                  pl.BlockSpec((1, D), lambda i, j: (0, 0)),
                  pl.BlockSpec((D, tn), lambda i, j: (0, col0 // tn + j))],
        out_specs=pl.BlockSpec((tm, tn), lambda i, j: (i, j)),
        out_shape=jax.ShapeDtypeStruct((S, n_cols), dtype),
        scratch_shapes=[pltpu.VMEM((tm, D), bf16)],
        compiler_params=_cparams(("arbitrary", "arbitrary")),
        name="rms_matmul",
    )(x, g, w)


def _prep_kernel(p_ref, cos_ref, sin_ref, gain_ref, bf_ref,
                 mq_ref, mk_ref, nq_ref, ksl_ref, kw_ref, fq_ref, fk_ref,
                 mv_ref, fv_ref, vsl_ref, vw_ref, zs_ref, carry_sc, *, ts):
    i = pl.program_id(0)

    @pl.when(i == 0)
    def _():
        carry_sc[...] = jnp.zeros_like(carry_sc)

    lane = _iota2((ts, LANES), 1)
    pos = _iota2((ts, LANES), 0) + i * ts
    blockdiag = jnp.where(_div(_iota2((LANES, LANES), 0), HEAD_DIM) == _div(_iota2((LANES, LANES), 1), HEAD_DIM),
                          1.0, 0.0).astype(bf16)
    first_half = _mod(lane, HEAD_DIM) < (ROPE_DIM // 2)
    low_lanes = lane < HEAD_DIM
    cos = cos_ref[...]
    sin = sin_ref[...]

    def normed(c, rope, scale=None):
        x = p_ref[:, c * LANES:(c + 1) * LANES]
        x2 = x * x
        hi = x2.astype(bf16)
        lo = (x2 - hi.astype(f32)).astype(bf16)
        ss = _dot(hi, blockdiag) + _dot(lo, blockdiag)
        y = x * lax.rsqrt(ss * (1.0 / HEAD_DIM) + RMS_EPS) * gain_ref[:, c * LANES:(c + 1) * LANES]
        if rope:
            up = pltpu.roll(y, LANES - ROPE_DIM // 2, 1)
            dn = pltpu.roll(y, ROPE_DIM // 2, 1)
            y = y * cos + jnp.where(first_half, up, dn) * sin
        return y if scale is None else y * scale

    def head_rows(y_t, half):
        return y_t[half * HEAD_DIM:(half + 1) * HEAD_DIM, :]

    def head_lanes(y, half):
        return y if half == 0 else pltpu.roll(y, HEAD_DIM, 1)

    d = p_ref[:, OFF_D:OFF_D + LANES] + bf_ref[...]
    logf = jnp.minimum(d, 0.0) - jnp.log(1.0 + jnp.exp(-jnp.abs(d)))
    tri = jnp.where(_iota2((ts, ts), 1) <= _iota2((ts, ts), 0), 1.0, 0.0).astype(bf16)
    lh, lm, ll = _split3(logf)
    c = carry_sc[0:1, :] + (_dot(tri, lh) + _dot(tri, lm) + _dot(tri, ll))
    carry_sc[...] = jnp.broadcast_to(c[ts - 1:ts, :], carry_sc.shape)
    c2 = c * LOG2E
    pieces = [p.astype(f32) for p in _split3(c2)]
    pieces_t = [p.astype(f32) for p in _split3(c2.T)]
    row64 = _iota2((HEAD_DIM, ts), 0)

    def decay_rows(h):
        r = FF_LANE + h
        hi, mid, lo = (jnp.broadcast_to(p[r:r + 1, :], (HEAD_DIM, ts)) for p in pieces_t)
        return jnp.where(row64 == 0, hi, jnp.where(row64 == 1, mid, jnp.where(row64 == 2, lo,
                         jnp.where(row64 < 6, 1.0, 0.0))))

    def decay_lanes(h):
        r = FF_LANE + h
        hi, mid, lo = (jnp.broadcast_to(p[:, r:r + 1], (ts, LANES)) for p in pieces)
        return jnp.where(lane == HEAD_DIM + 3, -hi, jnp.where(lane == HEAD_DIM + 4, -mid,
                         jnp.where(lane == HEAD_DIM + 5, -lo, jnp.where(lane < HEAD_DIM + 3, 1.0, 0.0))))

    moba_code = jnp.where((lane - HEAD_DIM) == _div(pos, MOBA_BLOCK), 1.0, 0.0)
    for c_i in range(3):
        yq_t = normed(c_i, True, Q_SCALE).T
        yk = normed(3 + c_i, True)
        for half in range(2):
            h = 2 * c_i + half
            mq_ref[h] = head_rows(yq_t, half).astype(bf16)
            mk_ref[h] = jnp.where(low_lanes, head_lanes(yk, half), moba_code).astype(bf16)
    for c_i in range(2):
        y_t = normed(6 + c_i, True, Q_SCALE).T
        for half in range(2):
            h = 2 * c_i + half
            yh = head_rows(y_t, half).astype(bf16)
            for qb in range(ts // Q_BLOCK):
                nq_ref[:, (qb * NSA_HEADS + h) * Q_BLOCK:(qb * NSA_HEADS + h + 1) * Q_BLOCK] = (
                    yh[:, qb * Q_BLOCK:(qb + 1) * Q_BLOCK])
    y = normed(8, True)
    slc_code = jnp.where((lane - HEAD_DIM) == _mod(_div(pos, NSA_SLC_BLOCK), CODE_BLOCKS), 1.0, 0.0)
    ksl_ref[...] = jnp.where(low_lanes, y, slc_code).astype(bf16)
    kw_ref[...] = jnp.where(low_lanes, head_lanes(y, 1), 0.0).astype(bf16)
    for c_i in range(3):
        yq_t = normed(9 + c_i, False, Q_SCALE).T
        yk = normed(12 + c_i, False)
        for half in range(2):
            h = 2 * c_i + half
            fq_ref[h] = jnp.concatenate([head_rows(yq_t, half), decay_rows(h)], axis=0).astype(bf16)
            fk_ref[h] = jnp.where(low_lanes, head_lanes(yk, half), decay_lanes(h)).astype(bf16)
    ones_rows = jnp.where(_iota2((V_ROWS - HEAD_DIM, ts), 0) == 0, 1.0, 0.0)

    def value_rows(x_t, half):
        return jnp.concatenate([head_rows(x_t, half), ones_rows], axis=0).astype(bf16)

    for c_i in range(3):
        xm_t = p_ref[:, OFF_B + c_i * LANES:OFF_B + (c_i + 1) * LANES].T
        xf_t = p_ref[:, OFF_B + MOBA_W + c_i * LANES:OFF_B + MOBA_W + (c_i + 1) * LANES].T
        for half in range(2):
            mv_ref[2 * c_i + half] = value_rows(xm_t, half)
            fv_ref[2 * c_i + half] = value_rows(xf_t, half)
    xs_t = p_ref[:, OFF_B + 2 * MOBA_W:OFF_B + 2 * MOBA_W + LANES].T
    vsl_ref[...] = value_rows(xs_t, 0)
    vw_ref[...] = value_rows(xs_t, 1)
    z = p_ref[:, OFF_C:OFF_C + SEG_C]
    zs_ref[...] = (z * (1.0 / (1.0 + jnp.exp(-z)))).astype(bf16)


def prep(proj, cos_t, sin_t, gain_row, bf_row, ts=256):
    S = proj.shape[0]
    head128 = lambda n: jax.ShapeDtypeStruct((n, S, AUG), bf16)
    spec_h = lambda n, w: pl.BlockSpec((n, ts, w), lambda i: (0, i, 0))
    spec_r = lambda w: pl.BlockSpec((ts, w), lambda i: (i, 0))
    head_t = lambda n, r: jax.ShapeDtypeStruct((n, r, S), bf16)
    spec_ht = lambda n, r: pl.BlockSpec((n, r, ts), lambda i: (0, 0, i))
    spec_t = pl.BlockSpec((V_ROWS, ts), lambda i: (0, i))
    out_shape = (head_t(MOBA_HEADS, HEAD_DIM), head128(MOBA_HEADS),
                 jax.ShapeDtypeStruct((HEAD_DIM, NSA_HEADS * S), bf16),
                 jax.ShapeDtypeStruct((S, AUG), bf16), jax.ShapeDtypeStruct((S, AUG), bf16),
                 head_t(FOX_HEADS, AUG), head128(FOX_HEADS),
                 head_t(MOBA_HEADS, V_ROWS), head_t(FOX_HEADS, V_ROWS),
                 jax.ShapeDtypeStruct((V_ROWS, S), bf16), jax.ShapeDtypeStruct((V_ROWS, S), bf16),
                 jax.ShapeDtypeStruct((S, SEG_C), bf16))
    out_specs = (spec_ht(MOBA_HEADS, HEAD_DIM), spec_h(MOBA_HEADS, AUG),
                 pl.BlockSpec((HEAD_DIM, NSA_HEADS * ts), lambda i: (0, i)),
                 spec_r(AUG), spec_r(AUG),
                 spec_ht(FOX_HEADS, AUG), spec_h(FOX_HEADS, AUG),
                 spec_ht(MOBA_HEADS, V_ROWS), spec_ht(FOX_HEADS, V_ROWS),
                 spec_t, spec_t,
                 spec_r(SEG_C))
    return pl.pallas_call(
        functools.partial(_prep_kernel, ts=ts),
        grid=(S // ts,),
        in_specs=[pl.BlockSpec((ts, MAIN_COLS), lambda i: (i, 0)),
                  spec_r(LANES), spec_r(LANES),
                  pl.BlockSpec((1, SEG_A), lambda i: (0, 0)),
                  pl.BlockSpec((1, LANES), lambda i: (0, 0))],
        out_specs=out_specs,
        out_shape=out_shape,
        scratch_shapes=[pltpu.VMEM((8, LANES), f32)],
        compiler_params=_cparams(("arbitrary",)),
        name="prep",
    )(proj, cos_t, sin_t, gain_row, bf_row)


def _kmean_kernel(k_ref, o_ref, *, rows):
    n = rows // MOBA_BLOCK
    avg = jnp.where(_div(_iota2((n, rows), 1), MOBA_BLOCK) == _iota2((n, rows), 0),
                    1.0 / MOBA_BLOCK, 0.0).astype(bf16)
    o_ref[0] = _dot(avg, k_ref[0])[:, :HEAD_DIM]


def moba_kmean(mk_aug):
    H, S, _ = mk_aug.shape
    rows = 8 * MOBA_BLOCK
    return pl.pallas_call(
        functools.partial(_kmean_kernel, rows=rows),
        grid=(H, S // rows),
        in_specs=[pl.BlockSpec((1, rows, AUG), lambda h, i: (h, i, 0))],
        out_specs=pl.BlockSpec((1, 8, HEAD_DIM), lambda h, i: (h, i, 0)),
        out_shape=jax.ShapeDtypeStruct((H, CODE_BLOCKS, HEAD_DIM), f32),
        compiler_params=_cparams(("arbitrary", "arbitrary")),
        name="moba_kmean",
    )(mk_aug)


def _top_select(scores, idx, n_pick, floor):
    big = jnp.int32(2 ** 30)
    scores = list(scores)
    for _ in range(n_pick):
        for j, score in enumerate(scores):
            m = jnp.max(score, axis=0, keepdims=True)
            first = jnp.min(jnp.where(score == m, idx, big), axis=0, keepdims=True)
            first = jnp.where(m > floor, first, big)
            scores[j] = jnp.where(idx == first, -jnp.inf, score)
    return [s == -jnp.inf for s in scores]


def _moba_select_kernel(q_ref, km_ref, o_ref, *, tq, n_heads):
    i = pl.program_id(0)
    blk = _iota2((CODE_BLOCKS, tq), 0)
    cur = _div(_iota2((CODE_BLOCKS, tq), 1) + i * tq, MOBA_BLOCK)
    scores = []
    for h in range(n_heads):
        q = q_ref[h]
        km = km_ref[h]
        km_hi = km.astype(bf16)
        km_lo = (km - km_hi.astype(f32)).astype(bf16)
        gate = _dot(km_hi, q) + _dot(km_lo, q)
        scores.append(jnp.where(blk < cur, gate, NEG_INF))
    for h, sel in enumerate(_top_select(scores, blk, MOBA_TOPK, MASKED_BELOW)):
        o_ref[h, 0, 0:HEAD_DIM, :] = q_ref[h]
        o_ref[h, 0, HEAD_DIM:AUG, :] = jnp.where(sel | (blk == cur), 0.0, NEG_INF).astype(bf16)


def moba_select(mq, kmean, tq=512):
    H, _, S = mq.shape
    return pl.pallas_call(
        functools.partial(_moba_select_kernel, tq=tq, n_heads=H),
        grid=(S // tq,),
        in_specs=[pl.BlockSpec((H, HEAD_DIM, tq), lambda i: (0, 0, i)),
                  pl.BlockSpec((H, CODE_BLOCKS, HEAD_DIM), lambda i: (0, 0, 0))],
        out_specs=pl.BlockSpec((H, 1, AUG, tq), lambda i: (0, 0, 0, i)),
        out_shape=jax.ShapeDtypeStruct((H, 1, AUG, S), bf16),
        compiler_params=_cparams(("arbitrary",)),
        name="moba_select",
    )(mq, kmean)


def _cmp_kernel(x_ref, pe_ref, w1_ref, w2_ref, gain_ref, cos_ref, sin_ref, kc_ref, vc_ref, *, ncp):
    top = jnp.zeros((ncp, 2 * NSA_CMP_HIDDEN), f32)
    nxt = jnp.zeros((ncp, 2 * NSA_CMP_HIDDEN), f32)
    for j in range(NSA_CMP_STRIDE):
        xj = x_ref[pl.ds(j, ncp, stride=NSA_CMP_STRIDE), :]
        top = top + _dot((xj + pe_ref[j:j + 1, :]).astype(bf16), w1_ref[j])
        nxt = nxt + _dot((xj + pe_ref[NSA_CMP_STRIDE + j:NSA_CMP_STRIDE + j + 1, :]).astype(bf16),
                         w1_ref[NSA_CMP_STRIDE + j])
    hid = top + pltpu.roll(nxt, ncp - 1, 0)
    act = (hid * (1.0 / (1.0 + jnp.exp(-hid)))).astype(bf16)
    k = _dot(act[:, :NSA_CMP_HIDDEN], w2_ref[0])
    v = _dot(act[:, NSA_CMP_HIDDEN:], w2_ref[1])
    ms = jnp.sum(k * k, axis=-1, keepdims=True) * (1.0 / HEAD_DIM)
    y = k * lax.rsqrt(ms + RMS_EPS) * gain_ref[...]
    lane = _iota2((ncp, LANES), 1)
    up = pltpu.roll(y, LANES - ROPE_DIM // 2, 1)
    dn = pltpu.roll(y, ROPE_DIM // 2, 1)
    y = y * cos_ref[...] + jnp.where(lane < ROPE_DIM // 2, up, dn) * sin_ref[...]
    kc_ref[...] = y[:, :HEAD_DIM].astype(bf16)
    vc_ref[...] = _dot_nt(_place_t(V_ROWS, LANES, 0, 0), v.astype(bf16)).astype(bf16)


def nsa_compress(proj, pe, w1, w2, gain, cos_c, sin_c):
    S = proj.shape[0]
    ncp = S // NSA_CMP_STRIDE
    full = lambda a: pl.BlockSpec(a.shape, lambda i: (0,) * a.ndim)
    return pl.pallas_call(
        functools.partial(_cmp_kernel, ncp=ncp),
        grid=(1,),
        in_specs=[pl.BlockSpec((S, LANES), lambda i: (0, KV_CMP_OFF // LANES)),
                  full(pe), full(w1), full(w2), full(gain), full(cos_c), full(sin_c)],
        out_specs=(pl.BlockSpec((ncp, HEAD_DIM), lambda i: (0, 0)), pl.BlockSpec((V_ROWS, ncp), lambda i: (0, 0))),
        out_shape=(jax.ShapeDtypeStruct((ncp, HEAD_DIM), bf16), jax.ShapeDtypeStruct((V_ROWS, ncp), bf16)),
        compiler_params=_cparams(("arbitrary",)),
        name="nsa_compress",
    )(proj, pe, w1, w2, gain, cos_c, sin_c)


def _nsa_cmp_select_kernel(q_ref, kc_ref, vc_ref, oc_ref, qa_ref, *, ncp, nsp, n_win, n_qb):
    i = pl.program_id(0)
    cols = NSA_HEADS * Q_BLOCK
    kc = kc_ref[...]
    vc = vc_ref[...]
    c0 = _iota2((nsp, ncp), 1) * NSA_CMP_STRIDE
    b0 = _iota2((nsp, ncp), 0) * NSA_SLC_BLOCK
    overlap = jnp.where((c0 <= b0 + (NSA_SLC_BLOCK - 1)) & (c0 + (NSA_CMP_LEN - 1) >= b0), 1.0, 0.0).astype(bf16)
    kend = _iota2((ncp, cols), 0) * NSA_CMP_STRIDE + (NSA_CMP_LEN - 1)
    col_pos = _mod(_iota2((ncp, cols), 1), Q_BLOCK)
    blk = _iota2((nsp, Q_BLOCK), 0)

    def scores(b):
        return _dot(kc, q_ref[:, b * cols:(b + 1) * cols])

    def probs(b, s):
        ok = kend <= (i * n_qb + b) * Q_BLOCK + col_pos
        sm = jnp.where(ok, s, NEG_INF)
        m = jnp.max(sm, axis=0, keepdims=True)
        e = jnp.exp2(sm - m)
        return e * jnp.where(m > MASKED_BELOW, 1.0 / jnp.sum(e, axis=0, keepdims=True), 0.0)

    def cur(b):
        return _div(_iota2((nsp, Q_BLOCK), 1) + (i * n_qb + b) * Q_BLOCK, NSA_SLC_BLOCK)

    def forced(b):
        return (blk == 0) | (blk == cur(b)) | (blk == cur(b) - 1)

    def importance(b, p):
        o = _dot(vc, p.astype(bf16))
        oc_ref[b * Q_BLOCK:(b + 1) * Q_BLOCK, :] = _heads_to_lanes(
            [o[0:HEAD_DIM, h * Q_BLOCK:(h + 1) * Q_BLOCK] for h in range(NSA_HEADS)]).astype(bf16)
        psum = (p[:, 0:Q_BLOCK] + p[:, Q_BLOCK:2 * Q_BLOCK]
                + p[:, 2 * Q_BLOCK:3 * Q_BLOCK] + p[:, 3 * Q_BLOCK:4 * Q_BLOCK])
        ph, pm, plo = _split3(psum)
        imp = _dot(overlap, ph) + _dot(overlap, pm) + _dot(overlap, plo)
        return jnp.where((blk <= cur(b)) & jnp.logical_not(forced(b)), imp, NEG_INF)

    block_scores = []
    s_next = scores(0)
    for b in range(n_qb):
        s = s_next
        if b + 1 < n_qb:
            s_next = scores(b + 1)
        block_scores.append(importance(b, probs(b, s)))
    n_free = NSA_SLC_TOPN - 3
    for b, sel in enumerate(_top_select(block_scores, blk, n_free, MASKED_BELOW)):
        bias = jnp.where(sel | forced(b), 0.0, NEG_INF).astype(bf16)
        for w in range(n_win):
            bw = bias[w * CODE_BLOCKS:(w + 1) * CODE_BLOCKS, :]
            qa_ref[0, w, 0:HEAD_DIM, b * cols:(b + 1) * cols] = q_ref[:, b * cols:(b + 1) * cols]
            qa_ref[0, w, HEAD_DIM:AUG, b * cols:(b + 1) * cols] = jnp.concatenate([bw] * NSA_HEADS, axis=1)


def nsa_cmp_select(nq, kc, vc, S, n_qb=4):
    ncp = kc.shape[0]
    ns = S // NSA_SLC_BLOCK
    nsp = max(LANES, ns)
    n_win = max(1, ns // CODE_BLOCKS)
    cols = n_qb * NSA_HEADS * Q_BLOCK
    return pl.pallas_call(
        functools.partial(_nsa_cmp_select_kernel, ncp=ncp, nsp=nsp, n_win=n_win, n_qb=n_qb),
        grid=(S // (n_qb * Q_BLOCK),),
        in_specs=[pl.BlockSpec((HEAD_DIM, cols), lambda i: (0, i)),
                  pl.BlockSpec((ncp, HEAD_DIM), lambda i: (0, 0)),
                  pl.BlockSpec((V_ROWS, ncp), lambda i: (0, 0))],
        out_specs=(pl.BlockSpec((n_qb * Q_BLOCK, NSA_W), lambda i: (i, 0)),
                   pl.BlockSpec((1, n_win, AUG, cols), lambda i: (0, 0, 0, i))),
        out_shape=(jax.ShapeDtypeStruct((S, NSA_W), bf16),
                   jax.ShapeDtypeStruct((1, n_win, AUG, NSA_HEADS * S), bf16)),
        compiler_params=_cparams(("arbitrary",)),
        name="nsa_cmp_select",
    )(nq, kc, vc)


def _flash_kernel(qi_t, kb_t, win_t, first_t, last_t, var_t, q_ref, k_ref, v_ref, o_ref, m_sc, acc_sc,
                  *, n_heads, tq, qw, tq_pos, rep_cols, pos_block, window, variants):
    p_id = pl.program_id(0)
    qi = qi_t[p_id]
    kb = kb_t[p_id]

    @pl.when(first_t[p_id] == 1)
    def _():
        m_sc[...] = jnp.full(m_sc.shape, M_FLOOR, f32)
        acc_sc[...] = jnp.zeros_like(acc_sc)

    def tile(modes):
        chains = [(g, slice(c * qw, (c + 1) * qw), slice(kt * FLASH_TK, (kt + 1) * FLASH_TK))
                  for kt in range(FLASH_SUB) if modes[kt] for g in range(n_heads) for c in range(tq // qw)]
        if 2 in modes:
            rel = _mod(_iota2((FLASH_TK, qw), 1), pos_block) - _iota2((FLASH_TK, qw), 0)

        def scores(t):
            g, cols, keys = chains[t]
            s = _dot(k_ref[g, keys, :], q_ref[g, 0, :, cols])
            if modes[keys.start // FLASH_TK] == 1:
                return s
            base = (qi * tq_pos + (cols.start // rep_cols) * pos_block + cols.start % pos_block
                    - kb * (FLASH_SUB * FLASH_TK) - keys.start)
            delta = rel + base
            ok = (delta >= 0) if window is None else (delta.astype(jnp.uint32) < jnp.uint32(window))
            return jnp.where(ok, s, NEG_INF)

        def softmax(t, s):
            g, cols, _ = chains[t]
            m_prev = m_sc[g, 0:1, cols]
            m_new = jnp.maximum(m_prev, jnp.max(s, axis=0, keepdims=True))
            m_sc[g, :, cols] = jnp.broadcast_to(m_new, (m_sc.shape[1], qw))
            return jnp.exp2(s - m_new).astype(bf16), jnp.exp2(m_prev - m_new)

        def accumulate(t, p, alpha):
            g, cols, keys = chains[t]
            acc_sc[g, :, cols] = alpha * acc_sc[g, :, cols] + _dot(v_ref[g, :, keys], p)

        n = len(chains)
        s_next, staged = scores(0), None
        for t in range(n + 1):
            s = s_next
            if t + 1 < n:
                s_next = scores(t + 1)
            if staged is not None:
                accumulate(t - 1, *staged)
            staged = softmax(t, s) if t < n else None

    for vid, modes in enumerate(variants):
        pl.when(var_t[p_id] == vid)(functools.partial(tile, modes))

    @pl.when(last_t[p_id] == 1)
    def _():
        def normalised(g, cols):
            acc = acc_sc[g, :, cols]
            l = acc[HEAD_DIM:HEAD_DIM + 1, :]
            return acc[0:HEAD_DIM, :] * jnp.where(l > 0.0, 1.0 / l, 0.0)

        if n_heads > 1:
            o_ref[...] = _heads_to_lanes([normalised(g, slice(0, tq)) for g in range(n_heads)]).astype(o_ref.dtype)
        else:
            per_block = tq // tq_pos * pos_block
            for qb in range(tq // per_block):
                pieces = [normalised(0, slice(qb * per_block + h * pos_block, qb * per_block + (h + 1) * pos_block))
                          for h in range(per_block // pos_block)]
                o_ref[qb * pos_block:(qb + 1) * pos_block, :] = _heads_to_lanes(pieces).astype(o_ref.dtype)


def _step_tables(n_q, key_tiles, win_keys):
    variants, cols = [], ([], [], [], [], [], [])
    for a in range(n_q):
        tiles = key_tiles(a)
        groups = sorted({t // FLASH_SUB for t in tiles})
        for kb in groups:
            modes = tuple(tiles.get(kb * FLASH_SUB + j, 0) for j in range(FLASH_SUB))
            if modes not in variants:
                variants.append(modes)
            row = (a, kb, (kb * FLASH_SUB * FLASH_TK) // win_keys, int(kb == groups[0]), int(kb == groups[-1]),
                   variants.index(modes))
            for c, v in zip(cols, row):
                c.append(v)
    return tuple(jnp.asarray(np.asarray(c, np.int32)) for c in cols), tuple(variants)


def flash(q, k, v, steps, *, tq, pos_block, window, name):
    tables, variants = steps
    G, _, _, Sq = q.shape
    tq_pos = FLASH_TQ
    rep_cols = (tq // tq_pos) * pos_block
    assert rep_cols % FLASH_QW == 0 and (pos_block % FLASH_QW == 0 or FLASH_QW % pos_block == 0)
    tk = FLASH_SUB * FLASH_TK
    width = max(G, tq // tq_pos) * HEAD_DIM
    assert G == 1 or tq == tq_pos
    grid_spec = pltpu.PrefetchScalarGridSpec(
        num_scalar_prefetch=6,
        grid=(tables[0].shape[0],),
        in_specs=[pl.BlockSpec((G, 1, AUG, tq), lambda p, qi, kb, wi, fi, la, va: (0, wi[p], 0, qi[p])),
                  pl.BlockSpec((G, tk, AUG), lambda p, qi, kb, wi, fi, la, va: (0, kb[p], 0)),
                  pl.BlockSpec((G, V_ROWS, tk), lambda p, qi, kb, wi, fi, la, va: (0, 0, kb[p]))],
        out_specs=pl.BlockSpec((tq_pos, width), lambda p, qi, kb, wi, fi, la, va: (qi[p], 0)),
        scratch_shapes=[pltpu.VMEM((G, 8, tq), f32), pltpu.VMEM((G, V_ROWS, tq), f32)],
    )
    return pl.pallas_call(
        functools.partial(_flash_kernel, n_heads=G, tq=tq, qw=FLASH_QW, tq_pos=tq_pos, rep_cols=rep_cols,
                          pos_block=pos_block, window=window, variants=variants),
        grid_spec=grid_spec,
        out_shape=jax.ShapeDtypeStruct((Sq // (tq // tq_pos), width), bf16),
        compiler_params=_cparams(("arbitrary",)),
        name=name,
    )(*tables, q, k, v)


def _out_kernel(x_ref, gl0_ref, gl1_ref, gl2_ref, sd_ref, bg_ref, om_ref, of_ref, oc_ref, os_ref, ow_ref, zs_ref,
                wm_ref, wn_ref, wf_ref, wo_ref, o_ref):
    zs = zs_ref[...].astype(f32)
    sg = 1.0 / (1.0 + jnp.exp(-sd_ref[...]))
    sg_hi = sg.astype(bf16)
    sg_lo = (sg - sg_hi.astype(f32)).astype(bf16)
    src = _iota2((LANES, NSA_W), 0)
    head = _div(_iota2((LANES, NSA_W), 1), HEAD_DIM)
    o_n = jnp.zeros(oc_ref.shape, f32)
    for branch, ref in enumerate((oc_ref, os_ref, ow_ref)):
        spread = jnp.where(src == 3 * head + branch, 1.0, 0.0).astype(bf16)
        o_n = o_n + (_dot(sg_hi, spread) + _dot(sg_lo, spread)) * ref[...].astype(f32)
    mixers = ((om_ref[...].astype(f32), 0, wm_ref), (o_n, MOBA_W, wn_ref), (of_ref[...].astype(f32), MOBA_W + NSA_W, wf_ref))
    merged = jnp.zeros(x_ref.shape, f32)
    for (o, z_off, w_ref), gl_ref, b in zip(mixers, (gl0_ref, gl1_ref, gl2_ref), range(N_BRANCH)):
        y = _dot((o * zs[:, z_off:z_off + o.shape[1]]).astype(bf16), w_ref[...])
        gate = 1.0 / (1.0 + jnp.exp(-(gl_ref[...].astype(f32) + bg_ref[:, b * D_MODEL:(b + 1) * D_MODEL])))
        merged = merged + y * gate
    o_ref[...] = x_ref[...] + _dot(merged.astype(bf16), wo_ref[...])


def out_proj(x, proj, gate_logits, b_gate, o_m, o_f, o_c, o_s, o_w, zs, w_m, w_n, w_f, w_o, tm=256):
    S = x.shape[0]
    full = lambda a: pl.BlockSpec(a.shape, lambda i: (0,) * a.ndim)
    rows = lambda w: pl.BlockSpec((tm, w), lambda i: (i, 0))
    gl_spec = lambda b: pl.BlockSpec((tm, D_MODEL), lambda i: (i, b))
    return pl.pallas_call(
        _out_kernel,
        grid=(S // tm,),
        in_specs=[rows(D_MODEL), gl_spec(0), gl_spec(1), gl_spec(2),
                  pl.BlockSpec((tm, LANES), lambda i: (i, OFF_D // LANES)),
                  full(b_gate), rows(MOBA_W), rows(FOX_W), rows(NSA_W), rows(NSA_W), rows(NSA_W), rows(SEG_C),
                  full(w_m), full(w_n), full(w_f), full(w_o)],
        out_specs=rows(D_MODEL),
        out_shape=jax.ShapeDtypeStruct((S, D_MODEL), f32),
        compiler_params=_cparams(("arbitrary",)),
        name="out_proj",
    )(x, gate_logits, gate_logits, gate_logits, proj, b_gate, o_m, o_f, o_c, o_s, o_w, zs, w_m, w_n, w_f, w_o)


def _rope_tables(pos, rows):
    half = ROPE_DIM // 2
    inv = ROPE_THETA ** (-jnp.arange(0, ROPE_DIM, 2, dtype=f32) / ROPE_DIM)
    d = np.arange(LANES) % HEAD_DIM
    rotated = d < ROPE_DIM
    inv_row = jnp.where(rotated, inv[d % half], 0.0)
    sign_row = np.where(d < half, -1.0, 1.0).astype(np.float32)
    ang = pos.astype(f32)[:, None] * inv_row[None, :]
    pad = ((0, rows - pos.shape[0]), (0, 0))
    return jnp.pad(jnp.cos(ang), pad), jnp.pad(jnp.sin(ang) * sign_row[None, :], pad)


def _layer(x, norm_g, w_main, b_f, b_gate, moba_qk_g, nsa_q_g, nsa_k_g, fox_qk_g,
           cmp_pe, cmp_w1, cmp_w2, w_up_moba, w_up_nsa, w_up_fox, w_out, tables, rope):
    S = x.shape[0]
    g_row = norm_g.reshape(1, D_MODEL)
    proj = rms_matmul(x, g_row, w_main, 0, MAIN_COLS, f32)
    gate_logits = rms_matmul(x, g_row, w_main, MAIN_COLS, N_BRANCH * D_MODEL, bf16)

    gain_row = jnp.concatenate([jnp.tile(moba_qk_g[0], MOBA_HEADS), jnp.tile(moba_qk_g[1], MOBA_HEADS),
                                jnp.tile(nsa_q_g, NSA_HEADS), nsa_k_g[1], nsa_k_g[2],
                                jnp.tile(fox_qk_g[0], FOX_HEADS), jnp.tile(fox_qk_g[1], FOX_HEADS)]).reshape(1, SEG_A)
    bf_row = jnp.zeros((1, LANES), f32).at[0, FF_LANE:FF_LANE + FOX_HEADS].set(b_f)
    (mq, mk, nq, ksl, kw, fq, fk, mv, fv, vsl, vw, zs) = prep(proj, rope[0], rope[1], gain_row, bf_row)

    q_moba = moba_select(mq, moba_kmean(mk))
    o_m = flash(q_moba, mk, mv, tables["causal"], tq=FLASH_TQ, pos_block=FLASH_TQ, window=None, name="flash_moba")

    pe = jnp.concatenate([cmp_pe[0], cmp_pe[1]], axis=1)
    w1 = cmp_w1.astype(bf16).reshape(2, NSA_CMP_LEN, HEAD_DIM, NSA_CMP_HIDDEN)
    zero = jnp.zeros_like(w1[0])
    w1 = jnp.concatenate([jnp.concatenate([w1[0], zero], axis=2),
                          jnp.concatenate([zero, w1[1]], axis=2)], axis=1)
    w2p = jnp.pad(cmp_w2, ((0, 0), (0, 0), (0, LANES - HEAD_DIM))).astype(bf16)
    gain_c = jnp.pad(nsa_k_g[0], (0, LANES - HEAD_DIM)).reshape(1, LANES)
    kc, vc = nsa_compress(proj, pe, w1, w2p, gain_c, rope[2], rope[3])
    o_c, q_slc = nsa_cmp_select(nq, kc, vc, S)
    nsa_cols = NSA_HEADS * FLASH_TQ
    o_s = flash(q_slc, ksl[None], vsl[None], tables["slc"], tq=nsa_cols,
                pos_block=Q_BLOCK, window=None, name="flash_nsa_slc")
    o_w = flash(q_slc, kw[None], vw[None], tables["win"], tq=nsa_cols,
                pos_block=Q_BLOCK, window=NSA_WINDOW, name="flash_nsa_win")

    o_f = flash(fq[:, None], fk, fv, tables["causal"], tq=FLASH_TQ, pos_block=FLASH_TQ, window=None, name="flash_fox")
    return out_proj(x, proj, gate_logits, b_gate.reshape(1, N_BRANCH * D_MODEL), o_m, o_f, o_c, o_s, o_w, zs,
                    w_up_moba.astype(bf16), w_up_nsa.astype(bf16), w_up_fox.astype(bf16), w_out.astype(bf16))


def kernel(x, norm_g, w_in, b_f, b_gate, moba_qk_g, nsa_q_g, nsa_k_g, fox_qk_g, cmp_pe, cmp_w1, cmp_w2,
           w_up_moba, w_up_nsa, w_up_fox, w_out):
    B, S, _ = x.shape
    assert B == 1 and S % 2048 == 0 and S // MOBA_BLOCK <= CODE_BLOCKS
    depth = norm_g.shape[0]
    win_keys = CODE_BLOCKS * NSA_SLC_BLOCK
    n_q = S // FLASH_TQ
    causal = lambda a: {t: (2 if (t + 1) * FLASH_TK > a * FLASH_TQ else 1)
                        for t in range((a + 1) * FLASH_TQ // FLASH_TK)}
    band = lambda a: {t: 2 for t in range(max(0, a * FLASH_TQ - NSA_WINDOW + 1) // FLASH_TK,
                                          (a + 1) * FLASH_TQ // FLASH_TK)}
    tables = {"causal": _step_tables(n_q, causal, S), "slc": _step_tables(n_q, causal, win_keys),
              "win": _step_tables(n_q, band, S)}
    ncp = S // NSA_CMP_STRIDE
    cos_t, sin_t = _rope_tables(jnp.arange(S), S)
    cmp_end = jnp.arange(ncp - 1) * NSA_CMP_STRIDE + (NSA_CMP_LEN - 1)
    cos_c, sin_c = _rope_tables(cmp_end, ncp)
    rope = (cos_t, sin_t, cos_c, sin_c)
    h = x[0]
    for l in range(depth):
        h = _layer(h, norm_g[l], repack_w_in(w_in, l), b_f[l], b_gate[l], moba_qk_g[l], nsa_q_g[l], nsa_k_g[l], fox_qk_g[l],
                   cmp_pe[l], cmp_w1[l], cmp_w2[l], w_up_moba[l], w_up_nsa[l], w_up_fox[l], w_out[l], tables, rope)
    return h[None]
```

```python
import functools

import numpy as np
import jax
import jax.numpy as jnp
from jax import lax
from jax.experimental import pallas as pl
from jax.experimental.pallas import tpu as pltpu

D_MODEL = 1024
HEAD_DIM = 64
ROPE_DIM = HEAD_DIM // 4
ROPE_THETA = 500000.0
RMS_EPS = 1e-6
NEG_INF = -1e30
MASKED_BELOW = 0.5 * NEG_INF
M_FLOOR = -1e20

MOBA_HEADS = 6
MOBA_BLOCK = 256
MOBA_TOPK = 3
NSA_HEADS = 4
NSA_CMP_LEN = 32
NSA_CMP_STRIDE = 16
NSA_CMP_HIDDEN = 4 * HEAD_DIM
NSA_SLC_BLOCK = 64
NSA_SLC_TOPN = 16
NSA_WINDOW = 512
FOX_HEADS = 6
N_BRANCH = 3
MOBA_W = MOBA_HEADS * HEAD_DIM
NSA_W = NSA_HEADS * HEAD_DIM
FOX_W = FOX_HEADS * HEAD_DIM
IN_SPLITS = (MOBA_W,) * 4 + (NSA_W,) + (HEAD_DIM,) * 6 + (3 * NSA_HEADS, NSA_W) + (FOX_W,) * 3 + (FOX_HEADS, FOX_W, N_BRANCH * D_MODEL)
ATTN_SCALE = HEAD_DIM ** -0.5

LANES = 128
AUG = 2 * HEAD_DIM
CODE_BLOCKS = AUG - HEAD_DIM
Q_BLOCK = 128

SEG_A = 2 * MOBA_W + NSA_W + 2 * HEAD_DIM + 2 * FOX_W
SEG_B = 2 * MOBA_W + 4 * HEAD_DIM
SEG_C = MOBA_W + NSA_W + FOX_W
SEG_D = LANES
OFF_B = SEG_A
OFF_C = OFF_B + SEG_B
OFF_D = OFF_C + SEG_C
MAIN_COLS = OFF_D + SEG_D
FF_LANE = 3 * NSA_HEADS
KV_CMP_OFF = OFF_B + 2 * MOBA_W + 2 * HEAD_DIM
ALL_COLS = MAIN_COLS + N_BRANCH * D_MODEL

VMEM_LIMIT = 56 * 1024 * 1024
FLASH_QW = 512
FLASH_TQ = 512
FLASH_TK = 256
FLASH_SUB = 8
LOG2E = 1.4426950408889634
Q_SCALE = ATTN_SCALE * LOG2E
V_ROWS = 80

f32 = jnp.float32
bf16 = jnp.bfloat16


def _cparams(sem):
    return pltpu.CompilerParams(dimension_semantics=sem, vmem_limit_bytes=VMEM_LIMIT)


def _iota2(shape, dim):
    return lax.broadcasted_iota(jnp.int32, shape, dim)


def _div(x, d):
    return jnp.right_shift(x, int(d).bit_length() - 1)


def _mod(x, d):
    return jnp.bitwise_and(x, d - 1)


def _place(n_in, n_out, in_off, out_off, width=HEAD_DIM, val=1.0):
    r = _iota2((n_in, n_out), 0) - in_off
    c = _iota2((n_in, n_out), 1) - out_off
    hit = (r == c) & (r >= 0) & (r < width)
    return jnp.where(hit, val, 0.0).astype(bf16)


def _place_t(n_out, n_in, in_off, out_off, width=HEAD_DIM, val=1.0):
    r = _iota2((n_out, n_in), 0) - out_off
    c = _iota2((n_out, n_in), 1) - in_off
    hit = (r == c) & (r >= 0) & (r < width)
    return jnp.where(hit, val, 0.0).astype(bf16)


def _split3(x):
    hi = x.astype(bf16)
    r = x - hi.astype(f32)
    mid = r.astype(bf16)
    lo = (r - mid.astype(f32)).astype(bf16)
    return hi, mid, lo


def _heads_to_lanes(pieces):
    rows = len(pieces) * HEAD_DIM
    pad = [jnp.zeros((-rows % LANES, pieces[0].shape[1]), f32)] if rows % LANES else []
    return jnp.concatenate(list(pieces) + pad, axis=0).T[:, :rows]


def _dot(a, b):
    return jnp.dot(a, b, preferred_element_type=f32)


def _dot_nt(a, b):
    return lax.dot_general(a, b, (((1,), (1,)), ((), ())), preferred_element_type=f32)


def _w_in_plan():
    offs = np.concatenate([[0], np.cumsum(IN_SPLITS)])
    names = ("mq", "mk", "mv", "mz", "nq", "kc", "vc", "ksl", "vsl", "kw", "vw", "ng", "nz", "fq", "fk", "fv", "ff", "fz", "gl")
    start = {n: int(offs[j]) for j, n in enumerate(names)}
    width = {n: int(IN_SPLITS[j]) for j, n in enumerate(names)}
    order = ("mq", "mk", "nq", "ksl", "kw", "fq", "fk", "mv", "fv", "vsl", "vw", "kc", "vc", "mz", "nz", "fz", "ng", "ff")
    plan = [[] for _ in range(ALL_COLS // LANES)]
    new = 0
    for n in order + ("pad", "gl"):
        if n == "pad":
            new = MAIN_COLS
            continue
        src, left = start[n], width[n]
        while left > 0:
            w = min(left, LANES - new % LANES, LANES - src % LANES)
            plan[new // LANES].append((src // LANES, src % LANES, new % LANES, w))
            src, new, left = src + w, new + w, left - w
    assert new == ALL_COLS
    return plan


def _repack_kernel(w_ref, o_ref, *, plan, tr, n_cols):
    lane = _iota2((tr, LANES), 1)
    loaded = {}

    def source(a):
        if a not in loaded:
            x = w_ref[0, :, a * LANES:(a + 1) * LANES]
            if (a + 1) * LANES > n_cols:
                x = jnp.where(lane < n_cols - a * LANES, x, 0.0)
            loaded[a] = x.astype(bf16)
        return loaded[a]

    for b, pieces in enumerate(plan):
        acc = jnp.zeros((tr, LANES), f32)
        for a, lane_in, lane_out, w in pieces:
            acc = acc + _dot(source(a), _place(LANES, LANES, lane_in, lane_out, width=w))
        o_ref[:, b * LANES:(b + 1) * LANES] = acc.astype(bf16)


def repack_w_in(w_in, layer, tr=128):
    _, D, n_cols = w_in.shape
    padded = pl.cdiv(n_cols, LANES) * LANES
    return pl.pallas_call(
        functools.partial(_repack_kernel, plan=_w_in_plan(), tr=tr, n_cols=n_cols),
        grid=(D // tr,),
        in_specs=[pl.BlockSpec((1, tr, padded), lambda i: (layer, i, 0))],
        out_specs=pl.BlockSpec((tr, ALL_COLS), lambda i: (i, 0)),
        out_shape=jax.ShapeDtypeStruct((D, ALL_COLS), bf16),
        compiler_params=_cparams(("arbitrary",)),
        name="repack_w_in",
    )(w_in)


def _rms_matmul_kernel(x_ref, g_ref, w_ref, o_ref, h_sc):
    @pl.when(pl.program_id(1) == 0)
    def _():
        x = x_ref[...]
        ms = jnp.mean(x * x, axis=-1, keepdims=True)
        h_sc[...] = (x * lax.rsqrt(ms + RMS_EPS) * g_ref[...]).astype(bf16)

    o_ref[...] = _dot(h_sc[...], w_ref[...]).astype(o_ref.dtype)


def rms_matmul(x, g, w, col0, n_cols, dtype, tn, tm=1024):
    S, D = x.shape
    assert col0 % tn == 0 and n_cols % tn == 0
    return pl.pallas_call(
        _rms_matmul_kernel,
        grid=(S // tm, n_cols // tn),
        in_specs=[pl.BlockSpec((tm, D), lambda i, j: (i, 0)),
                  pl.BlockSpec((1, D), lambda i, j: (0, 0)),
                  pl.BlockSpec((D, tn), lambda i, j: (0, col0 // tn + j))],
        out_specs=pl.BlockSpec((tm, tn), lambda i, j: (i, j)),
        out_shape=jax.ShapeDtypeStruct((S, n_cols), dtype),
        scratch_shapes=[pltpu.VMEM((tm, D), bf16)],
        compiler_params=_cparams(("arbitrary", "arbitrary")),
        name="rms_matmul",
    )(x, g, w)


def _prep_kernel(p_ref, cos_ref, sin_ref, gain_ref, bf_ref,
                 mq_ref, mk_ref, nq_ref, ksl_ref, kw_ref, fq_ref, fk_ref,
                 mv_ref, fv_ref, vsl_ref, vw_ref, zs_ref, carry_sc, *, ts):
    i = pl.program_id(0)

    @pl.when(i == 0)
    def _():
        carry_sc[...] = jnp.zeros_like(carry_sc)

    lane = _iota2((ts, LANES), 1)
    pos = _iota2((ts, LANES), 0) + i * ts
    blockdiag = jnp.where(_div(_iota2((LANES, LANES), 0), HEAD_DIM) == _div(_iota2((LANES, LANES), 1), HEAD_DIM),
                          1.0, 0.0).astype(bf16)
    first_half = _mod(lane, HEAD_DIM) < (ROPE_DIM // 2)
    low_lanes = lane < HEAD_DIM
    cos = cos_ref[...]
    sin = sin_ref[...]

    def normed(c, rope, scale=None):
        x = p_ref[:, c * LANES:(c + 1) * LANES]
        x2 = x * x
        hi = x2.astype(bf16)
        lo = (x2 - hi.astype(f32)).astype(bf16)
        ss = _dot(hi, blockdiag) + _dot(lo, blockdiag)
        y = x * lax.rsqrt(ss * (1.0 / HEAD_DIM) + RMS_EPS) * gain_ref[:, c * LANES:(c + 1) * LANES]
        if rope:
            up = pltpu.roll(y, LANES - ROPE_DIM // 2, 1)
            dn = pltpu.roll(y, ROPE_DIM // 2, 1)
            y = y * cos + jnp.where(first_half, up, dn) * sin
        return y if scale is None else y * scale

    def head_rows(y_t, half):
        return y_t[half * HEAD_DIM:(half + 1) * HEAD_DIM, :]

    def head_lanes(y, half):
        return y if half == 0 else pltpu.roll(y, HEAD_DIM, 1)

    d = p_ref[:, OFF_D:OFF_D + LANES] + bf_ref[...]
    logf = jnp.minimum(d, 0.0) - jnp.log(1.0 + jnp.exp(-jnp.abs(d)))
    tri = jnp.where(_iota2((ts, ts), 1) <= _iota2((ts, ts), 0), 1.0, 0.0).astype(bf16)
    lh, lm, ll = _split3(logf)
    c = carry_sc[0:1, :] + (_dot(tri, lh) + _dot(tri, lm) + _dot(tri, ll))
    carry_sc[...] = jnp.broadcast_to(c[ts - 1:ts, :], carry_sc.shape)
    c2 = c * LOG2E
    pieces = [p.astype(f32) for p in _split3(c2)]
    pieces_t = [p.astype(f32) for p in _split3(c2.T)]
    row64 = _iota2((HEAD_DIM, ts), 0)

    def decay_rows(h):
        r = FF_LANE + h
        hi, mid, lo = (jnp.broadcast_to(p[r:r + 1, :], (HEAD_DIM, ts)) for p in pieces_t)
        return jnp.where(row64 == 0, hi, jnp.where(row64 == 1, mid, jnp.where(row64 == 2, lo,
                         jnp.where(row64 < 6, 1.0, 0.0))))

    def decay_lanes(h):
        r = FF_LANE + h
        hi, mid, lo = (jnp.broadcast_to(p[:, r:r + 1], (ts, LANES)) for p in pieces)
        return jnp.where(lane == HEAD_DIM + 3, -hi, jnp.where(lane == HEAD_DIM + 4, -mid,
                         jnp.where(lane == HEAD_DIM + 5, -lo, jnp.where(lane < HEAD_DIM + 3, 1.0, 0.0))))

    moba_code = jnp.where((lane - HEAD_DIM) == _div(pos, MOBA_BLOCK), 1.0, 0.0)
    for c_i in range(3):
        yq_t = normed(c_i, True, Q_SCALE).T
        yk = normed(3 + c_i, True)
        for half in range(2):
            h = 2 * c_i + half
            mq_ref[h] = head_rows(yq_t, half).astype(bf16)
            mk_ref[h] = jnp.where(low_lanes, head_lanes(yk, half), moba_code).astype(bf16)
    for c_i in range(2):
        y_t = normed(6 + c_i, True, Q_SCALE).T
        for half in range(2):
            h = 2 * c_i + half
            yh = head_rows(y_t, half).astype(bf16)
            for qb in range(ts // Q_BLOCK):
                nq_ref[:, (qb * NSA_HEADS + h) * Q_BLOCK:(qb * NSA_HEADS + h + 1) * Q_BLOCK] = (
                    yh[:, qb * Q_BLOCK:(qb + 1) * Q_BLOCK])
    y = normed(8, True)
    slc_code = jnp.where((lane - HEAD_DIM) == _mod(_div(pos, NSA_SLC_BLOCK), CODE_BLOCKS), 1.0, 0.0)
    ksl_ref[...] = jnp.where(low_lanes, y, slc_code).astype(bf16)
    kw_ref[...] = jnp.where(low_lanes, head_lanes(y, 1), 0.0).astype(bf16)
    for c_i in range(3):
        yq_t = normed(9 + c_i, False, Q_SCALE).T
        yk = normed(12 + c_i, False)
        for half in range(2):
            h = 2 * c_i + half
            fq_ref[h] = jnp.concatenate([head_rows(yq_t, half), decay_rows(h)], axis=0).astype(bf16)
            fk_ref[h] = jnp.where(low_lanes, head_lanes(yk, half), decay_lanes(h)).astype(bf16)
    ones_rows = jnp.where(_iota2((V_ROWS - HEAD_DIM, ts), 0) == 0, 1.0, 0.0)

    def value_rows(x_t, half):
        return jnp.concatenate([head_rows(x_t, half), ones_rows], axis=0).astype(bf16)

    for c_i in range(3):
        xm_t = p_ref[:, OFF_B + c_i * LANES:OFF_B + (c_i + 1) * LANES].T
        xf_t = p_ref[:, OFF_B + MOBA_W + c_i * LANES:OFF_B + MOBA_W + (c_i + 1) * LANES].T
        for half in range(2):
            mv_ref[2 * c_i + half] = value_rows(xm_t, half)
            fv_ref[2 * c_i + half] = value_rows(xf_t, half)
    xs_t = p_ref[:, OFF_B + 2 * MOBA_W:OFF_B + 2 * MOBA_W + LANES].T
    vsl_ref[...] = value_rows(xs_t, 0)
    vw_ref[...] = value_rows(xs_t, 1)
    z = p_ref[:, OFF_C:OFF_C + SEG_C]
    zs_ref[...] = (z * (1.0 / (1.0 + jnp.exp(-z)))).astype(bf16)


def prep(proj, cos_t, sin_t, gain_row, bf_row, ts=512):
    S = proj.shape[0]
    head128 = lambda n: jax.ShapeDtypeStruct((n, S, AUG), bf16)
    spec_h = lambda n, w: pl.BlockSpec((n, ts, w), lambda i: (0, i, 0))
    spec_r = lambda w: pl.BlockSpec((ts, w), lambda i: (i, 0))
    head_t = lambda n, r: jax.ShapeDtypeStruct((n, r, S), bf16)
    spec_ht = lambda n, r: pl.BlockSpec((n, r, ts), lambda i: (0, 0, i))
    spec_t = pl.BlockSpec((V_ROWS, ts), lambda i: (0, i))
    out_shape = (head_t(MOBA_HEADS, HEAD_DIM), head128(MOBA_HEADS),
                 jax.ShapeDtypeStruct((HEAD_DIM, NSA_HEADS * S), bf16),
                 jax.ShapeDtypeStruct((S, AUG), bf16), jax.ShapeDtypeStruct((S, AUG), bf16),
                 head_t(FOX_HEADS, AUG), head128(FOX_HEADS),
                 head_t(MOBA_HEADS, V_ROWS), head_t(FOX_HEADS, V_ROWS),
                 jax.ShapeDtypeStruct((V_ROWS, S), bf16), jax.ShapeDtypeStruct((V_ROWS, S), bf16),
                 jax.ShapeDtypeStruct((S, SEG_C), bf16))
    out_specs = (spec_ht(MOBA_HEADS, HEAD_DIM), spec_h(MOBA_HEADS, AUG),
                 pl.BlockSpec((HEAD_DIM, NSA_HEADS * ts), lambda i: (0, i)),
                 spec_r(AUG), spec_r(AUG),
                 spec_ht(FOX_HEADS, AUG), spec_h(FOX_HEADS, AUG),
                 spec_ht(MOBA_HEADS, V_ROWS), spec_ht(FOX_HEADS, V_ROWS),
                 spec_t, spec_t,
                 spec_r(SEG_C))
    return pl.pallas_call(
        functools.partial(_prep_kernel, ts=ts),
        grid=(S // ts,),
        in_specs=[pl.BlockSpec((ts, MAIN_COLS), lambda i: (i, 0)),
                  spec_r(LANES), spec_r(LANES),
                  pl.BlockSpec((1, SEG_A), lambda i: (0, 0)),
                  pl.BlockSpec((1, LANES), lambda i: (0, 0))],
        out_specs=out_specs,
        out_shape=out_shape,
        scratch_shapes=[pltpu.VMEM((8, LANES), f32)],
        compiler_params=_cparams(("arbitrary",)),
        name="prep",
    )(proj, cos_t, sin_t, gain_row, bf_row)


def _kmean_kernel(k_ref, o_ref, *, rows):
    n = rows // MOBA_BLOCK
    avg = jnp.where(_div(_iota2((n, rows), 1), MOBA_BLOCK) == _iota2((n, rows), 0),
                    1.0 / MOBA_BLOCK, 0.0).astype(bf16)
    o_ref[0] = _dot(avg, k_ref[0])[:, :HEAD_DIM]


def moba_kmean(mk_aug):
    H, S, _ = mk_aug.shape
    rows = 8 * MOBA_BLOCK
    return pl.pallas_call(
        functools.partial(_kmean_kernel, rows=rows),
        grid=(H, S // rows),
        in_specs=[pl.BlockSpec((1, rows, AUG), lambda h, i: (h, i, 0))],
        out_specs=pl.BlockSpec((1, 8, HEAD_DIM), lambda h, i: (h, i, 0)),
        out_shape=jax.ShapeDtypeStruct((H, CODE_BLOCKS, HEAD_DIM), f32),
        compiler_params=_cparams(("arbitrary", "arbitrary")),
        name="moba_kmean",
    )(mk_aug)


def _top_select(scores, idx, n_pick, floor):
    big = jnp.int32(2 ** 30)
    scores = list(scores)
    for _ in range(n_pick):
        for j, score in enumerate(scores):
            m = jnp.max(score, axis=0, keepdims=True)
            first = jnp.min(jnp.where(score == m, idx, big), axis=0, keepdims=True)
            first = jnp.where(m > floor, first, big)
            scores[j] = jnp.where(idx == first, -jnp.inf, score)
    return [s == -jnp.inf for s in scores]


def _moba_select_kernel(q_ref, km_ref, o_ref, *, tq, n_heads):
    i = pl.program_id(0)
    blk = _iota2((CODE_BLOCKS, tq), 0)
    cur = _div(_iota2((CODE_BLOCKS, tq), 1) + i * tq, MOBA_BLOCK)
    scores = []
    for h in range(n_heads):
        q = q_ref[h]
        km = km_ref[h]
        km_hi = km.astype(bf16)
        km_lo = (km - km_hi.astype(f32)).astype(bf16)
        gate = _dot(km_hi, q) + _dot(km_lo, q)
        scores.append(jnp.where(blk < cur, gate, NEG_INF))
    for h, sel in enumerate(_top_select(scores, blk, MOBA_TOPK, MASKED_BELOW)):
        o_ref[h, 0, 0:HEAD_DIM, :] = q_ref[h]
        o_ref[h, 0, HEAD_DIM:AUG, :] = jnp.where(sel | (blk == cur), 0.0, NEG_INF).astype(bf16)


def moba_select(mq, kmean, tq=512):
    H, _, S = mq.shape
    return pl.pallas_call(
        functools.partial(_moba_select_kernel, tq=tq, n_heads=H),
        grid=(S // tq,),
        in_specs=[pl.BlockSpec((H, HEAD_DIM, tq), lambda i: (0, 0, i)),
                  pl.BlockSpec((H, CODE_BLOCKS, HEAD_DIM), lambda i: (0, 0, 0))],
        out_specs=pl.BlockSpec((H, 1, AUG, tq), lambda i: (0, 0, 0, i)),
        out_shape=jax.ShapeDtypeStruct((H, 1, AUG, S), bf16),
        compiler_params=_cparams(("arbitrary",)),
        name="moba_select",
    )(mq, kmean)


def _cmp_kernel(x_ref, pe_ref, w1_ref, w2_ref, gain_ref, cos_ref, sin_ref, kc_ref, vc_ref, *, ncp):
    top = jnp.zeros((ncp, 2 * NSA_CMP_HIDDEN), f32)
    nxt = jnp.zeros((ncp, 2 * NSA_CMP_HIDDEN), f32)
    for j in range(NSA_CMP_STRIDE):
        xj = x_ref[pl.ds(j, ncp, stride=NSA_CMP_STRIDE), :]
        top = top + _dot((xj + pe_ref[j:j + 1, :]).astype(bf16), w1_ref[j])
        nxt = nxt + _dot((xj + pe_ref[NSA_CMP_STRIDE + j:NSA_CMP_STRIDE + j + 1, :]).astype(bf16),
                         w1_ref[NSA_CMP_STRIDE + j])
    hid = top + pltpu.roll(nxt, ncp - 1, 0)
    act = (hid * (1.0 / (1.0 + jnp.exp(-hid)))).astype(bf16)
    k = _dot(act[:, :NSA_CMP_HIDDEN], w2_ref[0])
    v = _dot(act[:, NSA_CMP_HIDDEN:], w2_ref[1])
    ms = jnp.sum(k * k, axis=-1, keepdims=True) * (1.0 / HEAD_DIM)
    y = k * lax.rsqrt(ms + RMS_EPS) * gain_ref[...]
    lane = _iota2((ncp, LANES), 1)
    up = pltpu.roll(y, LANES - ROPE_DIM // 2, 1)
    dn = pltpu.roll(y, ROPE_DIM // 2, 1)
    y = y * cos_ref[...] + jnp.where(lane < ROPE_DIM // 2, up, dn) * sin_ref[...]
    kc_ref[...] = y[:, :HEAD_DIM].astype(bf16)
    vc_ref[...] = _dot_nt(_place_t(V_ROWS, LANES, 0, 0), v.astype(bf16)).astype(bf16)


def nsa_compress(proj, pe, w1, w2, gain, cos_c, sin_c):
    S = proj.shape[0]
    ncp = S // NSA_CMP_STRIDE
    full = lambda a: pl.BlockSpec(a.shape, lambda i: (0,) * a.ndim)
    return pl.pallas_call(
        functools.partial(_cmp_kernel, ncp=ncp),
        grid=(1,),
        in_specs=[pl.BlockSpec((S, LANES), lambda i: (0, KV_CMP_OFF // LANES)),
                  full(pe), full(w1), full(w2), full(gain), full(cos_c), full(sin_c)],
        out_specs=(pl.BlockSpec((ncp, HEAD_DIM), lambda i: (0, 0)), pl.BlockSpec((V_ROWS, ncp), lambda i: (0, 0))),
        out_shape=(jax.ShapeDtypeStruct((ncp, HEAD_DIM), bf16), jax.ShapeDtypeStruct((V_ROWS, ncp), bf16)),
        compiler_params=_cparams(("arbitrary",)),
        name="nsa_compress",
    )(proj, pe, w1, w2, gain, cos_c, sin_c)


def _nsa_cmp_select_kernel(q_ref, kc_ref, vc_ref, oc_ref, qa_ref, *, ncp, nsp, n_win, n_qb):
    i = pl.program_id(0)
    cols = NSA_HEADS * Q_BLOCK
    kc = kc_ref[...]
    vc = vc_ref[...]
    c0 = _iota2((nsp, ncp), 1) * NSA_CMP_STRIDE
    b0 = _iota2((nsp, ncp), 0) * NSA_SLC_BLOCK
    overlap = jnp.where((c0 <= b0 + (NSA_SLC_BLOCK - 1)) & (c0 + (NSA_CMP_LEN - 1) >= b0), 1.0, 0.0).astype(bf16)
    kend = _iota2((ncp, cols), 0) * NSA_CMP_STRIDE + (NSA_CMP_LEN - 1)
    col_pos = _mod(_iota2((ncp, cols), 1), Q_BLOCK)
    blk = _iota2((nsp, Q_BLOCK), 0)

    def scores(b):
        return _dot(kc, q_ref[:, b * cols:(b + 1) * cols])

    def probs(b, s):
        ok = kend <= (i * n_qb + b) * Q_BLOCK + col_pos
        sm = jnp.where(ok, s, NEG_INF)
        m = jnp.max(sm, axis=0, keepdims=True)
        e = jnp.exp2(sm - m)
        return e * jnp.where(m > MASKED_BELOW, 1.0 / jnp.sum(e, axis=0, keepdims=True), 0.0)

    def cur(b):
        return _div(_iota2((nsp, Q_BLOCK), 1) + (i * n_qb + b) * Q_BLOCK, NSA_SLC_BLOCK)

    def forced(b):
        return (blk == 0) | (blk == cur(b)) | (blk == cur(b) - 1)

    def importance(b, p):
        o = _dot(vc, p.astype(bf16))
        oc_ref[b * Q_BLOCK:(b + 1) * Q_BLOCK, :] = _heads_to_lanes(
            [o[0:HEAD_DIM, h * Q_BLOCK:(h + 1) * Q_BLOCK] for h in range(NSA_HEADS)]).astype(bf16)
        psum = (p[:, 0:Q_BLOCK] + p[:, Q_BLOCK:2 * Q_BLOCK]
                + p[:, 2 * Q_BLOCK:3 * Q_BLOCK] + p[:, 3 * Q_BLOCK:4 * Q_BLOCK])
        ph, pm, plo = _split3(psum)
        imp = _dot(overlap, ph) + _dot(overlap, pm) + _dot(overlap, plo)
        return jnp.where((blk <= cur(b)) & jnp.logical_not(forced(b)), imp, NEG_INF)

    block_scores = []
    s_next = scores(0)
    for b in range(n_qb):
        s = s_next
        if b + 1 < n_qb:
            s_next = scores(b + 1)
        block_scores.append(importance(b, probs(b, s)))
    n_free = NSA_SLC_TOPN - 3
    for b, sel in enumerate(_top_select(block_scores, blk, n_free, MASKED_BELOW)):
        bias = jnp.where(sel | forced(b), 0.0, NEG_INF).astype(bf16)
        for w in range(n_win):
            bw = bias[w * CODE_BLOCKS:(w + 1) * CODE_BLOCKS, :]
            qa_ref[0, w, 0:HEAD_DIM, b * cols:(b + 1) * cols] = q_ref[:, b * cols:(b + 1) * cols]
            qa_ref[0, w, HEAD_DIM:AUG, b * cols:(b + 1) * cols] = jnp.concatenate([bw] * NSA_HEADS, axis=1)


def nsa_cmp_select(nq, kc, vc, S, n_qb=4):
    ncp = kc.shape[0]
    ns = S // NSA_SLC_BLOCK
    nsp = max(LANES, ns)
    n_win = max(1, ns // CODE_BLOCKS)
    cols = n_qb * NSA_HEADS * Q_BLOCK
    return pl.pallas_call(
        functools.partial(_nsa_cmp_select_kernel, ncp=ncp, nsp=nsp, n_win=n_win, n_qb=n_qb),
        grid=(S // (n_qb * Q_BLOCK),),
        in_specs=[pl.BlockSpec((HEAD_DIM, cols), lambda i: (0, i)),
                  pl.BlockSpec((ncp, HEAD_DIM), lambda i: (0, 0)),
                  pl.BlockSpec((V_ROWS, ncp), lambda i: (0, 0))],
        out_specs=(pl.BlockSpec((n_qb * Q_BLOCK, NSA_W), lambda i: (i, 0)),
                   pl.BlockSpec((1, n_win, AUG, cols), lambda i: (0, 0, 0, i))),
        out_shape=(jax.ShapeDtypeStruct((S, NSA_W), bf16),
                   jax.ShapeDtypeStruct((1, n_win, AUG, NSA_HEADS * S), bf16)),
        compiler_params=_cparams(("arbitrary",)),
        name="nsa_cmp_select",
    )(nq, kc, vc)


def _flash_kernel(qi_t, kb_t, win_t, first_t, last_t, var_t, q_ref, k_ref, v_ref, o_ref, m_sc, acc_sc,
                  *, n_heads, tq, qw, tq_pos, rep_cols, pos_block, window, variants):
    p_id = pl.program_id(0)
    qi = qi_t[p_id]
    kb = kb_t[p_id]

    @pl.when(first_t[p_id] == 1)
    def _():
        m_sc[...] = jnp.full(m_sc.shape, M_FLOOR, f32)
        acc_sc[...] = jnp.zeros_like(acc_sc)

    def tile(modes):
        chains = [(g, slice(c * qw, (c + 1) * qw), slice(kt * FLASH_TK, (kt + 1) * FLASH_TK))
                  for kt in range(FLASH_SUB) if modes[kt] for g in range(n_heads) for c in range(tq // qw)]
        if 2 in modes:
            rel = _mod(_iota2((FLASH_TK, qw), 1), pos_block) - _iota2((FLASH_TK, qw), 0)

        def scores(t):
            g, cols, keys = chains[t]
            s = _dot(k_ref[g, keys, :], q_ref[g, 0, :, cols])
            if modes[keys.start // FLASH_TK] == 1:
                return s
            base = (qi * tq_pos + (cols.start // rep_cols) * pos_block + cols.start % pos_block
                    - kb * (FLASH_SUB * FLASH_TK) - keys.start)
            delta = rel + base
            ok = (delta >= 0) if window is None else (delta.astype(jnp.uint32) < jnp.uint32(window))
            return jnp.where(ok, s, NEG_INF)

        def softmax(t, s):
            g, cols, _ = chains[t]
            m_prev = m_sc[g, 0:1, cols]
            m_new = jnp.maximum(m_prev, jnp.max(s, axis=0, keepdims=True))
            m_sc[g, :, cols] = jnp.broadcast_to(m_new, (m_sc.shape[1], qw))
            return jnp.exp2(s - m_new).astype(bf16), jnp.exp2(m_prev - m_new)

        def accumulate(t, p, alpha):
            g, cols, keys = chains[t]
            acc_sc[g, :, cols] = alpha * acc_sc[g, :, cols] + _dot(v_ref[g, :, keys], p)

        n = len(chains)
        s_next, staged = scores(0), None
        for t in range(n + 1):
            s = s_next
            if t + 1 < n:
                s_next = scores(t + 1)
            if staged is not None:
                accumulate(t - 1, *staged)
            staged = softmax(t, s) if t < n else None

    for vid, modes in enumerate(variants):
        pl.when(var_t[p_id] == vid)(functools.partial(tile, modes))

    @pl.when(last_t[p_id] == 1)
    def _():
        def normalised(g, cols):
            acc = acc_sc[g, :, cols]
            l = acc[HEAD_DIM:HEAD_DIM + 1, :]
            return acc[0:HEAD_DIM, :] * jnp.where(l > 0.0, 1.0 / l, 0.0)

        if n_heads > 1:
            o_ref[...] = _heads_to_lanes([normalised(g, slice(0, tq)) for g in range(n_heads)]).astype(o_ref.dtype)
        else:
            per_block = tq // tq_pos * pos_block
            for qb in range(tq // per_block):
                pieces = [normalised(0, slice(qb * per_block + h * pos_block, qb * per_block + (h + 1) * pos_block))
                          for h in range(per_block // pos_block)]
                o_ref[qb * pos_block:(qb + 1) * pos_block, :] = _heads_to_lanes(pieces).astype(o_ref.dtype)


def _step_tables(n_q, key_tiles, win_keys):
    variants, cols = [], ([], [], [], [], [], [])
    for a in range(n_q):
        tiles = key_tiles(a)
        groups = sorted({t // FLASH_SUB for t in tiles})
        for kb in groups:
            modes = tuple(tiles.get(kb * FLASH_SUB + j, 0) for j in range(FLASH_SUB))
            if modes not in variants:
                variants.append(modes)
            row = (a, kb, (kb * FLASH_SUB * FLASH_TK) // win_keys, int(kb == groups[0]), int(kb == groups[-1]),
                   variants.index(modes))
            for c, v in zip(cols, row):
                c.append(v)
    return tuple(jnp.asarray(np.asarray(c, np.int32)) for c in cols), tuple(variants)


def flash(q, k, v, steps, *, tq, pos_block, window, name):
    tables, variants = steps
    G, _, _, Sq = q.shape
    tq_pos = FLASH_TQ
    rep_cols = (tq // tq_pos) * pos_block
    assert rep_cols % FLASH_QW == 0 and (pos_block % FLASH_QW == 0 or FLASH_QW % pos_block == 0)
    tk = FLASH_SUB * FLASH_TK
    width = max(G, tq // tq_pos) * HEAD_DIM
    assert G == 1 or tq == tq_pos
    grid_spec = pltpu.PrefetchScalarGridSpec(
        num_scalar_prefetch=6,
        grid=(tables[0].shape[0],),
        in_specs=[pl.BlockSpec((G, 1, AUG, tq), lambda p, qi, kb, wi, fi, la, va: (0, wi[p], 0, qi[p])),
                  pl.BlockSpec((G, tk, AUG), lambda p, qi, kb, wi, fi, la, va: (0, kb[p], 0)),
                  pl.BlockSpec((G, V_ROWS, tk), lambda p, qi, kb, wi, fi, la, va: (0, 0, kb[p]))],
        out_specs=pl.BlockSpec((tq_pos, width), lambda p, qi, kb, wi, fi, la, va: (qi[p], 0)),
        scratch_shapes=[pltpu.VMEM((G, 8, tq), f32), pltpu.VMEM((G, V_ROWS, tq), f32)],
    )
    return pl.pallas_call(
        functools.partial(_flash_kernel, n_heads=G, tq=tq, qw=FLASH_QW, tq_pos=tq_pos, rep_cols=rep_cols,
                          pos_block=pos_block, window=window, variants=variants),
        grid_spec=grid_spec,
        out_shape=jax.ShapeDtypeStruct((Sq // (tq // tq_pos), width), bf16),
        compiler_params=_cparams(("arbitrary",)),
        name=name,
    )(*tables, q, k, v)


def _out_kernel(x_ref, gl0_ref, gl1_ref, gl2_ref, sd_ref, bg_ref, om_ref, of_ref, oc_ref, os_ref, ow_ref, zs_ref,
                wm_ref, wn_ref, wf_ref, wo_ref, o_ref):
    zs = zs_ref[...].astype(f32)
    sg = 1.0 / (1.0 + jnp.exp(-sd_ref[...]))
    sg_hi = sg.astype(bf16)
    sg_lo = (sg - sg_hi.astype(f32)).astype(bf16)
    src = _iota2((LANES, NSA_W), 0)
    head = _div(_iota2((LANES, NSA_W), 1), HEAD_DIM)
    o_n = jnp.zeros(oc_ref.shape, f32)
    for branch, ref in enumerate((oc_ref, os_ref, ow_ref)):
        spread = jnp.where(src == 3 * head + branch, 1.0, 0.0).astype(bf16)
        o_n = o_n + (_dot(sg_hi, spread) + _dot(sg_lo, spread)) * ref[...].astype(f32)
    mixers = ((om_ref[...].astype(f32), 0, wm_ref), (o_n, MOBA_W, wn_ref), (of_ref[...].astype(f32), MOBA_W + NSA_W, wf_ref))
    merged = jnp.zeros(x_ref.shape, f32)
    for (o, z_off, w_ref), gl_ref, b in zip(mixers, (gl0_ref, gl1_ref, gl2_ref), range(N_BRANCH)):
        y = _dot((o * zs[:, z_off:z_off + o.shape[1]]).astype(bf16), w_ref[...])
        gate = 1.0 / (1.0 + jnp.exp(-(gl_ref[...].astype(f32) + bg_ref[:, b * D_MODEL:(b + 1) * D_MODEL])))
        merged = merged + y * gate
    o_ref[...] = x_ref[...] + _dot(merged.astype(bf16), wo_ref[...])


def out_proj(x, proj, gate_logits, b_gate, o_m, o_f, o_c, o_s, o_w, zs, w_m, w_n, w_f, w_o, tm=512):
    S = x.shape[0]
    full = lambda a: pl.BlockSpec(a.shape, lambda i: (0,) * a.ndim)
    rows = lambda w: pl.BlockSpec((tm, w), lambda i: (i, 0))
    gl_spec = lambda b: pl.BlockSpec((tm, D_MODEL), lambda i: (i, b))
    return pl.pallas_call(
        _out_kernel,
        grid=(S // tm,),
        in_specs=[rows(D_MODEL), gl_spec(0), gl_spec(1), gl_spec(2),
                  pl.BlockSpec((tm, LANES), lambda i: (i, OFF_D // LANES)),
                  full(b_gate), rows(MOBA_W), rows(FOX_W), rows(NSA_W), rows(NSA_W), rows(NSA_W), rows(SEG_C),
                  full(w_m), full(w_n), full(w_f), full(w_o)],
        out_specs=rows(D_MODEL),
        out_shape=jax.ShapeDtypeStruct((S, D_MODEL), f32),
        compiler_params=_cparams(("arbitrary",)),
        name="out_proj",
    )(x, gate_logits, gate_logits, gate_logits, proj, b_gate, o_m, o_f, o_c, o_s, o_w, zs, w_m, w_n, w_f, w_o)


def _rope_tables(pos, rows):
    half = ROPE_DIM // 2
    inv = ROPE_THETA ** (-jnp.arange(0, ROPE_DIM, 2, dtype=f32) / ROPE_DIM)
    d = np.arange(LANES) % HEAD_DIM
    rotated = d < ROPE_DIM
    inv_row = jnp.where(rotated, inv[d % half], 0.0)
    sign_row = np.where(d < half, -1.0, 1.0).astype(np.float32)
    ang = pos.astype(f32)[:, None] * inv_row[None, :]
    pad = ((0, rows - pos.shape[0]), (0, 0))
    return jnp.pad(jnp.cos(ang), pad), jnp.pad(jnp.sin(ang) * sign_row[None, :], pad)


def _layer(x, norm_g, w_main, b_f, b_gate, moba_qk_g, nsa_q_g, nsa_k_g, fox_qk_g,
           cmp_pe, cmp_w1, cmp_w2, w_up_moba, w_up_nsa, w_up_fox, w_out, tables, rope):
    S = x.shape[0]
    g_row = norm_g.reshape(1, D_MODEL)
    proj = rms_matmul(x, g_row, w_main, 0, MAIN_COLS, f32, tn=2 * D_MODEL)
    gate_logits = rms_matmul(x, g_row, w_main, MAIN_COLS, N_BRANCH * D_MODEL, bf16, tn=D_MODEL)

    gain_row = jnp.concatenate([jnp.tile(moba_qk_g[0], MOBA_HEADS), jnp.tile(moba_qk_g[1], MOBA_HEADS),
                                jnp.tile(nsa_q_g, NSA_HEADS), nsa_k_g[1], nsa_k_g[2],
                                jnp.tile(fox_qk_g[0], FOX_HEADS), jnp.tile(fox_qk_g[1], FOX_HEADS)]).reshape(1, SEG_A)
    bf_row = jnp.zeros((1, LANES), f32).at[0, FF_LANE:FF_LANE + FOX_HEADS].set(b_f)
    (mq, mk, nq, ksl, kw, fq, fk, mv, fv, vsl, vw, zs) = prep(proj, rope[0], rope[1], gain_row, bf_row)

    q_moba = moba_select(mq, moba_kmean(mk))
    o_m = flash(q_moba, mk, mv, tables["causal"], tq=FLASH_TQ, pos_block=FLASH_TQ, window=None, name="flash_moba")

    pe = jnp.concatenate([cmp_pe[0], cmp_pe[1]], axis=1)
    w1 = cmp_w1.astype(bf16).reshape(2, NSA_CMP_LEN, HEAD_DIM, NSA_CMP_HIDDEN)
    zero = jnp.zeros_like(w1[0])
    w1 = jnp.concatenate([jnp.concatenate([w1[0], zero], axis=2),
                          jnp.concatenate([zero, w1[1]], axis=2)], axis=1)
    w2p = jnp.pad(cmp_w2, ((0, 0), (0, 0), (0, LANES - HEAD_DIM))).astype(bf16)
    gain_c = jnp.pad(nsa_k_g[0], (0, LANES - HEAD_DIM)).reshape(1, LANES)
    kc, vc = nsa_compress(proj, pe, w1, w2p, gain_c, rope[2], rope[3])
    o_c, q_slc = nsa_cmp_select(nq, kc, vc, S)
    nsa_cols = NSA_HEADS * FLASH_TQ
    o_s = flash(q_slc, ksl[None], vsl[None], tables["slc"], tq=nsa_cols,
                pos_block=Q_BLOCK, window=None, name="flash_nsa_slc")
    o_w = flash(q_slc, kw[None], vw[None], tables["win"], tq=nsa_cols,
                pos_block=Q_BLOCK, window=NSA_WINDOW, name="flash_nsa_win")

    o_f = flash(fq[:, None], fk, fv, tables["causal"], tq=FLASH_TQ, pos_block=FLASH_TQ, window=None, name="flash_fox")
    return out_proj(x, proj, gate_logits, b_gate.reshape(1, N_BRANCH * D_MODEL), o_m, o_f, o_c, o_s, o_w, zs,
                    w_up_moba.astype(bf16), w_up_nsa.astype(bf16), w_up_fox.astype(bf16), w_out.astype(bf16))


def kernel(x, norm_g, w_in, b_f, b_gate, moba_qk_g, nsa_q_g, nsa_k_g, fox_qk_g, cmp_pe, cmp_w1, cmp_w2,
           w_up_moba, w_up_nsa, w_up_fox, w_out):
    B, S, _ = x.shape
    assert B == 1 and S % 2048 == 0 and S // MOBA_BLOCK <= CODE_BLOCKS
    depth = norm_g.shape[0]
    win_keys = CODE_BLOCKS * NSA_SLC_BLOCK
    n_q = S // FLASH_TQ
    causal = lambda a: {t: (2 if (t + 1) * FLASH_TK > a * FLASH_TQ else 1)
                        for t in range((a + 1) * FLASH_TQ // FLASH_TK)}
    band = lambda a: {t: 2 for t in range(max(0, a * FLASH_TQ - NSA_WINDOW + 1) // FLASH_TK,
                                          (a + 1) * FLASH_TQ // FLASH_TK)}
    tables = {"causal": _step_tables(n_q, causal, S), "slc": _step_tables(n_q, causal, win_keys),
              "win": _step_tables(n_q, band, S)}
    ncp = S // NSA_CMP_STRIDE
    cos_t, sin_t = _rope_tables(jnp.arange(S), S)
    cmp_end = jnp.arange(ncp - 1) * NSA_CMP_STRIDE + (NSA_CMP_LEN - 1)
    cos_c, sin_c = _rope_tables(cmp_end, ncp)
    rope = (cos_t, sin_t, cos_c, sin_c)
    h = x[0]
    for l in range(depth):
        h = _layer(h, norm_g[l], repack_w_in(w_in, l), b_f[l], b_gate[l], moba_qk_g[l], nsa_q_g[l], nsa_k_g[l], fox_qk_g[l],
                   cmp_pe[l], cmp_w1[l], cmp_w2[l], w_up_moba[l], w_up_nsa[l], w_up_fox[l], w_out[l], tables, rope)
    return h[None]
```

```python
import functools

import numpy as np
import jax
import jax.numpy as jnp
from jax import lax
from jax.experimental import pallas as pl
from jax.experimental.pallas import tpu as pltpu

D_MODEL = 1024
HEAD_DIM = 64
ROPE_DIM = HEAD_DIM // 4
ROPE_THETA = 500000.0
RMS_EPS = 1e-6
NEG_INF = -1e30
MASKED_BELOW = 0.5 * NEG_INF
M_FLOOR = -1e20

MOBA_HEADS = 6
MOBA_BLOCK = 256
MOBA_TOPK = 3
NSA_HEADS = 4
NSA_CMP_LEN = 32
NSA_CMP_STRIDE = 16
NSA_CMP_HIDDEN = 4 * HEAD_DIM
NSA_SLC_BLOCK = 64
NSA_SLC_TOPN = 16
NSA_WINDOW = 512
FOX_HEADS = 6
N_BRANCH = 3
MOBA_W = MOBA_HEADS * HEAD_DIM
NSA_W = NSA_HEADS * HEAD_DIM
FOX_W = FOX_HEADS * HEAD_DIM
IN_SPLITS = (MOBA_W,) * 4 + (NSA_W,) + (HEAD_DIM,) * 6 + (3 * NSA_HEADS, NSA_W) + (FOX_W,) * 3 + (FOX_HEADS, FOX_W, N_BRANCH * D_MODEL)
ATTN_SCALE = HEAD_DIM ** -0.5

LANES = 128
AUG = 2 * HEAD_DIM
CODE_BLOCKS = AUG - HEAD_DIM
Q_BLOCK = 128

SEG_A = 2 * MOBA_W + NSA_W + 2 * HEAD_DIM + 2 * FOX_W
SEG_B = 2 * MOBA_W + 4 * HEAD_DIM
SEG_C = MOBA_W + NSA_W + FOX_W
SEG_D = LANES
OFF_B = SEG_A
OFF_C = OFF_B + SEG_B
OFF_D = OFF_C + SEG_C
MAIN_COLS = OFF_D + SEG_D
FF_LANE = 3 * NSA_HEADS
KV_CMP_OFF = OFF_B + 2 * MOBA_W + 2 * HEAD_DIM
ALL_COLS = MAIN_COLS + N_BRANCH * D_MODEL

VMEM_LIMIT = 56 * 1024 * 1024
FLASH_QW = 512
FLASH_TQ = 512
FLASH_TK = 256
FLASH_SUB = 8
LOG2E = 1.4426950408889634
Q_SCALE = ATTN_SCALE * LOG2E
V_ROWS = 80

f32 = jnp.float32
bf16 = jnp.bfloat16


def _cparams(sem):
    return pltpu.CompilerParams(dimension_semantics=sem, vmem_limit_bytes=VMEM_LIMIT)


def _iota2(shape, dim):
    return lax.broadcasted_iota(jnp.int32, shape, dim)


def _div(x, d):
    return jnp.right_shift(x, int(d).bit_length() - 1)


def _mod(x, d):
    return jnp.bitwise_and(x, d - 1)


def _place(n_in, n_out, in_off, out_off, width=HEAD_DIM, val=1.0):
    r = _iota2((n_in, n_out), 0) - in_off
    c = _iota2((n_in, n_out), 1) - out_off
    hit = (r == c) & (r >= 0) & (r < width)
    return jnp.where(hit, val, 0.0).astype(bf16)


def _place_t(n_out, n_in, in_off, out_off, width=HEAD_DIM, val=1.0):
    r = _iota2((n_out, n_in), 0) - out_off
    c = _iota2((n_out, n_in), 1) - in_off
    hit = (r == c) & (r >= 0) & (r < width)
    return jnp.where(hit, val, 0.0).astype(bf16)


def _split3(x):
    hi = x.astype(bf16)
    r = x - hi.astype(f32)
    mid = r.astype(bf16)
    lo = (r - mid.astype(f32)).astype(bf16)
    return hi, mid, lo


def _heads_to_lanes(pieces):
    rows = len(pieces) * HEAD_DIM
    pad = [jnp.zeros((-rows % LANES, pieces[0].shape[1]), f32)] if rows % LANES else []
    return jnp.concatenate(list(pieces) + pad, axis=0).T[:, :rows]


def _dot(a, b):
    return jnp.dot(a, b, preferred_element_type=f32)


def _dot_nt(a, b):
    return lax.dot_general(a, b, (((1,), (1,)), ((), ())), preferred_element_type=f32)


def _w_in_plan():
    offs = np.concatenate([[0], np.cumsum(IN_SPLITS)])
    names = ("mq", "mk", "mv", "mz", "nq", "kc", "vc", "ksl", "vsl", "kw", "vw", "ng", "nz", "fq", "fk", "fv", "ff", "fz", "gl")
    start = {n: int(offs[j]) for j, n in enumerate(names)}
    width = {n: int(IN_SPLITS[j]) for j, n in enumerate(names)}
    order = ("mq", "mk", "nq", "ksl", "kw", "fq", "fk", "mv", "fv", "vsl", "vw", "kc", "vc", "mz", "nz", "fz", "ng", "ff")
    plan = [[] for _ in range(ALL_COLS // LANES)]
    new = 0
    for n in order + ("pad", "gl"):
        if n == "pad":
            new = MAIN_COLS
            continue
        src, left = start[n], width[n]
        while left > 0:
            w = min(left, LANES - new % LANES, LANES - src % LANES)
            plan[new // LANES].append((src // LANES, src % LANES, new % LANES, w))
            src, new, left = src + w, new + w, left - w
    assert new == ALL_COLS
    return plan


def _repack_kernel(w_ref, o_ref, *, plan, tr, n_cols):
    lane = _iota2((tr, LANES), 1)
    loaded = {}

    def source(a):
        if a not in loaded:
            x = w_ref[0, :, a * LANES:(a + 1) * LANES]
            if (a + 1) * LANES > n_cols:
                x = jnp.where(lane < n_cols - a * LANES, x, 0.0)
            loaded[a] = x.astype(bf16)
        return loaded[a]

    for b, pieces in enumerate(plan):
        acc = jnp.zeros((tr, LANES), f32)
        for a, lane_in, lane_out, w in pieces:
            acc = acc + _dot(source(a), _place(LANES, LANES, lane_in, lane_out, width=w))
        o_ref[:, b * LANES:(b + 1) * LANES] = acc.astype(bf16)


def repack_w_in(w_in, layer, tr=128):
    _, D, n_cols = w_in.shape
    padded = pl.cdiv(n_cols, LANES) * LANES
    return pl.pallas_call(
        functools.partial(_repack_kernel, plan=_w_in_plan(), tr=tr, n_cols=n_cols),
        grid=(D // tr,),
        in_specs=[pl.BlockSpec((1, tr, padded), lambda i: (layer, i, 0))],
        out_specs=pl.BlockSpec((tr, ALL_COLS), lambda i: (i, 0)),
        out_shape=jax.ShapeDtypeStruct((D, ALL_COLS), bf16),
        compiler_params=_cparams(("arbitrary",)),
        name="repack_w_in",
    )(w_in)


def _rms_matmul_kernel(x_ref, g_ref, w_ref, o_ref, h_sc):
    @pl.when(pl.program_id(1) == 0)
    def _():
        x = x_ref[...]
        ms = jnp.mean(x * x, axis=-1, keepdims=True)
        h_sc[...] = (x * lax.rsqrt(ms + RMS_EPS) * g_ref[...]).astype(bf16)

    o_ref[...] = _dot(h_sc[...], w_ref[...]).astype(o_ref.dtype)


def rms_matmul(x, g, w, col0, n_cols, dtype, tn, tm=1024):
    S, D = x.shape
    assert col0 % tn == 0 and n_cols % tn == 0
    return pl.pallas_call(
        _rms_matmul_kernel,
        grid=(S // tm, n_cols // tn),
        in_specs=[pl.BlockSpec((tm, D), lambda i, j: (i, 0)),
                  pl.BlockSpec((1, D), lambda i, j: (0, 0)),
                  pl.BlockSpec((D, tn), lambda i, j: (0, col0 // tn + j))],
        out_specs=pl.BlockSpec((tm, tn), lambda i, j: (i, j)),
        out_shape=jax.ShapeDtypeStruct((S, n_cols), dtype),
        scratch_shapes=[pltpu.VMEM((tm, D), bf16)],
        compiler_params=_cparams(("arbitrary", "arbitrary")),
        name="rms_matmul",
    )(x, g, w)


def _prep_kernel(p_ref, cos_ref, sin_ref, gain_ref, bf_ref,
                 mq_ref, mk_ref, nq_ref, ksl_ref, kw_ref, fq_ref, fk_ref,
                 mv_ref, fv_ref, vsl_ref, vw_ref, zs_ref, carry_sc, *, ts):
    i = pl.program_id(0)

    @pl.when(i == 0)
    def _():
        carry_sc[...] = jnp.zeros_like(carry_sc)

    lane = _iota2((ts, LANES), 1)
    pos = _iota2((ts, LANES), 0) + i * ts
    blockdiag = jnp.where(_div(_iota2((LANES, LANES), 0), HEAD_DIM) == _div(_iota2((LANES, LANES), 1), HEAD_DIM),
                          1.0, 0.0).astype(bf16)
    first_half = _mod(lane, HEAD_DIM) < (ROPE_DIM // 2)
    low_lanes = lane < HEAD_DIM
    cos = cos_ref[...]
    sin = sin_ref[...]

    def normed(c, rope, scale=None):
        x = p_ref[:, c * LANES:(c + 1) * LANES]
        x2 = x * x
        hi = x2.astype(bf16)
        lo = (x2 - hi.astype(f32)).astype(bf16)
        ss = _dot(hi, blockdiag) + _dot(lo, blockdiag)
        y = x * lax.rsqrt(ss * (1.0 / HEAD_DIM) + RMS_EPS) * gain_ref[:, c * LANES:(c + 1) * LANES]
        if rope:
            up = pltpu.roll(y, LANES - ROPE_DIM // 2, 1)
            dn = pltpu.roll(y, ROPE_DIM // 2, 1)
            y = y * cos + jnp.where(first_half, up, dn) * sin
        return y if scale is None else y * scale

    def head_rows(y_t, half):
        return y_t[half * HEAD_DIM:(half + 1) * HEAD_DIM, :]

    def head_lanes(y, half):
        return y if half == 0 else pltpu.roll(y, HEAD_DIM, 1)

    d = p_ref[:, OFF_D:OFF_D + LANES] + bf_ref[...]
    logf = jnp.minimum(d, 0.0) - jnp.log(1.0 + jnp.exp(-jnp.abs(d)))
    tri = jnp.where(_iota2((ts, ts), 1) <= _iota2((ts, ts), 0), 1.0, 0.0).astype(bf16)
    lh, lm, ll = _split3(logf)
    c = carry_sc[0:1, :] + (_dot(tri, lh) + _dot(tri, lm) + _dot(tri, ll))
    carry_sc[...] = jnp.broadcast_to(c[ts - 1:ts, :], carry_sc.shape)
    c2 = c * LOG2E
    pieces = [p.astype(f32) for p in _split3(c2)]
    pieces_t = [p.astype(f32) for p in _split3(c2.T)]
    row64 = _iota2((HEAD_DIM, ts), 0)

    def decay_rows(h):
        r = FF_LANE + h
        hi, mid, lo = (jnp.broadcast_to(p[r:r + 1, :], (HEAD_DIM, ts)) for p in pieces_t)
        return jnp.where(row64 == 0, hi, jnp.where(row64 == 1, mid, jnp.where(row64 == 2, lo,
                         jnp.where(row64 < 6, 1.0, 0.0))))

    def decay_lanes(h):
        r = FF_LANE + h
        hi, mid, lo = (jnp.broadcast_to(p[:, r:r + 1], (ts, LANES)) for p in pieces)
        return jnp.where(lane == HEAD_DIM + 3, -hi, jnp.where(lane == HEAD_DIM + 4, -mid,
                         jnp.where(lane == HEAD_DIM + 5, -lo, jnp.where(lane < HEAD_DIM + 3, 1.0, 0.0))))

    moba_code = jnp.where((lane - HEAD_DIM) == _div(pos, MOBA_BLOCK), 1.0, 0.0)
    for c_i in range(3):
        yq_t = normed(c_i, True, Q_SCALE).T
        yk = normed(3 + c_i, True)
        for half in range(2):
            h = 2 * c_i + half
            mq_ref[h] = head_rows(yq_t, half).astype(bf16)
            mk_ref[h] = jnp.where(low_lanes, head_lanes(yk, half), moba_code).astype(bf16)
    for c_i in range(2):
        y_t = normed(6 + c_i, True, Q_SCALE).T
        for half in range(2):
            h = 2 * c_i + half
            yh = head_rows(y_t, half).astype(bf16)
            for qb in range(ts // Q_BLOCK):
                nq_ref[:, (qb * NSA_HEADS + h) * Q_BLOCK:(qb * NSA_HEADS + h + 1) * Q_BLOCK] = (
                    yh[:, qb * Q_BLOCK:(qb + 1) * Q_BLOCK])
    y = normed(8, True)
    slc_code = jnp.where((lane - HEAD_DIM) == _mod(_div(pos, NSA_SLC_BLOCK), CODE_BLOCKS), 1.0, 0.0)
    ksl_ref[...] = jnp.where(low_lanes, y, slc_code).astype(bf16)
    kw_ref[...] = jnp.where(low_lanes, head_lanes(y, 1), 0.0).astype(bf16)
    for c_i in range(3):
        yq_t = normed(9 + c_i, False, Q_SCALE).T
        yk = normed(12 + c_i, False)
        for half in range(2):
            h = 2 * c_i + half
            fq_ref[h] = jnp.concatenate([head_rows(yq_t, half), decay_rows(h)], axis=0).astype(bf16)
            fk_ref[h] = jnp.where(low_lanes, head_lanes(yk, half), decay_lanes(h)).astype(bf16)
    ones_rows = jnp.where(_iota2((V_ROWS - HEAD_DIM, ts), 0) == 0, 1.0, 0.0)

    def value_rows(x_t, half):
        return jnp.concatenate([head_rows(x_t, half), ones_rows], axis=0).astype(bf16)

    for c_i in range(3):
        xm_t = p_ref[:, OFF_B + c_i * LANES:OFF_B + (c_i + 1) * LANES].T
        xf_t = p_ref[:, OFF_B + MOBA_W + c_i * LANES:OFF_B + MOBA_W + (c_i + 1) * LANES].T
        for half in range(2):
            mv_ref[2 * c_i + half] = value_rows(xm_t, half)
            fv_ref[2 * c_i + half] = value_rows(xf_t, half)
    xs_t = p_ref[:, OFF_B + 2 * MOBA_W:OFF_B + 2 * MOBA_W + LANES].T
    vsl_ref[...] = value_rows(xs_t, 0)
    vw_ref[...] = value_rows(xs_t, 1)
    z = p_ref[:, OFF_C:OFF_C + SEG_C]
    zs_ref[...] = (z * (1.0 / (1.0 + jnp.exp(-z)))).astype(bf16)


def prep(proj, cos_t, sin_t, gain_row, bf_row, ts=512):
    S = proj.shape[0]
    head128 = lambda n: jax.ShapeDtypeStruct((n, S, AUG), bf16)
    spec_h = lambda n, w: pl.BlockSpec((n, ts, w), lambda i: (0, i, 0))
    spec_r = lambda w: pl.BlockSpec((ts, w), lambda i: (i, 0))
    head_t = lambda n, r: jax.ShapeDtypeStruct((n, r, S), bf16)
    spec_ht = lambda n, r: pl.BlockSpec((n, r, ts), lambda i: (0, 0, i))
    spec_t = pl.BlockSpec((V_ROWS, ts), lambda i: (0, i))
    out_shape = (head_t(MOBA_HEADS, HEAD_DIM), head128(MOBA_HEADS),
                 jax.ShapeDtypeStruct((HEAD_DIM, NSA_HEADS * S), bf16),
                 jax.ShapeDtypeStruct((S, AUG), bf16), jax.ShapeDtypeStruct((S, AUG), bf16),
                 head_t(FOX_HEADS, AUG), head128(FOX_HEADS),
                 head_t(MOBA_HEADS, V_ROWS), head_t(FOX_HEADS, V_ROWS),
                 jax.ShapeDtypeStruct((V_ROWS, S), bf16), jax.ShapeDtypeStruct((V_ROWS, S), bf16),
                 jax.ShapeDtypeStruct((S, SEG_C), bf16))
    out_specs = (spec_ht(MOBA_HEADS, HEAD_DIM), spec_h(MOBA_HEADS, AUG),
                 pl.BlockSpec((HEAD_DIM, NSA_HEADS * ts), lambda i: (0, i)),
                 spec_r(AUG), spec_r(AUG),
                 spec_ht(FOX_HEADS, AUG), spec_h(FOX_HEADS, AUG),
                 spec_ht(MOBA_HEADS, V_ROWS), spec_ht(FOX_HEADS, V_ROWS),
                 spec_t, spec_t,
                 spec_r(SEG_C))
    return pl.pallas_call(
        functools.partial(_prep_kernel, ts=ts),
        grid=(S // ts,),
        in_specs=[pl.BlockSpec((ts, MAIN_COLS), lambda i: (i, 0)),
                  spec_r(LANES), spec_r(LANES),
                  pl.BlockSpec((1, SEG_A), lambda i: (0, 0)),
                  pl.BlockSpec((1, LANES), lambda i: (0, 0))],
        out_specs=out_specs,
        out_shape=out_shape,
        scratch_shapes=[pltpu.VMEM((8, LANES), f32)],
        compiler_params=_cparams(("arbitrary",)),
        name="prep",
    )(proj, cos_t, sin_t, gain_row, bf_row)


def _kmean_kernel(k_ref, o_ref, *, rows):
    n = rows // MOBA_BLOCK
    avg = jnp.where(_div(_iota2((n, rows), 1), MOBA_BLOCK) == _iota2((n, rows), 0),
                    1.0 / MOBA_BLOCK, 0.0).astype(bf16)
    o_ref[0] = _dot(avg, k_ref[0])[:, :HEAD_DIM]


def moba_kmean(mk_aug):
    H, S, _ = mk_aug.shape
    rows = 8 * MOBA_BLOCK
    return pl.pallas_call(
        functools.partial(_kmean_kernel, rows=rows),
        grid=(H, S // rows),
        in_specs=[pl.BlockSpec((1, rows, AUG), lambda h, i: (h, i, 0))],
        out_specs=pl.BlockSpec((1, 8, HEAD_DIM), lambda h, i: (h, i, 0)),
        out_shape=jax.ShapeDtypeStruct((H, CODE_BLOCKS, HEAD_DIM), f32),
        compiler_params=_cparams(("arbitrary", "arbitrary")),
        name="moba_kmean",
    )(mk_aug)


def _top_select(scores, idx, n_pick, floor):
    big = jnp.int32(2 ** 30)
    scores = list(scores)
    for _ in range(n_pick):
        for j, score in enumerate(scores):
            m = jnp.max(score, axis=0, keepdims=True)
            first = jnp.min(jnp.where(score == m, idx, big), axis=0, keepdims=True)
            first = jnp.where(m > floor, first, big)
            scores[j] = jnp.where(idx == first, -jnp.inf, score)
    return [s == -jnp.inf for s in scores]


def _moba_select_kernel(q_ref, km_ref, o_ref, *, tq, n_heads):
    i = pl.program_id(0)
    blk = _iota2((CODE_BLOCKS, tq), 0)
    cur = _div(_iota2((CODE_BLOCKS, tq), 1) + i * tq, MOBA_BLOCK)
    scores = []
    for h in range(n_heads):
        q = q_ref[h]
        km = km_ref[h]
        km_hi = km.astype(bf16)
        km_lo = (km - km_hi.astype(f32)).astype(bf16)
        gate = _dot(km_hi, q) + _dot(km_lo, q)
        scores.append(jnp.where(blk < cur, gate, NEG_INF))
    for h, sel in enumerate(_top_select(scores, blk, MOBA_TOPK, MASKED_BELOW)):
        o_ref[h, 0, 0:HEAD_DIM, :] = q_ref[h]
        o_ref[h, 0, HEAD_DIM:AUG, :] = jnp.where(sel | (blk == cur), 0.0, NEG_INF).astype(bf16)


def moba_select(mq, kmean, tq=512):
    H, _, S = mq.shape
    return pl.pallas_call(
        functools.partial(_moba_select_kernel, tq=tq, n_heads=H),
        grid=(S // tq,),
        in_specs=[pl.BlockSpec((H, HEAD_DIM, tq), lambda i: (0, 0, i)),
                  pl.BlockSpec((H, CODE_BLOCKS, HEAD_DIM), lambda i: (0, 0, 0))],
        out_specs=pl.BlockSpec((H, 1, AUG, tq), lambda i: (0, 0, 0, i)),
        out_shape=jax.ShapeDtypeStruct((H, 1, AUG, S), bf16),
        compiler_params=_cparams(("arbitrary",)),
        name="moba_select",
    )(mq, kmean)


def _cmp_kernel(x_ref, pe_ref, w1_ref, w2_ref, gain_ref, cos_ref, sin_ref, kc_ref, vc_ref, *, ncp):
    top = jnp.zeros((ncp, 2 * NSA_CMP_HIDDEN), f32)
    nxt = jnp.zeros((ncp, 2 * NSA_CMP_HIDDEN), f32)
    for j in range(NSA_CMP_STRIDE):
        xj = x_ref[pl.ds(j, ncp, stride=NSA_CMP_STRIDE), :]
        top = top + _dot((xj + pe_ref[j:j + 1, :]).astype(bf16), w1_ref[j])
        nxt = nxt + _dot((xj + pe_ref[NSA_CMP_STRIDE + j:NSA_CMP_STRIDE + j + 1, :]).astype(bf16),
                         w1_ref[NSA_CMP_STRIDE + j])
    hid = top + pltpu.roll(nxt, ncp - 1, 0)
    act = (hid * (1.0 / (1.0 + jnp.exp(-hid)))).astype(bf16)
    k = _dot(act[:, :NSA_CMP_HIDDEN], w2_ref[0])
    v = _dot(act[:, NSA_CMP_HIDDEN:], w2_ref[1])
    ms = jnp.sum(k * k, axis=-1, keepdims=True) * (1.0 / HEAD_DIM)
    y = k * lax.rsqrt(ms + RMS_EPS) * gain_ref[...]
    lane = _iota2((ncp, LANES), 1)
    up = pltpu.roll(y, LANES - ROPE_DIM // 2, 1)
    dn = pltpu.roll(y, ROPE_DIM // 2, 1)
    y = y * cos_ref[...] + jnp.where(lane < ROPE_DIM // 2, up, dn) * sin_ref[...]
    kc_ref[...] = y[:, :HEAD_DIM].astype(bf16)
    vc_ref[...] = _dot_nt(_place_t(V_ROWS, LANES, 0, 0), v.astype(bf16)).astype(bf16)


def nsa_compress(proj, pe, w1, w2, gain, cos_c, sin_c):
    S = proj.shape[0]
    ncp = S // NSA_CMP_STRIDE
    full = lambda a: pl.BlockSpec(a.shape, lambda i: (0,) * a.ndim)
    return pl.pallas_call(
        functools.partial(_cmp_kernel, ncp=ncp),
        grid=(1,),
        in_specs=[pl.BlockSpec((S, LANES), lambda i: (0, KV_CMP_OFF // LANES)),
                  full(pe), full(w1), full(w2), full(gain), full(cos_c), full(sin_c)],
        out_specs=(pl.BlockSpec((ncp, HEAD_DIM), lambda i: (0, 0)), pl.BlockSpec((V_ROWS, ncp), lambda i: (0, 0))),
        out_shape=(jax.ShapeDtypeStruct((ncp, HEAD_DIM), bf16), jax.ShapeDtypeStruct((V_ROWS, ncp), bf16)),
        compiler_params=_cparams(("arbitrary",)),
        name="nsa_compress",
    )(proj, pe, w1, w2, gain, cos_c, sin_c)


def _nsa_cmp_select_kernel(q_ref, kc_ref, vc_ref, oc_ref, qa_ref, imp_sc, *, ncp, nsp, n_win, n_qb):
    i = pl.program_id(0)
    cols = NSA_HEADS * Q_BLOCK
    blk = _iota2((nsp, Q_BLOCK), 0)

    def cur(b):
        return _div(_iota2((nsp, Q_BLOCK), 1) + (i * n_qb + b) * Q_BLOCK, NSA_SLC_BLOCK)

    def forced(b):
        return (blk == 0) | (blk == cur(b)) | (blk == cur(b) - 1)

    n_split = 4 if ncp % (4 * LANES) == 0 else 1
    lo = 0
    for part in range(n_split):
        nk = ncp * (part + 1) // n_split
        hi = nk * NSA_CMP_STRIDE // (n_qb * Q_BLOCK)
        pl.when((i >= lo) & (i < hi))(functools.partial(
            _cmp_attention, q_ref, kc_ref, vc_ref, oc_ref, imp_sc, i, cur, forced, blk, nk=nk, nsp=nsp, n_qb=n_qb))
        lo = hi

    n_free = NSA_SLC_TOPN - 3
    for b, sel in enumerate(_top_select([imp_sc[b] for b in range(n_qb)], blk, n_free, MASKED_BELOW)):
        bias = jnp.where(sel | forced(b), 0.0, NEG_INF).astype(bf16)
        for w in range(n_win):
            bw = bias[w * CODE_BLOCKS:(w + 1) * CODE_BLOCKS, :]
            qa_ref[0, w, 0:HEAD_DIM, b * cols:(b + 1) * cols] = q_ref[:, b * cols:(b + 1) * cols]
            qa_ref[0, w, HEAD_DIM:AUG, b * cols:(b + 1) * cols] = jnp.concatenate([bw] * NSA_HEADS, axis=1)


def _cmp_attention(q_ref, kc_ref, vc_ref, oc_ref, imp_sc, i, cur, forced, blk, *, nk, nsp, n_qb):
    cols = NSA_HEADS * Q_BLOCK
    kc = kc_ref[0:nk, :]
    vc = vc_ref[:, 0:nk]
    c0 = _iota2((nsp, nk), 1) * NSA_CMP_STRIDE
    b0 = _iota2((nsp, nk), 0) * NSA_SLC_BLOCK
    overlap = jnp.where((c0 <= b0 + (NSA_SLC_BLOCK - 1)) & (c0 + (NSA_CMP_LEN - 1) >= b0), 1.0, 0.0).astype(bf16)
    kend = _iota2((nk, cols), 0) * NSA_CMP_STRIDE + (NSA_CMP_LEN - 1)
    col_pos = _mod(_iota2((nk, cols), 1), Q_BLOCK)

    def scores(b):
        return _dot(kc, q_ref[:, b * cols:(b + 1) * cols])

    def probs(b, s):
        ok = kend <= (i * n_qb + b) * Q_BLOCK + col_pos
        sm = jnp.where(ok, s, NEG_INF)
        m = jnp.max(sm, axis=0, keepdims=True)
        e = jnp.exp2(sm - m)
        return e * jnp.where(m > MASKED_BELOW, 1.0 / jnp.sum(e, axis=0, keepdims=True), 0.0)

    def importance(b, p):
        o = _dot(vc, p.astype(bf16))
        oc_ref[b * Q_BLOCK:(b + 1) * Q_BLOCK, :] = _heads_to_lanes(
            [o[0:HEAD_DIM, h * Q_BLOCK:(h + 1) * Q_BLOCK] for h in range(NSA_HEADS)]).astype(bf16)
        psum = (p[:, 0:Q_BLOCK] + p[:, Q_BLOCK:2 * Q_BLOCK]
                + p[:, 2 * Q_BLOCK:3 * Q_BLOCK] + p[:, 3 * Q_BLOCK:4 * Q_BLOCK])
        ph, pm, plo = _split3(psum)
        imp = _dot(overlap, ph) + _dot(overlap, pm) + _dot(overlap, plo)
        return jnp.where((blk <= cur(b)) & jnp.logical_not(forced(b)), imp, NEG_INF)

    s_next = scores(0)
    for b in range(n_qb):
        s = s_next
        if b + 1 < n_qb:
            s_next = scores(b + 1)
        imp_sc[b] = importance(b, probs(b, s))


def nsa_cmp_select(nq, kc, vc, S, n_qb=4):
    ncp = kc.shape[0]
    ns = S // NSA_SLC_BLOCK
    nsp = max(LANES, ns)
    n_win = max(1, ns // CODE_BLOCKS)
    cols = n_qb * NSA_HEADS * Q_BLOCK
    return pl.pallas_call(
        functools.partial(_nsa_cmp_select_kernel, ncp=ncp, nsp=nsp, n_win=n_win, n_qb=n_qb),
        grid=(S // (n_qb * Q_BLOCK),),
        in_specs=[pl.BlockSpec((HEAD_DIM, cols), lambda i: (0, i)),
                  pl.BlockSpec((ncp, HEAD_DIM), lambda i: (0, 0)),
                  pl.BlockSpec((V_ROWS, ncp), lambda i: (0, 0))],
        out_specs=(pl.BlockSpec((n_qb * Q_BLOCK, NSA_W), lambda i: (i, 0)),
                   pl.BlockSpec((1, n_win, AUG, cols), lambda i: (0, 0, 0, i))),
        out_shape=(jax.ShapeDtypeStruct((S, NSA_W), bf16),
                   jax.ShapeDtypeStruct((1, n_win, AUG, NSA_HEADS * S), bf16)),
        scratch_shapes=[pltpu.VMEM((n_qb, nsp, Q_BLOCK), f32)],
        compiler_params=_cparams(("arbitrary",)),
        name="nsa_cmp_select",
    )(nq, kc, vc)


def _flash_kernel(qi_t, kb_t, win_t, first_t, last_t, var_t, q_ref, k_ref, v_ref, o_ref, m_sc, acc_sc,
                  *, n_heads, tq, qw, tq_pos, rep_cols, pos_block, window, variants):
    p_id = pl.program_id(0)
    qi = qi_t[p_id]
    kb = kb_t[p_id]

    @pl.when(first_t[p_id] == 1)
    def _():
        m_sc[...] = jnp.full(m_sc.shape, M_FLOOR, f32)
        acc_sc[...] = jnp.zeros_like(acc_sc)

    def tile(modes):
        chains = [(g, slice(c * qw, (c + 1) * qw), slice(kt * FLASH_TK, (kt + 1) * FLASH_TK))
                  for kt in range(FLASH_SUB) if modes[kt] for g in range(n_heads) for c in range(tq // qw)]
        if 2 in modes:
            rel = _mod(_iota2((FLASH_TK, qw), 1), pos_block) - _iota2((FLASH_TK, qw), 0)

        def scores(t):
            g, cols, keys = chains[t]
            s = _dot(k_ref[g, keys, :], q_ref[g, 0, :, cols])
            if modes[keys.start // FLASH_TK] == 1:
                return s
            base = (qi * tq_pos + (cols.start // rep_cols) * pos_block + cols.start % pos_block
                    - kb * (FLASH_SUB * FLASH_TK) - keys.start)
            delta = rel + base
            ok = (delta >= 0) if window is None else (delta.astype(jnp.uint32) < jnp.uint32(window))
            return jnp.where(ok, s, NEG_INF)

        def softmax(t, s):
            g, cols, _ = chains[t]
            m_prev = m_sc[g, 0:1, cols]
            m_new = jnp.maximum(m_prev, jnp.max(s, axis=0, keepdims=True))
            m_sc[g, :, cols] = jnp.broadcast_to(m_new, (m_sc.shape[1], qw))
            return jnp.exp2(s - m_new).astype(bf16), jnp.exp2(m_prev - m_new)

        def accumulate(t, p, alpha):
            g, cols, keys = chains[t]
            acc_sc[g, :, cols] = alpha * acc_sc[g, :, cols] + _dot(v_ref[g, :, keys], p)

        n = len(chains)
        s_next, staged = scores(0), None
        for t in range(n + 1):
            s = s_next
            if t + 1 < n:
                s_next = scores(t + 1)
            if staged is not None:
                accumulate(t - 1, *staged)
            staged = softmax(t, s) if t < n else None

    for vid, modes in enumerate(variants):
        pl.when(var_t[p_id] == vid)(functools.partial(tile, modes))

    @pl.when(last_t[p_id] == 1)
    def _():
        def normalised(g, cols):
            acc = acc_sc[g, :, cols]
            l = acc[HEAD_DIM:HEAD_DIM + 1, :]
            return acc[0:HEAD_DIM, :] * jnp.where(l > 0.0, 1.0 / l, 0.0)

        if n_heads > 1:
            o_ref[...] = _heads_to_lanes([normalised(g, slice(0, tq)) for g in range(n_heads)]).astype(o_ref.dtype)
        else:
            per_block = tq // tq_pos * pos_block
            for qb in range(tq // per_block):
                pieces = [normalised(0, slice(qb * per_block + h * pos_block, qb * per_block + (h + 1) * pos_block))
                          for h in range(per_block // pos_block)]
                o_ref[qb * pos_block:(qb + 1) * pos_block, :] = _heads_to_lanes(pieces).astype(o_ref.dtype)


def _step_tables(n_q, key_tiles, win_keys):
    variants, cols = [], ([], [], [], [], [], [])
    for a in range(n_q):
        tiles = key_tiles(a)
        groups = sorted({t // FLASH_SUB for t in tiles})
        for kb in groups:
            modes = tuple(tiles.get(kb * FLASH_SUB + j, 0) for j in range(FLASH_SUB))
            if modes not in variants:
                variants.append(modes)
            row = (a, kb, (kb * FLASH_SUB * FLASH_TK) // win_keys, int(kb == groups[0]), int(kb == groups[-1]),
                   variants.index(modes))
            for c, v in zip(cols, row):
                c.append(v)
    return tuple(jnp.asarray(np.asarray(c, np.int32)) for c in cols), tuple(variants)


def flash(q, k, v, steps, *, tq, pos_block, window, name):
    tables, variants = steps
    G, _, _, Sq = q.shape
    tq_pos = FLASH_TQ
    rep_cols = (tq // tq_pos) * pos_block
    assert rep_cols % FLASH_QW == 0 and (pos_block % FLASH_QW == 0 or FLASH_QW % pos_block == 0)
    tk = FLASH_SUB * FLASH_TK
    width = max(G, tq // tq_pos) * HEAD_DIM
    assert G == 1 or tq == tq_pos
    grid_spec = pltpu.PrefetchScalarGridSpec(
        num_scalar_prefetch=6,
        grid=(tables[0].shape[0],),
        in_specs=[pl.BlockSpec((G, 1, AUG, tq), lambda p, qi, kb, wi, fi, la, va: (0, wi[p], 0, qi[p])),
                  pl.BlockSpec((G, tk, AUG), lambda p, qi, kb, wi, fi, la, va: (0, kb[p], 0)),
                  pl.BlockSpec((G, V_ROWS, tk), lambda p, qi, kb, wi, fi, la, va: (0, 0, kb[p]))],
        out_specs=pl.BlockSpec((tq_pos, width), lambda p, qi, kb, wi, fi, la, va: (qi[p], 0)),
        scratch_shapes=[pltpu.VMEM((G, 8, tq), f32), pltpu.VMEM((G, V_ROWS, tq), f32)],
    )
    return pl.pallas_call(
        functools.partial(_flash_kernel, n_heads=G, tq=tq, qw=FLASH_QW, tq_pos=tq_pos, rep_cols=rep_cols,
                          pos_block=pos_block, window=window, variants=variants),
        grid_spec=grid_spec,
        out_shape=jax.ShapeDtypeStruct((Sq // (tq // tq_pos), width), bf16),
        compiler_params=_cparams(("arbitrary",)),
        name=name,
    )(*tables, q, k, v)


def _out_kernel(x_ref, gl0_ref, gl1_ref, gl2_ref, sd_ref, bg_ref, om_ref, of_ref, oc_ref, os_ref, ow_ref, zs_ref,
                wm_ref, wn_ref, wf_ref, wo_ref, o_ref):
    zs = zs_ref[...].astype(f32)
    sg = 1.0 / (1.0 + jnp.exp(-sd_ref[...]))
    sg_hi = sg.astype(bf16)
    sg_lo = (sg - sg_hi.astype(f32)).astype(bf16)
    src = _iota2((LANES, NSA_W), 0)
    head = _div(_iota2((LANES, NSA_W), 1), HEAD_DIM)
    o_n = jnp.zeros(oc_ref.shape, f32)
    for branch, ref in enumerate((oc_ref, os_ref, ow_ref)):
        spread = jnp.where(src == 3 * head + branch, 1.0, 0.0).astype(bf16)
        o_n = o_n + (_dot(sg_hi, spread) + _dot(sg_lo, spread)) * ref[...].astype(f32)
    mixers = ((om_ref[...].astype(f32), 0, wm_ref), (o_n, MOBA_W, wn_ref), (of_ref[...].astype(f32), MOBA_W + NSA_W, wf_ref))
    merged = jnp.zeros(x_ref.shape, f32)
    for (o, z_off, w_ref), gl_ref, b in zip(mixers, (gl0_ref, gl1_ref, gl2_ref), range(N_BRANCH)):
        y = _dot((o * zs[:, z_off:z_off + o.shape[1]]).astype(bf16), w_ref[...])
        gate = 1.0 / (1.0 + jnp.exp(-(gl_ref[...].astype(f32) + bg_ref[:, b * D_MODEL:(b + 1) * D_MODEL])))
        merged = merged + y * gate
    o_ref[...] = x_ref[...] + _dot(merged.astype(bf16), wo_ref[...])


def out_proj(x, proj, gate_logits, b_gate, o_m, o_f, o_c, o_s, o_w, zs, w_m, w_n, w_f, w_o, tm=512):
    S = x.shape[0]
    full = lambda a: pl.BlockSpec(a.shape, lambda i: (0,) * a.ndim)
    rows = lambda w: pl.BlockSpec((tm, w), lambda i: (i, 0))
    gl_spec = lambda b: pl.BlockSpec((tm, D_MODEL), lambda i: (i, b))
    return pl.pallas_call(
        _out_kernel,
        grid=(S // tm,),
        in_specs=[rows(D_MODEL), gl_spec(0), gl_spec(1), gl_spec(2),
                  pl.BlockSpec((tm, LANES), lambda i: (i, OFF_D // LANES)),
                  full(b_gate), rows(MOBA_W), rows(FOX_W), rows(NSA_W), rows(NSA_W), rows(NSA_W), rows(SEG_C),
                  full(w_m), full(w_n), full(w_f), full(w_o)],
        out_specs=rows(D_MODEL),
        out_shape=jax.ShapeDtypeStruct((S, D_MODEL), f32),
        compiler_params=_cparams(("arbitrary",)),
        name="out_proj",
    )(x, gate_logits, gate_logits, gate_logits, proj, b_gate, o_m, o_f, o_c, o_s, o_w, zs, w_m, w_n, w_f, w_o)


def _rope_tables(pos, rows):
    half = ROPE_DIM // 2
    inv = ROPE_THETA ** (-jnp.arange(0, ROPE_DIM, 2, dtype=f32) / ROPE_DIM)
    d = np.arange(LANES) % HEAD_DIM
    rotated = d < ROPE_DIM
    inv_row = jnp.where(rotated, inv[d % half], 0.0)
    sign_row = np.where(d < half, -1.0, 1.0).astype(np.float32)
    ang = pos.astype(f32)[:, None] * inv_row[None, :]
    pad = ((0, rows - pos.shape[0]), (0, 0))
    return jnp.pad(jnp.cos(ang), pad), jnp.pad(jnp.sin(ang) * sign_row[None, :], pad)


def _layer(x, norm_g, w_main, b_f, b_gate, moba_qk_g, nsa_q_g, nsa_k_g, fox_qk_g,
           cmp_pe, cmp_w1, cmp_w2, w_up_moba, w_up_nsa, w_up_fox, w_out, tables, rope):
    S = x.shape[0]
    g_row = norm_g.reshape(1, D_MODEL)
    proj = rms_matmul(x, g_row, w_main, 0, MAIN_COLS, f32, tn=2 * D_MODEL)
    gate_logits = rms_matmul(x, g_row, w_main, MAIN_COLS, N_BRANCH * D_MODEL, bf16, tn=D_MODEL)

    gain_row = jnp.concatenate([jnp.tile(moba_qk_g[0], MOBA_HEADS), jnp.tile(moba_qk_g[1], MOBA_HEADS),
                                jnp.tile(nsa_q_g, NSA_HEADS), nsa_k_g[1], nsa_k_g[2],
                                jnp.tile(fox_qk_g[0], FOX_HEADS), jnp.tile(fox_qk_g[1], FOX_HEADS)]).reshape(1, SEG_A)
    bf_row = jnp.zeros((1, LANES), f32).at[0, FF_LANE:FF_LANE + FOX_HEADS].set(b_f)
    (mq, mk, nq, ksl, kw, fq, fk, mv, fv, vsl, vw, zs) = prep(proj, rope[0], rope[1], gain_row, bf_row)

    q_moba = moba_select(mq, moba_kmean(mk))
    o_m = flash(q_moba, mk, mv, tables["causal"], tq=FLASH_TQ, pos_block=FLASH_TQ, window=None, name="flash_moba")

    pe = jnp.concatenate([cmp_pe[0], cmp_pe[1]], axis=1)
    w1 = cmp_w1.astype(bf16).reshape(2, NSA_CMP_LEN, HEAD_DIM, NSA_CMP_HIDDEN)
    zero = jnp.zeros_like(w1[0])
    w1 = jnp.concatenate([jnp.concatenate([w1[0], zero], axis=2),
                          jnp.concatenate([zero, w1[1]], axis=2)], axis=1)
    w2p = jnp.pad(cmp_w2, ((0, 0), (0, 0), (0, LANES - HEAD_DIM))).astype(bf16)
    gain_c = jnp.pad(nsa_k_g[0], (0, LANES - HEAD_DIM)).reshape(1, LANES)
    kc, vc = nsa_compress(proj, pe, w1, w2p, gain_c, rope[2], rope[3])
    o_c, q_slc = nsa_cmp_select(nq, kc, vc, S)
    nsa_cols = NSA_HEADS * FLASH_TQ
    o_s = flash(q_slc, ksl[None], vsl[None], tables["slc"], tq=nsa_cols,
                pos_block=Q_BLOCK, window=None, name="flash_nsa_slc")
    o_w = flash(q_slc, kw[None], vw[None], tables["win"], tq=nsa_cols,
                pos_block=Q_BLOCK, window=NSA_WINDOW, name="flash_nsa_win")

    o_f = flash(fq[:, None], fk, fv, tables["causal"], tq=FLASH_TQ, pos_block=FLASH_TQ, window=None, name="flash_fox")
    return out_proj(x, proj, gate_logits, b_gate.reshape(1, N_BRANCH * D_MODEL), o_m, o_f, o_c, o_s, o_w, zs,
                    w_up_moba.astype(bf16), w_up_nsa.astype(bf16), w_up_fox.astype(bf16), w_out.astype(bf16))


def kernel(x, norm_g, w_in, b_f, b_gate, moba_qk_g, nsa_q_g, nsa_k_g, fox_qk_g, cmp_pe, cmp_w1, cmp_w2,
           w_up_moba, w_up_nsa, w_up_fox, w_out):
    B, S, _ = x.shape
    assert B == 1 and S % 2048 == 0 and S // MOBA_BLOCK <= CODE_BLOCKS
    depth = norm_g.shape[0]
    win_keys = CODE_BLOCKS * NSA_SLC_BLOCK
    n_q = S // FLASH_TQ
    causal = lambda a: {t: (2 if (t + 1) * FLASH_TK > a * FLASH_TQ else 1)
                        for t in range((a + 1) * FLASH_TQ // FLASH_TK)}
    band = lambda a: {t: 2 for t in range(max(0, a * FLASH_TQ - NSA_WINDOW + 1) // FLASH_TK,
                                          (a + 1) * FLASH_TQ // FLASH_TK)}
    tables = {"causal": _step_tables(n_q, causal, S), "slc": _step_tables(n_q, causal, win_keys),
              "win": _step_tables(n_q, band, S)}
    ncp = S // NSA_CMP_STRIDE
    cos_t, sin_t = _rope_tables(jnp.arange(S), S)
    cmp_end = jnp.arange(ncp - 1) * NSA_CMP_STRIDE + (NSA_CMP_LEN - 1)
    cos_c, sin_c = _rope_tables(cmp_end, ncp)
    rope = (cos_t, sin_t, cos_c, sin_c)
    h = x[0]
    for l in range(depth):
        h = _layer(h, norm_g[l], repack_w_in(w_in, l), b_f[l], b_gate[l], moba_qk_g[l], nsa_q_g[l], nsa_k_g[l], fox_qk_g[l],
                   cmp_pe[l], cmp_w1[l], cmp_w2[l], w_up_moba[l], w_up_nsa[l], w_up_fox[l], w_out[l], tables, rope)
    return h[None]
```

```python
import functools

import numpy as np
import jax
import jax.numpy as jnp
from jax import lax
from jax.experimental import pallas as pl
from jax.experimental.pallas import tpu as pltpu

D_MODEL = 1024
HEAD_DIM = 64
ROPE_DIM = HEAD_DIM // 4
ROPE_THETA = 500000.0
RMS_EPS = 1e-6
NEG_INF = -1e30
MASKED_BELOW = 0.5 * NEG_INF
M_FLOOR = -1e20

MOBA_HEADS = 6
MOBA_BLOCK = 256
MOBA_TOPK = 3
NSA_HEADS = 4
NSA_CMP_LEN = 32
NSA_CMP_STRIDE = 16
NSA_CMP_HIDDEN = 4 * HEAD_DIM
NSA_SLC_BLOCK = 64
NSA_SLC_TOPN = 16
NSA_WINDOW = 512
FOX_HEADS = 6
N_BRANCH = 3
MOBA_W = MOBA_HEADS * HEAD_DIM
NSA_W = NSA_HEADS * HEAD_DIM
FOX_W = FOX_HEADS * HEAD_DIM
IN_SPLITS = (MOBA_W,) * 4 + (NSA_W,) + (HEAD_DIM,) * 6 + (3 * NSA_HEADS, NSA_W) + (FOX_W,) * 3 + (FOX_HEADS, FOX_W, N_BRANCH * D_MODEL)
ATTN_SCALE = HEAD_DIM ** -0.5

LANES = 128
AUG = 2 * HEAD_DIM
CODE_BLOCKS = AUG - HEAD_DIM
Q_BLOCK = 128

SEG_A = 2 * MOBA_W + NSA_W + 2 * HEAD_DIM + 2 * FOX_W
SEG_B = 2 * MOBA_W + 4 * HEAD_DIM
SEG_C = MOBA_W + NSA_W + FOX_W
SEG_D = LANES
OFF_B = SEG_A
OFF_C = OFF_B + SEG_B
OFF_D = OFF_C + SEG_C
MAIN_COLS = OFF_D + SEG_D
FF_LANE = 3 * NSA_HEADS
KV_CMP_OFF = OFF_B + 2 * MOBA_W + 2 * HEAD_DIM
ALL_COLS = MAIN_COLS + N_BRANCH * D_MODEL

VMEM_LIMIT = 56 * 1024 * 1024
FLASH_QW = 512
FLASH_TQ = 512
FLASH_TK = 256
FLASH_SUB = 8
LOG2E = 1.4426950408889634
Q_SCALE = ATTN_SCALE * LOG2E
V_ROWS = 80

f32 = jnp.float32
bf16 = jnp.bfloat16


def _cparams(sem):
    return pltpu.CompilerParams(dimension_semantics=sem, vmem_limit_bytes=VMEM_LIMIT)


def _iota2(shape, dim):
    return lax.broadcasted_iota(jnp.int32, shape, dim)


def _div(x, d):
    return jnp.right_shift(x, int(d).bit_length() - 1)


def _mod(x, d):
    return jnp.bitwise_and(x, d - 1)


def _place(n_in, n_out, in_off, out_off, width=HEAD_DIM, val=1.0):
    r = _iota2((n_in, n_out), 0) - in_off
    c = _iota2((n_in, n_out), 1) - out_off
    hit = (r == c) & (r >= 0) & (r < width)
    return jnp.where(hit, val, 0.0).astype(bf16)


def _place_t(n_out, n_in, in_off, out_off, width=HEAD_DIM, val=1.0):
    r = _iota2((n_out, n_in), 0) - out_off
    c = _iota2((n_out, n_in), 1) - in_off
    hit = (r == c) & (r >= 0) & (r < width)
    return jnp.where(hit, val, 0.0).astype(bf16)


def _split3(x):
    hi = x.astype(bf16)
    r = x - hi.astype(f32)
    mid = r.astype(bf16)
    lo = (r - mid.astype(f32)).astype(bf16)
    return hi, mid, lo


def _heads_to_lanes(pieces):
    rows = len(pieces) * HEAD_DIM
    pad = [jnp.zeros((-rows % LANES, pieces[0].shape[1]), f32)] if rows % LANES else []
    return jnp.concatenate(list(pieces) + pad, axis=0).T[:, :rows]


def _dot(a, b):
    return jnp.dot(a, b, preferred_element_type=f32)


def _dot_nt(a, b):
    return lax.dot_general(a, b, (((1,), (1,)), ((), ())), preferred_element_type=f32)


def _w_in_plan():
    offs = np.concatenate([[0], np.cumsum(IN_SPLITS)])
    names = ("mq", "mk", "mv", "mz", "nq", "kc", "vc", "ksl", "vsl", "kw", "vw", "ng", "nz", "fq", "fk", "fv", "ff", "fz", "gl")
    start = {n: int(offs[j]) for j, n in enumerate(names)}
    width = {n: int(IN_SPLITS[j]) for j, n in enumerate(names)}
    order = ("mq", "mk", "nq", "ksl", "kw", "fq", "fk", "mv", "fv", "vsl", "vw", "kc", "vc", "mz", "nz", "fz", "ng", "ff")
    plan = [[] for _ in range(ALL_COLS // LANES)]
    new = 0
    for n in order + ("pad", "gl"):
        if n == "pad":
            new = MAIN_COLS
            continue
        src, left = start[n], width[n]
        while left > 0:
            w = min(left, LANES - new % LANES, LANES - src % LANES)
            plan[new // LANES].append((src // LANES, src % LANES, new % LANES, w))
            src, new, left = src + w, new + w, left - w
    assert new == ALL_COLS
    return plan


def _repack_kernel(w_ref, o_ref, *, plan, tr, n_cols):
    lane = _iota2((tr, LANES), 1)
    loaded = {}

    def source(a):
        if a not in loaded:
            x = w_ref[0, :, a * LANES:(a + 1) * LANES]
            if (a + 1) * LANES > n_cols:
                x = jnp.where(lane < n_cols - a * LANES, x, 0.0)
            loaded[a] = x.astype(bf16)
        return loaded[a]

    for b, pieces in enumerate(plan):
        acc = jnp.zeros((tr, LANES), f32)
        for a, lane_in, lane_out, w in pieces:
            acc = acc + _dot(source(a), _place(LANES, LANES, lane_in, lane_out, width=w))
        o_ref[:, b * LANES:(b + 1) * LANES] = acc.astype(bf16)


def repack_w_in(w_in, layer, tr=128):
    _, D, n_cols = w_in.shape
    padded = pl.cdiv(n_cols, LANES) * LANES
    return pl.pallas_call(
        functools.partial(_repack_kernel, plan=_w_in_plan(), tr=tr, n_cols=n_cols),
        grid=(D // tr,),
        in_specs=[pl.BlockSpec((1, tr, padded), lambda i: (layer, i, 0))],
        out_specs=pl.BlockSpec((tr, ALL_COLS), lambda i: (i, 0)),
        out_shape=jax.ShapeDtypeStruct((D, ALL_COLS), bf16),
        compiler_params=_cparams(("arbitrary",)),
        name="repack_w_in",
    )(w_in)


def _rms_matmul_kernel(x_ref, g_ref, w_ref, o_ref, h_sc):
    @pl.when(pl.program_id(1) == 0)
    def _():
        x = x_ref[...]
        ms = jnp.mean(x * x, axis=-1, keepdims=True)
        h_sc[...] = (x * lax.rsqrt(ms + RMS_EPS) * g_ref[...]).astype(bf16)

    o_ref[...] = _dot(h_sc[...], w_ref[...]).astype(o_ref.dtype)


def rms_matmul(x, g, w, col0, n_cols, dtype, tn, tm=1024):
    S, D = x.shape
    assert col0 % tn == 0 and n_cols % tn == 0
    return pl.pallas_call(
        _rms_matmul_kernel,
        grid=(S // tm, n_cols // tn),
        in_specs=[pl.BlockSpec((tm, D), lambda i, j: (i, 0)),
                  pl.BlockSpec((1, D), lambda i, j: (0, 0)),
                  pl.BlockSpec((D, tn), lambda i, j: (0, col0 // tn + j))],
        out_specs=pl.BlockSpec((tm, tn), lambda i, j: (i, j)),
        out_shape=jax.ShapeDtypeStruct((S, n_cols), dtype),
        scratch_shapes=[pltpu.VMEM((tm, D), bf16)],
        compiler_params=_cparams(("arbitrary", "arbitrary")),
        name="rms_matmul",
    )(x, g, w)


def _prep_kernel(p_ref, cos_ref, sin_ref, gain_ref, bf_ref,
                 mq_ref, mk_ref, nq_ref, ksl_ref, kw_ref, fq_ref, fk_ref,
                 mv_ref, fv_ref, vsl_ref, vw_ref, zs_ref, carry_sc, *, ts):
    i = pl.program_id(0)

    @pl.when(i == 0)
    def _():
        carry_sc[...] = jnp.zeros_like(carry_sc)

    lane = _iota2((ts, LANES), 1)
    pos = _iota2((ts, LANES), 0) + i * ts
    blockdiag = jnp.where(_div(_iota2((LANES, LANES), 0), HEAD_DIM) == _div(_iota2((LANES, LANES), 1), HEAD_DIM),
                          1.0, 0.0).astype(bf16)
    first_half = _mod(lane, HEAD_DIM) < (ROPE_DIM // 2)
    low_lanes = lane < HEAD_DIM
    cos = cos_ref[...]
    sin = sin_ref[...]

    def normed(c, rope, scale=None):
        x = p_ref[:, c * LANES:(c + 1) * LANES]
        x2 = x * x
        hi = x2.astype(bf16)
        lo = (x2 - hi.astype(f32)).astype(bf16)
        ss = _dot(hi, blockdiag) + _dot(lo, blockdiag)
        y = x * lax.rsqrt(ss * (1.0 / HEAD_DIM) + RMS_EPS) * gain_ref[:, c * LANES:(c + 1) * LANES]
        if rope:
            up = pltpu.roll(y, LANES - ROPE_DIM // 2, 1)
            dn = pltpu.roll(y, ROPE_DIM // 2, 1)
            y = y * cos + jnp.where(first_half, up, dn) * sin
        return y if scale is None else y * scale

    def head_rows(y_t, half):
        return y_t[half * HEAD_DIM:(half + 1) * HEAD_DIM, :]

    def head_lanes(y, half):
        return y if half == 0 else pltpu.roll(y, HEAD_DIM, 1)

    d = p_ref[:, OFF_D:OFF_D + LANES] + bf_ref[...]
    logf = jnp.minimum(d, 0.0) - jnp.log(1.0 + jnp.exp(-jnp.abs(d)))
    tri = jnp.where(_iota2((ts, ts), 1) <= _iota2((ts, ts), 0), 1.0, 0.0).astype(bf16)
    lh, lm, ll = _split3(logf)
    c = carry_sc[0:1, :] + (_dot(tri, lh) + _dot(tri, lm) + _dot(tri, ll))
    carry_sc[...] = jnp.broadcast_to(c[ts - 1:ts, :], carry_sc.shape)
    c2 = c * LOG2E
    pieces = [p.astype(f32) for p in _split3(c2)]
    pieces_t = [p.astype(f32) for p in _split3(c2.T)]
    row64 = _iota2((HEAD_DIM, ts), 0)

    def decay_rows(h):
        r = FF_LANE + h
        hi, mid, lo = (jnp.broadcast_to(p[r:r + 1, :], (HEAD_DIM, ts)) for p in pieces_t)
        return jnp.where(row64 == 0, hi, jnp.where(row64 == 1, mid, jnp.where(row64 == 2, lo,
                         jnp.where(row64 < 6, 1.0, 0.0))))

    def decay_lanes(h):
        r = FF_LANE + h
        hi, mid, lo = (jnp.broadcast_to(p[:, r:r + 1], (ts, LANES)) for p in pieces)
        return jnp.where(lane == HEAD_DIM + 3, -hi, jnp.where(lane == HEAD_DIM + 4, -mid,
                         jnp.where(lane == HEAD_DIM + 5, -lo, jnp.where(lane < HEAD_DIM + 3, 1.0, 0.0))))

    moba_code = jnp.where((lane - HEAD_DIM) == _div(pos, MOBA_BLOCK), 1.0, 0.0)
    for c_i in range(3):
        yq_t = normed(c_i, True, Q_SCALE).T
        yk = normed(3 + c_i, True)
        for half in range(2):
            h = 2 * c_i + half
            mq_ref[h] = head_rows(yq_t, half).astype(bf16)
            mk_ref[h] = jnp.where(low_lanes, head_lanes(yk, half), moba_code).astype(bf16)
    for c_i in range(2):
        y_t = normed(6 + c_i, True, Q_SCALE).T
        for half in range(2):
            h = 2 * c_i + half
            yh = head_rows(y_t, half).astype(bf16)
            for qb in range(ts // Q_BLOCK):
                nq_ref[:, (qb * NSA_HEADS + h) * Q_BLOCK:(qb * NSA_HEADS + h + 1) * Q_BLOCK] = (
                    yh[:, qb * Q_BLOCK:(qb + 1) * Q_BLOCK])
    y = normed(8, True)
    slc_code = jnp.where((lane - HEAD_DIM) == _mod(_div(pos, NSA_SLC_BLOCK), CODE_BLOCKS), 1.0, 0.0)
    ksl_ref[...] = jnp.where(low_lanes, y, slc_code).astype(bf16)
    kw_ref[...] = jnp.where(low_lanes, head_lanes(y, 1), 0.0).astype(bf16)
    for c_i in range(3):
        yq_t = normed(9 + c_i, False, Q_SCALE).T
        yk = normed(12 + c_i, False)
        for half in range(2):
            h = 2 * c_i + half
            fq_ref[h] = jnp.concatenate([head_rows(yq_t, half), decay_rows(h)], axis=0).astype(bf16)
            fk_ref[h] = jnp.where(low_lanes, head_lanes(yk, half), decay_lanes(h)).astype(bf16)
    ones_rows = jnp.where(_iota2((V_ROWS - HEAD_DIM, ts), 0) == 0, 1.0, 0.0)

    def value_rows(x_t, half):
        return jnp.concatenate([head_rows(x_t, half), ones_rows], axis=0).astype(bf16)

    for c_i in range(3):
        xm_t = p_ref[:, OFF_B + c_i * LANES:OFF_B + (c_i + 1) * LANES].T
        xf_t = p_ref[:, OFF_B + MOBA_W + c_i * LANES:OFF_B + MOBA_W + (c_i + 1) * LANES].T
        for half in range(2):
            mv_ref[2 * c_i + half] = value_rows(xm_t, half)
            fv_ref[2 * c_i + half] = value_rows(xf_t, half)
    xs_t = p_ref[:, OFF_B + 2 * MOBA_W:OFF_B + 2 * MOBA_W + LANES].T
    vsl_ref[...] = value_rows(xs_t, 0)
    vw_ref[...] = value_rows(xs_t, 1)
    z = p_ref[:, OFF_C:OFF_C + SEG_C]
    zs_ref[...] = (z * (1.0 / (1.0 + jnp.exp(-z)))).astype(bf16)


def prep(proj, cos_t, sin_t, gain_row, bf_row, ts=512):
    S = proj.shape[0]
    head128 = lambda n: jax.ShapeDtypeStruct((n, S, AUG), bf16)
    spec_h = lambda n, w: pl.BlockSpec((n, ts, w), lambda i: (0, i, 0))
    spec_r = lambda w: pl.BlockSpec((ts, w), lambda i: (i, 0))
    head_t = lambda n, r: jax.ShapeDtypeStruct((n, r, S), bf16)
    spec_ht = lambda n, r: pl.BlockSpec((n, r, ts), lambda i: (0, 0, i))
    spec_t = pl.BlockSpec((V_ROWS, ts), lambda i: (0, i))
    out_shape = (head_t(MOBA_HEADS, HEAD_DIM), head128(MOBA_HEADS),
                 jax.ShapeDtypeStruct((HEAD_DIM, NSA_HEADS * S), bf16),
                 jax.ShapeDtypeStruct((S, AUG), bf16), jax.ShapeDtypeStruct((S, AUG), bf16),
                 head_t(FOX_HEADS, AUG), head128(FOX_HEADS),
                 head_t(MOBA_HEADS, V_ROWS), head_t(FOX_HEADS, V_ROWS),
                 jax.ShapeDtypeStruct((V_ROWS, S), bf16), jax.ShapeDtypeStruct((V_ROWS, S), bf16),
                 jax.ShapeDtypeStruct((S, SEG_C), bf16))
    out_specs = (spec_ht(MOBA_HEADS, HEAD_DIM), spec_h(MOBA_HEADS, AUG),
                 pl.BlockSpec((HEAD_DIM, NSA_HEADS * ts), lambda i: (0, i)),
                 spec_r(AUG), spec_r(AUG),
                 spec_ht(FOX_HEADS, AUG), spec_h(FOX_HEADS, AUG),
                 spec_ht(MOBA_HEADS, V_ROWS), spec_ht(FOX_HEADS, V_ROWS),
                 spec_t, spec_t,
                 spec_r(SEG_C))
    return pl.pallas_call(
        functools.partial(_prep_kernel, ts=ts),
        grid=(S // ts,),
        in_specs=[pl.BlockSpec((ts, MAIN_COLS), lambda i: (i, 0)),
                  spec_r(LANES), spec_r(LANES),
                  pl.BlockSpec((1, SEG_A), lambda i: (0, 0)),
                  pl.BlockSpec((1, LANES), lambda i: (0, 0))],
        out_specs=out_specs,
        out_shape=out_shape,
        scratch_shapes=[pltpu.VMEM((8, LANES), f32)],
        compiler_params=_cparams(("arbitrary",)),
        name="prep",
    )(proj, cos_t, sin_t, gain_row, bf_row)


def _kmean_kernel(k_ref, o_ref, *, rows):
    n = rows // MOBA_BLOCK
    avg = jnp.where(_div(_iota2((n, rows), 1), MOBA_BLOCK) == _iota2((n, rows), 0),
                    1.0 / MOBA_BLOCK, 0.0).astype(bf16)
    o_ref[0] = _dot(avg, k_ref[0])[:, :HEAD_DIM]


def moba_kmean(mk_aug):
    H, S, _ = mk_aug.shape
    rows = 8 * MOBA_BLOCK
    return pl.pallas_call(
        functools.partial(_kmean_kernel, rows=rows),
        grid=(H, S // rows),
        in_specs=[pl.BlockSpec((1, rows, AUG), lambda h, i: (h, i, 0))],
        out_specs=pl.BlockSpec((1, 8, HEAD_DIM), lambda h, i: (h, i, 0)),
        out_shape=jax.ShapeDtypeStruct((H, CODE_BLOCKS, HEAD_DIM), f32),
        compiler_params=_cparams(("arbitrary", "arbitrary")),
        name="moba_kmean",
    )(mk_aug)


def _top_select(scores, idx, n_pick, floor):
    big = jnp.int32(2 ** 30)
    scores = list(scores)
    for _ in range(n_pick):
        for j, score in enumerate(scores):
            m = jnp.max(score, axis=0, keepdims=True)
            first = jnp.min(jnp.where(score == m, idx, big), axis=0, keepdims=True)
            first = jnp.where(m > floor, first, big)
            scores[j] = jnp.where(idx == first, -jnp.inf, score)
    return [s == -jnp.inf for s in scores]


def _moba_select_kernel(q_ref, km_ref, o_ref, *, tq, n_heads):
    i = pl.program_id(0)
    blk = _iota2((CODE_BLOCKS, tq), 0)
    cur = _div(_iota2((CODE_BLOCKS, tq), 1) + i * tq, MOBA_BLOCK)
    scores = []
    for h in range(n_heads):
        q = q_ref[h]
        km = km_ref[h]
        km_hi = km.astype(bf16)
        km_lo = (km - km_hi.astype(f32)).astype(bf16)
        gate = _dot(km_hi, q) + _dot(km_lo, q)
        scores.append(jnp.where(blk < cur, gate, NEG_INF))
    for h, sel in enumerate(_top_select(scores, blk, MOBA_TOPK, MASKED_BELOW)):
        o_ref[h, 0, 0:HEAD_DIM, :] = q_ref[h]
        o_ref[h, 0, HEAD_DIM:AUG, :] = jnp.where(sel | (blk == cur), 0.0, NEG_INF).astype(bf16)


def moba_select(mq, kmean, tq=512):
    H, _, S = mq.shape
    return pl.pallas_call(
        functools.partial(_moba_select_kernel, tq=tq, n_heads=H),
        grid=(S // tq,),
        in_specs=[pl.BlockSpec((H, HEAD_DIM, tq), lambda i: (0, 0, i)),
                  pl.BlockSpec((H, CODE_BLOCKS, HEAD_DIM), lambda i: (0, 0, 0))],
        out_specs=pl.BlockSpec((H, 1, AUG, tq), lambda i: (0, 0, 0, i)),
        out_shape=jax.ShapeDtypeStruct((H, 1, AUG, S), bf16),
        compiler_params=_cparams(("arbitrary",)),
        name="moba_select",
    )(mq, kmean)


def _cmp_kernel(x_ref, pe_ref, w1_ref, w2_ref, gain_ref, cos_ref, sin_ref, kc_ref, vc_ref, *, ncp):
    top = jnp.zeros((ncp, 2 * NSA_CMP_HIDDEN), f32)
    nxt = jnp.zeros((ncp, 2 * NSA_CMP_HIDDEN), f32)
    for j in range(NSA_CMP_STRIDE):
        xj = x_ref[pl.ds(j, ncp, stride=NSA_CMP_STRIDE), :]
        top = top + _dot((xj + pe_ref[j:j + 1, :]).astype(bf16), w1_ref[j])
        nxt = nxt + _dot((xj + pe_ref[NSA_CMP_STRIDE + j:NSA_CMP_STRIDE + j + 1, :]).astype(bf16),
                         w1_ref[NSA_CMP_STRIDE + j])
    hid = top + pltpu.roll(nxt, ncp - 1, 0)
    act = (hid * (1.0 / (1.0 + jnp.exp(-hid)))).astype(bf16)
    k = _dot(act[:, :NSA_CMP_HIDDEN], w2_ref[0])
    v = _dot(act[:, NSA_CMP_HIDDEN:], w2_ref[1])
    ms = jnp.sum(k * k, axis=-1, keepdims=True) * (1.0 / HEAD_DIM)
    y = k * lax.rsqrt(ms + RMS_EPS) * gain_ref[...]
    lane = _iota2((ncp, LANES), 1)
    up = pltpu.roll(y, LANES - ROPE_DIM // 2, 1)
    dn = pltpu.roll(y, ROPE_DIM // 2, 1)
    y = y * cos_ref[...] + jnp.where(lane < ROPE_DIM // 2, up, dn) * sin_ref[...]
    kc_ref[...] = y[:, :HEAD_DIM].astype(bf16)
    vc_ref[...] = _dot_nt(_place_t(V_ROWS, LANES, 0, 0), v.astype(bf16)).astype(bf16)


def nsa_compress(proj, pe, w1, w2, gain, cos_c, sin_c):
    S = proj.shape[0]
    ncp = S // NSA_CMP_STRIDE
    full = lambda a: pl.BlockSpec(a.shape, lambda i: (0,) * a.ndim)
    return pl.pallas_call(
        functools.partial(_cmp_kernel, ncp=ncp),
        grid=(1,),
        in_specs=[pl.BlockSpec((S, LANES), lambda i: (0, KV_CMP_OFF // LANES)),
                  full(pe), full(w1), full(w2), full(gain), full(cos_c), full(sin_c)],
        out_specs=(pl.BlockSpec((ncp, HEAD_DIM), lambda i: (0, 0)), pl.BlockSpec((V_ROWS, ncp), lambda i: (0, 0))),
        out_shape=(jax.ShapeDtypeStruct((ncp, HEAD_DIM), bf16), jax.ShapeDtypeStruct((V_ROWS, ncp), bf16)),
        compiler_params=_cparams(("arbitrary",)),
        name="nsa_compress",
    )(proj, pe, w1, w2, gain, cos_c, sin_c)


def _nsa_cmp_select_kernel(q_ref, kc_ref, vc_ref, oc_ref, qa_ref, imp_sc, *, ncp, nsp, n_win, n_qb):
    i = pl.program_id(0)
    cols = NSA_HEADS * Q_BLOCK
    blk = _iota2((nsp, Q_BLOCK), 0)

    def cur(b):
        return _div(_iota2((nsp, Q_BLOCK), 1) + (i * n_qb + b) * Q_BLOCK, NSA_SLC_BLOCK)

    def forced(b):
        return (blk == 0) | (blk == cur(b)) | (blk == cur(b) - 1)

    n_split = 4 if ncp % (4 * LANES) == 0 else 1
    lo = 0
    for part in range(n_split):
        nk = ncp * (part + 1) // n_split
        hi = nk * NSA_CMP_STRIDE // (n_qb * Q_BLOCK)
        pl.when((i >= lo) & (i < hi))(functools.partial(
            _cmp_attention, q_ref, kc_ref, vc_ref, oc_ref, imp_sc, i, cur, forced, blk, nk=nk, nsp=nsp, n_qb=n_qb))
        lo = hi

    n_free = NSA_SLC_TOPN - 3
    for b, sel in enumerate(_top_select([imp_sc[b] for b in range(n_qb)], blk, n_free, MASKED_BELOW)):
        bias = jnp.where(sel | forced(b), 0.0, NEG_INF).astype(bf16)
        for w in range(n_win):
            bw = bias[w * CODE_BLOCKS:(w + 1) * CODE_BLOCKS, :]
            qa_ref[0, w, 0:HEAD_DIM, b * cols:(b + 1) * cols] = q_ref[:, b * cols:(b + 1) * cols]
            qa_ref[0, w, HEAD_DIM:AUG, b * cols:(b + 1) * cols] = jnp.concatenate([bw] * NSA_HEADS, axis=1)


def _cmp_attention(q_ref, kc_ref, vc_ref, oc_ref, imp_sc, i, cur, forced, blk, *, nk, nsp, n_qb):
    cols = NSA_HEADS * Q_BLOCK
    kc = kc_ref[0:nk, :]
    vc = vc_ref[:, 0:nk]
    c0 = _iota2((nsp, nk), 1) * NSA_CMP_STRIDE
    b0 = _iota2((nsp, nk), 0) * NSA_SLC_BLOCK
    overlap = jnp.where((c0 <= b0 + (NSA_SLC_BLOCK - 1)) & (c0 + (NSA_CMP_LEN - 1) >= b0), 1.0, 0.0).astype(bf16)
    kend = _iota2((nk, cols), 0) * NSA_CMP_STRIDE + (NSA_CMP_LEN - 1)
    col_pos = _mod(_iota2((nk, cols), 1), Q_BLOCK)

    def scores(b):
        return _dot(kc, q_ref[:, b * cols:(b + 1) * cols])

    def probs(b, s):
        ok = kend <= (i * n_qb + b) * Q_BLOCK + col_pos
        sm = jnp.where(ok, s, NEG_INF)
        m = jnp.max(sm, axis=0, keepdims=True)
        e = jnp.exp2(sm - m)
        return e * jnp.where(m > MASKED_BELOW, 1.0 / jnp.sum(e, axis=0, keepdims=True), 0.0)

    def importance(b, p):
        o = _dot(vc, p.astype(bf16))
        oc_ref[b * Q_BLOCK:(b + 1) * Q_BLOCK, :] = _heads_to_lanes(
            [o[0:HEAD_DIM, h * Q_BLOCK:(h + 1) * Q_BLOCK] for h in range(NSA_HEADS)]).astype(bf16)
        psum = (p[:, 0:Q_BLOCK] + p[:, Q_BLOCK:2 * Q_BLOCK]
                + p[:, 2 * Q_BLOCK:3 * Q_BLOCK] + p[:, 3 * Q_BLOCK:4 * Q_BLOCK])
        ph, pm, plo = _split3(psum)
        imp = _dot(overlap, ph) + _dot(overlap, pm) + _dot(overlap, plo)
        return jnp.where((blk <= cur(b)) & jnp.logical_not(forced(b)), imp, NEG_INF)

    s_next = scores(0)
    for b in range(n_qb):
        s = s_next
        if b + 1 < n_qb:
            s_next = scores(b + 1)
        imp_sc[b] = importance(b, probs(b, s))


def nsa_cmp_select(nq, kc, vc, S, n_qb=4):
    ncp = kc.shape[0]
    ns = S // NSA_SLC_BLOCK
    nsp = max(LANES, ns)
    n_win = max(1, ns // CODE_BLOCKS)
    cols = n_qb * NSA_HEADS * Q_BLOCK
    return pl.pallas_call(
        functools.partial(_nsa_cmp_select_kernel, ncp=ncp, nsp=nsp, n_win=n_win, n_qb=n_qb),
        grid=(S // (n_qb * Q_BLOCK),),
        in_specs=[pl.BlockSpec((HEAD_DIM, cols), lambda i: (0, i)),
                  pl.BlockSpec((ncp, HEAD_DIM), lambda i: (0, 0)),
                  pl.BlockSpec((V_ROWS, ncp), lambda i: (0, 0))],
        out_specs=(pl.BlockSpec((n_qb * Q_BLOCK, NSA_W), lambda i: (i, 0)),
                   pl.BlockSpec((1, n_win, AUG, cols), lambda i: (0, 0, 0, i))),
        out_shape=(jax.ShapeDtypeStruct((S, NSA_W), bf16),
                   jax.ShapeDtypeStruct((1, n_win, AUG, NSA_HEADS * S), bf16)),
        scratch_shapes=[pltpu.VMEM((n_qb, nsp, Q_BLOCK), f32)],
        compiler_params=_cparams(("arbitrary",)),
        name="nsa_cmp_select",
    )(nq, kc, vc)


def _flash_kernel(qi_t, kb_t, win_t, first_t, last_t, var_t, q_ref, k_ref, v_ref, o_ref, m_sc, acc_sc,
                  *, n_heads, tq, qw, tq_pos, rep_cols, pos_block, window, variants):
    p_id = pl.program_id(0)
    qi = qi_t[p_id]
    kb = kb_t[p_id]

    @pl.when(first_t[p_id] == 1)
    def _():
        m_sc[...] = jnp.full(m_sc.shape, M_FLOOR, f32)
        acc_sc[...] = jnp.zeros_like(acc_sc)

    def tile(modes):
        chains = [(g, slice(c * qw, (c + 1) * qw), slice(kt * FLASH_TK, (kt + 1) * FLASH_TK))
                  for kt in range(FLASH_SUB) if modes[kt] for g in range(n_heads) for c in range(tq // qw)]
        if 2 in modes:
            rel = _mod(_iota2((FLASH_TK, qw), 1), pos_block) - _iota2((FLASH_TK, qw), 0)

        def scores(t):
            g, cols, keys = chains[t]
            s = _dot(k_ref[g, keys, :], q_ref[g, 0, :, cols])
            if modes[keys.start // FLASH_TK] == 1:
                return s
            base = (qi * tq_pos + (cols.start // rep_cols) * pos_block + cols.start % pos_block
                    - kb * (FLASH_SUB * FLASH_TK) - keys.start)
            delta = rel + base
            ok = (delta >= 0) if window is None else (delta.astype(jnp.uint32) < jnp.uint32(window))
            return jnp.where(ok, s, NEG_INF)

        def softmax(t, s):
            g, cols, _ = chains[t]
            m_prev = m_sc[g, 0:1, cols]
            m_new = jnp.maximum(m_prev, jnp.max(s, axis=0, keepdims=True))
            m_sc[g, :, cols] = jnp.broadcast_to(m_new, (m_sc.shape[1], qw))
            return jnp.exp2(s.astype(bf16) - m_new.astype(bf16)), jnp.exp2(m_prev - m_new)

        def accumulate(t, p, alpha):
            g, cols, keys = chains[t]
            acc_sc[g, :, cols] = alpha * acc_sc[g, :, cols] + _dot(v_ref[g, :, keys], p)

        n = len(chains)
        s_next, staged = scores(0), None
        for t in range(n + 1):
            s = s_next
            if t + 1 < n:
                s_next = scores(t + 1)
            if staged is not None:
                accumulate(t - 1, *staged)
            staged = softmax(t, s) if t < n else None

    for vid, modes in enumerate(variants):
        pl.when(var_t[p_id] == vid)(functools.partial(tile, modes))

    @pl.when(last_t[p_id] == 1)
    def _():
        def normalised(g, cols):
            acc = acc_sc[g, :, cols]
            l = acc[HEAD_DIM:HEAD_DIM + 1, :]
            return acc[0:HEAD_DIM, :] * jnp.where(l > 0.0, 1.0 / l, 0.0)

        if n_heads > 1:
            o_ref[...] = _heads_to_lanes([normalised(g, slice(0, tq)) for g in range(n_heads)]).astype(o_ref.dtype)
        else:
            per_block = tq // tq_pos * pos_block
            for qb in range(tq // per_block):
                pieces = [normalised(0, slice(qb * per_block + h * pos_block, qb * per_block + (h + 1) * pos_block))
                          for h in range(per_block // pos_block)]
                o_ref[qb * pos_block:(qb + 1) * pos_block, :] = _heads_to_lanes(pieces).astype(o_ref.dtype)


def _step_tables(n_q, key_tiles, win_keys):
    variants, cols = [], ([], [], [], [], [], [])
    for a in range(n_q):
        tiles = key_tiles(a)
        groups = sorted({t // FLASH_SUB for t in tiles})
        for kb in groups:
            modes = tuple(tiles.get(kb * FLASH_SUB + j, 0) for j in range(FLASH_SUB))
            if modes not in variants:
                variants.append(modes)
            row = (a, kb, (kb * FLASH_SUB * FLASH_TK) // win_keys, int(kb == groups[0]), int(kb == groups[-1]),
                   variants.index(modes))
            for c, v in zip(cols, row):
                c.append(v)
    return tuple(jnp.asarray(np.asarray(c, np.int32)) for c in cols), tuple(variants)


def flash(q, k, v, steps, *, tq, pos_block, window, name):
    tables, variants = steps
    G, _, _, Sq = q.shape
    tq_pos = FLASH_TQ
    rep_cols = (tq // tq_pos) * pos_block
    assert rep_cols % FLASH_QW == 0 and (pos_block % FLASH_QW == 0 or FLASH_QW % pos_block == 0)
    tk = FLASH_SUB * FLASH_TK
    width = max(G, tq // tq_pos) * HEAD_DIM
    assert G == 1 or tq == tq_pos
    grid_spec = pltpu.PrefetchScalarGridSpec(
        num_scalar_prefetch=6,
        grid=(tables[0].shape[0],),
        in_specs=[pl.BlockSpec((G, 1, AUG, tq), lambda p, qi, kb, wi, fi, la, va: (0, wi[p], 0, qi[p])),
                  pl.BlockSpec((G, tk, AUG), lambda p, qi, kb, wi, fi, la, va: (0, kb[p], 0)),
                  pl.BlockSpec((G, V_ROWS, tk), lambda p, qi, kb, wi, fi, la, va: (0, 0, kb[p]))],
        out_specs=pl.BlockSpec((tq_pos, width), lambda p, qi, kb, wi, fi, la, va: (qi[p], 0)),
        scratch_shapes=[pltpu.VMEM((G, 8, tq), f32), pltpu.VMEM((G, V_ROWS, tq), f32)],
    )
    return pl.pallas_call(
        functools.partial(_flash_kernel, n_heads=G, tq=tq, qw=FLASH_QW, tq_pos=tq_pos, rep_cols=rep_cols,
                          pos_block=pos_block, window=window, variants=variants),
        grid_spec=grid_spec,
        out_shape=jax.ShapeDtypeStruct((Sq // (tq // tq_pos), width), bf16),
        compiler_params=_cparams(("arbitrary",)),
        name=name,
    )(*tables, q, k, v)


def _out_kernel(x_ref, gl0_ref, gl1_ref, gl2_ref, sd_ref, bg_ref, om_ref, of_ref, oc_ref, os_ref, ow_ref, zs_ref,
                wm_ref, wn_ref, wf_ref, wo_ref, o_ref):
    zs = zs_ref[...].astype(f32)
    sg = 1.0 / (1.0 + jnp.exp(-sd_ref[...]))
    sg_hi = sg.astype(bf16)
    sg_lo = (sg - sg_hi.astype(f32)).astype(bf16)
    src = _iota2((LANES, NSA_W), 0)
    head = _div(_iota2((LANES, NSA_W), 1), HEAD_DIM)
    o_n = jnp.zeros(oc_ref.shape, f32)
    for branch, ref in enumerate((oc_ref, os_ref, ow_ref)):
        spread = jnp.where(src == 3 * head + branch, 1.0, 0.0).astype(bf16)
        o_n = o_n + (_dot(sg_hi, spread) + _dot(sg_lo, spread)) * ref[...].astype(f32)
    mixers = ((om_ref[...].astype(f32), 0, wm_ref), (o_n, MOBA_W, wn_ref), (of_ref[...].astype(f32), MOBA_W + NSA_W, wf_ref))
    merged = jnp.zeros(x_ref.shape, f32)
    for (o, z_off, w_ref), gl_ref, b in zip(mixers, (gl0_ref, gl1_ref, gl2_ref), range(N_BRANCH)):
        y = _dot((o * zs[:, z_off:z_off + o.shape[1]]).astype(bf16), w_ref[...])
        gate = 1.0 / (1.0 + jnp.exp(-(gl_ref[...].astype(f32) + bg_ref[:, b * D_MODEL:(b + 1) * D_MODEL])))
        merged = merged + y * gate
    o_ref[...] = x_ref[...] + _dot(merged.astype(bf16), wo_ref[...])


def out_proj(x, proj, gate_logits, b_gate, o_m, o_f, o_c, o_s, o_w, zs, w_m, w_n, w_f, w_o, tm=512):
    S = x.shape[0]
    full = lambda a: pl.BlockSpec(a.shape, lambda i: (0,) * a.ndim)
    rows = lambda w: pl.BlockSpec((tm, w), lambda i: (i, 0))
    gl_spec = lambda b: pl.BlockSpec((tm, D_MODEL), lambda i: (i, b))
    return pl.pallas_call(
        _out_kernel,
        grid=(S // tm,),
        in_specs=[rows(D_MODEL), gl_spec(0), gl_spec(1), gl_spec(2),
                  pl.BlockSpec((tm, LANES), lambda i: (i, OFF_D // LANES)),
                  full(b_gate), rows(MOBA_W), rows(FOX_W), rows(NSA_W), rows(NSA_W), rows(NSA_W), rows(SEG_C),
                  full(w_m), full(w_n), full(w_f), full(w_o)],
        out_specs=rows(D_MODEL),
        out_shape=jax.ShapeDtypeStruct((S, D_MODEL), f32),
        compiler_params=_cparams(("arbitrary",)),
        name="out_proj",
    )(x, gate_logits, gate_logits, gate_logits, proj, b_gate, o_m, o_f, o_c, o_s, o_w, zs, w_m, w_n, w_f, w_o)


def _rope_tables(pos, rows):
    half = ROPE_DIM // 2
    inv = ROPE_THETA ** (-jnp.arange(0, ROPE_DIM, 2, dtype=f32) / ROPE_DIM)
    d = np.arange(LANES) % HEAD_DIM
    rotated = d < ROPE_DIM
    inv_row = jnp.where(rotated, inv[d % half], 0.0)
    sign_row = np.where(d < half, -1.0, 1.0).astype(np.float32)
    ang = pos.astype(f32)[:, None] * inv_row[None, :]
    pad = ((0, rows - pos.shape[0]), (0, 0))
    return jnp.pad(jnp.cos(ang), pad), jnp.pad(jnp.sin(ang) * sign_row[None, :], pad)


def _layer(x, norm_g, w_main, b_f, b_gate, moba_qk_g, nsa_q_g, nsa_k_g, fox_qk_g,
           cmp_pe, cmp_w1, cmp_w2, w_up_moba, w_up_nsa, w_up_fox, w_out, tables, rope):
    S = x.shape[0]
    g_row = norm_g.reshape(1, D_MODEL)
    proj = rms_matmul(x, g_row, w_main, 0, MAIN_COLS, f32, tn=2 * D_MODEL)
    gate_logits = rms_matmul(x, g_row, w_main, MAIN_COLS, N_BRANCH * D_MODEL, bf16, tn=D_MODEL)

    gain_row = jnp.concatenate([jnp.tile(moba_qk_g[0], MOBA_HEADS), jnp.tile(moba_qk_g[1], MOBA_HEADS),
                                jnp.tile(nsa_q_g, NSA_HEADS), nsa_k_g[1], nsa_k_g[2],
                                jnp.tile(fox_qk_g[0], FOX_HEADS), jnp.tile(fox_qk_g[1], FOX_HEADS)]).reshape(1, SEG_A)
    bf_row = jnp.zeros((1, LANES), f32).at[0, FF_LANE:FF_LANE + FOX_HEADS].set(b_f)
    (mq, mk, nq, ksl, kw, fq, fk, mv, fv, vsl, vw, zs) = prep(proj, rope[0], rope[1], gain_row, bf_row)

    q_moba = moba_select(mq, moba_kmean(mk))
    o_m = flash(q_moba, mk, mv, tables["causal"], tq=FLASH_TQ, pos_block=FLASH_TQ, window=None, name="flash_moba")

    pe = jnp.concatenate([cmp_pe[0], cmp_pe[1]], axis=1)
    w1 = cmp_w1.astype(bf16).reshape(2, NSA_CMP_LEN, HEAD_DIM, NSA_CMP_HIDDEN)
    zero = jnp.zeros_like(w1[0])
    w1 = jnp.concatenate([jnp.concatenate([w1[0], zero], axis=2),
                          jnp.concatenate([zero, w1[1]], axis=2)], axis=1)
    w2p = jnp.pad(cmp_w2, ((0, 0), (0, 0), (0, LANES - HEAD_DIM))).astype(bf16)
    gain_c = jnp.pad(nsa_k_g[0], (0, LANES - HEAD_DIM)).reshape(1, LANES)
    kc, vc = nsa_compress(proj, pe, w1, w2p, gain_c, rope[2], rope[3])
    o_c, q_slc = nsa_cmp_select(nq, kc, vc, S)
    nsa_cols = NSA_HEADS * FLASH_TQ
    o_s = flash(q_slc, ksl[None], vsl[None], tables["slc"], tq=nsa_cols,
                pos_block=Q_BLOCK, window=None, name="flash_nsa_slc")
    o_w = flash(q_slc, kw[None], vw[None], tables["win"], tq=nsa_cols,
                pos_block=Q_BLOCK, window=NSA_WINDOW, name="flash_nsa_win")

    o_f = flash(fq[:, None], fk, fv, tables["causal"], tq=FLASH_TQ, pos_block=FLASH_TQ, window=None, name="flash_fox")
    return out_proj(x, proj, gate_logits, b_gate.reshape(1, N_BRANCH * D_MODEL), o_m, o_f, o_c, o_s, o_w, zs,
                    w_up_moba.astype(bf16), w_up_nsa.astype(bf16), w_up_fox.astype(bf16), w_out.astype(bf16))


def kernel(x, norm_g, w_in, b_f, b_gate, moba_qk_g, nsa_q_g, nsa_k_g, fox_qk_g, cmp_pe, cmp_w1, cmp_w2,
           w_up_moba, w_up_nsa, w_up_fox, w_out):
    B, S, _ = x.shape
    assert B == 1 and S % 2048 == 0 and S // MOBA_BLOCK <= CODE_BLOCKS
    depth = norm_g.shape[0]
    win_keys = CODE_BLOCKS * NSA_SLC_BLOCK
    n_q = S // FLASH_TQ
    causal = lambda a: {t: (2 if (t + 1) * FLASH_TK > a * FLASH_TQ else 1)
                        for t in range((a + 1) * FLASH_TQ // FLASH_TK)}
    band = lambda a: {t: 2 for t in range(max(0, a * FLASH_TQ - NSA_WINDOW + 1) // FLASH_TK,
                                          (a + 1) * FLASH_TQ // FLASH_TK)}
    tables = {"causal": _step_tables(n_q, causal, S), "slc": _step_tables(n_q, causal, win_keys),
              "win": _step_tables(n_q, band, S)}
    ncp = S // NSA_CMP_STRIDE
    cos_t, sin_t = _rope_tables(jnp.arange(S), S)
    cmp_end = jnp.arange(ncp - 1) * NSA_CMP_STRIDE + (NSA_CMP_LEN - 1)
    cos_c, sin_c = _rope_tables(cmp_end, ncp)
    rope = (cos_t, sin_t, cos_c, sin_c)
    h = x[0]
    for l in range(depth):
        h = _layer(h, norm_g[l], repack_w_in(w_in, l), b_f[l], b_gate[l], moba_qk_g[l], nsa_q_g[l], nsa_k_g[l], fox_qk_g[l],
                   cmp_pe[l], cmp_w1[l], cmp_w2[l], w_up_moba[l], w_up_nsa[l], w_up_fox[l], w_out[l], tables, rope)
    return h[None]
```
